```python
import math
import jax, jax.numpy as jnp
from jax import lax
import numpy as np

D_MODEL = 1024
BATCH = 32
SEQ = 2048
DEPTH = 2
DEC_BATCH = 16
DEC_SEQ = 64
PAST_LEN = 2048

CHUNK = 64
N_META = 16
RMS_EPS = 1e-6
SSD_HEADS = 16
SSD_HEAD_DIM = 64
SSD_GROUPS = 2
SSD_HPG = SSD_HEADS // SSD_GROUPS
SSD_STATE = 64
SSD_WIDTH = SSD_HEADS * SSD_HEAD_DIM
CONV_W = 4
CONV_DIM = SSD_WIDTH + 2 * SSD_GROUPS * SSD_STATE
S5_WIDTH = D_MODEL // 2
S5_GROUP_CH = 16
S5_GROUPS = S5_WIDTH // S5_GROUP_CH
S5_STATE = 64
SB_HEADS = 8
SB_HEAD_DIM = 64
SB_WIDTH = SB_HEADS * SB_HEAD_DIM
SB_BLOCK = 128
SB_SCALE = 1.0 / math.sqrt(SB_HEAD_DIM)
N_BRANCH = 3
D_FF = 4 * D_MODEL
OFF_Z = 0
OFF_XBC = OFF_Z + SSD_WIDTH
OFF_DT = OFF_XBC + CONV_DIM
OFF_U = OFF_DT + SSD_HEADS
OFF_Q = OFF_U + S5_WIDTH
OFF_K = OFF_Q + SB_WIDTH
OFF_V = OFF_K + SB_WIDTH
OFF_GATE = OFF_V + SB_WIDTH
IN_COLS = OFF_GATE + N_BRANCH * D_MODEL

kernel_name = 'hybrid_ssd_s5_stickbreak_stream_step'


def rmsnorm(x, g):
    xf = x.astype(jnp.float32)
    return (xf * lax.rsqrt(jnp.mean(xf * xf, axis=-1, keepdims=True) + RMS_EPS)).astype(x.dtype) * g


def causal_dwconv(x, hist, w, b):
    xp = jnp.concatenate([hist.astype(x.dtype), x], axis=1)
    T = x.shape[1]
    y = b
    for k in range(CONV_W):
        y = y + xp[:, k:k + T] * w[k]
    return y, xp[:, -(CONV_W - 1):]


def ssd_scan(x, dt, a, bm, cm, s0):
    bsz, T = x.shape[:2]
    nc = T // CHUNK
    blk = lambda t: t.reshape((bsz, nc, CHUNK) + t.shape[2:])
    x, dt, bm, cm = blk(x), blk(dt), blk(bm), blk(cm)
    a_cum = jnp.cumsum(dt * a, axis=2)
    seg = a_cum[:, :, :, None] - a_cum[:, :, None]
    causal = jnp.tril(jnp.ones((CHUNK, CHUNK), bool))[:, :, None, None]
    decay = jnp.exp(jnp.where(causal, seg, -jnp.inf))
    cb = jnp.einsum('bcign,bcjgn->bcijg', cm, bm)
    y_diag = jnp.einsum('bcijgh,bcjghp->bcighp', cb[..., None] * decay * dt[:, :, None], x)
    to_end = jnp.exp(a_cum[:, :, -1:] - a_cum) * dt
    blk_states = jnp.einsum('bcjgn,bcjgh,bcjghp->bcghpn', bm, to_end, x)
    blk_decay = jnp.exp(a_cum[:, :, -1])

    def step(s, inp):
        dec, st = inp
        return dec[..., None, None] * s + st, s

    s_final, s_in = lax.scan(step, s0.astype(jnp.float32),
                             (jnp.moveaxis(blk_decay, 1, 0), jnp.moveaxis(blk_states, 1, 0)))
    s_in = jnp.moveaxis(s_in, 0, 1)
    y_off = jnp.einsum('bcign,bcghpn,bcigh->bcighp', cm, s_in, jnp.exp(a_cum))
    return (y_diag + y_off).reshape((bsz, T) + x.shape[3:]), s_final


def ssd_branch(h, w_z, w_xbc, w_dt, conv_hist, s0, conv_w, conv_b, dt_bias, a_log, d_skip, norm_w, front):
    bsz, T, _ = h.shape
    z = h @ w_z
    xbc, conv_new = causal_dwconv(h @ w_xbc, conv_hist, conv_w, conv_b)
    xbc = jax.nn.silu(xbc)
    n_bc = SSD_GROUPS * SSD_STATE
    xs = xbc[..., :SSD_WIDTH].reshape(bsz, T, SSD_GROUPS, SSD_HPG, SSD_HEAD_DIM)
    bm = xbc[..., SSD_WIDTH:SSD_WIDTH + n_bc].reshape(bsz, T, SSD_GROUPS, SSD_STATE)
    cm = xbc[..., SSD_WIDTH + n_bc:].reshape(bsz, T, SSD_GROUPS, SSD_STATE)
    dt = jax.nn.softplus((h @ w_dt + dt_bias).astype(jnp.float32)).reshape(bsz, T, SSD_GROUPS, SSD_HPG)
    a = -jnp.exp(a_log.astype(jnp.float32)).reshape(SSD_GROUPS, SSD_HPG)
    back = (-(front + T)) % CHUNK
    pad = lambda t: jnp.pad(t, [(0, 0), (front, back)] + [(0, 0)] * (t.ndim - 2))
    y, s_new = ssd_scan(pad(xs), pad(dt), a, pad(bm), pad(cm), s0)
    y = y[:, front:front + T] + d_skip.reshape(SSD_GROUPS, SSD_HPG)[:, :, None] * xs
    y = y.reshape(bsz, T, SSD_WIDTH) * jax.nn.silu(z)
    gw = SSD_WIDTH // SSD_GROUPS
    y = rmsnorm(y.reshape(bsz, T, SSD_GROUPS, gw), norm_w.reshape(SSD_GROUPS, gw)).reshape(bsz, T, SSD_WIDTH)
    return y.astype(h.dtype), conv_new, s_new


def _complex_affine_combine(e1, e2):
    ar1, ai1, br1, bi1 = e1
    ar2, ai2, br2, bi2 = e2
    return (ar1 * ar2 - ai1 * ai2, ar1 * ai2 + ai1 * ar2,
            ar2 * br1 - ai2 * bi1 + br2, ar2 * bi1 + ai2 * br1 + bi2)


def s5_branch(u, s_re0, s_im0, lam_re, lam_im, log_step, b_re, b_im, c_re, c_im, d_skip):
    f32 = jnp.float32
    bsz, T, _ = u.shape
    ug = u.reshape(bsz, T, S5_GROUPS, S5_GROUP_CH).astype(f32)
    lr, li = lam_re.astype(f32), lam_im.astype(f32)
    step = jnp.exp(log_step.astype(f32))[:, None]
    mag = jnp.exp(lr * step)
    ab_re, ab_im = mag * jnp.cos(li * step), mag * jnp.sin(li * step)
    den = lr * lr + li * li
    nr = ab_re - 1.0
    f_re = (nr * lr + ab_im * li) / den
    f_im = (ab_im * lr - nr * li) / den
    br, bi = b_re.astype(f32), b_im.astype(f32)
    bb_re = f_re[..., None] * br - f_im[..., None] * bi
    bb_im = f_re[..., None] * bi + f_im[..., None] * br
    bu_re = jnp.einsum('btgh,gph->btgp', ug, bb_re)
    bu_im = jnp.einsum('btgh,gph->btgp', ug, bb_im)
    sr, si = s_re0.astype(f32), s_im0.astype(f32)
    bu_re = bu_re.at[:, 0].add(ab_re * sr - ab_im * si)
    bu_im = bu_im.at[:, 0].add(ab_re * si + ab_im * sr)
    a_re = jnp.broadcast_to(ab_re, (1, T) + ab_re.shape)
    a_im = jnp.broadcast_to(ab_im, (1, T) + ab_im.shape)
    _, _, x_re, x_im = lax.associative_scan(_complex_affine_combine, (a_re, a_im, bu_re, bu_im), axis=1)
    y = (jnp.einsum('btgp,ghp->btgh', x_re, c_re.astype(f32))
         - jnp.einsum('btgp,ghp->btgh', x_im, c_im.astype(f32)) + d_skip * ug)
    return y.reshape(bsz, T, S5_WIDTH).astype(u.dtype), x_re[:, -1], x_im[:, -1]


def stick_breaking(q, k, v):
    tq, tk = q.shape[1], k.shape[1]
    n_hist = tk - tq
    outs = []
    for start in range(0, tq, SB_BLOCK):
        end = min(start + SB_BLOCK, tq)
        kend = n_hist + end
        z = jnp.einsum('bqhd,bkhd->bhqk', q[:, start:end], k[:, :kend]).astype(jnp.float32) * SB_SCALE
        q_pos = n_hist + jnp.arange(start, end)
        k_pos = jnp.arange(kend)
        mask = k_pos[None, :] < q_pos[:, None]
        log_keep = jnp.where(mask, jax.nn.log_sigmoid(-z), 0.0)
        later = lax.cumsum(log_keep, axis=3, reverse=True) - log_keep
        w = jnp.where(mask, jnp.exp(jax.nn.log_sigmoid(z) + later), 0.0)
        outs.append(jnp.einsum('bhqk,bkhd->bqhd', w.astype(v.dtype), v[:, :kend]))
    return jnp.concatenate(outs, axis=1)


def trunk_layer(x, k_hist, v_hist, conv_hist, ssd_s0, s5_re0, s5_im0, front, lp):
    bsz, T, _ = x.shape
    w_in = lp['w_in']
    col = lambda off, n: w_in[:, off:off + n]
    h = rmsnorm(x, lp['norm_mix'])
    y_a, conv_new, ssd_new = ssd_branch(h, col(OFF_Z, SSD_WIDTH), col(OFF_XBC, CONV_DIM), col(OFF_DT, SSD_HEADS),
                                        conv_hist, ssd_s0, lp['conv_w'], lp['conv_b'], lp['dt_bias'],
                                        lp['a_log'], lp['d_ssd'], lp['norm_ssd'], front)
    y_b, s5_re, s5_im = s5_branch(h @ col(OFF_U, S5_WIDTH), s5_re0, s5_im0, lp['lam_re'], lp['lam_im'],
                                  lp['log_step'], lp['b_re'], lp['b_im'], lp['c_re'], lp['c_im'], lp['d_s5'])
    glu = jax.nn.gelu(y_b) @ lp['w_glu']
    out_b = glu[..., :D_MODEL] * jax.nn.sigmoid(glu[..., D_MODEL:])
    heads = lambda t: t.reshape(bsz, T, SB_HEADS, SB_HEAD_DIM)
    q = rmsnorm(heads(h @ col(OFF_Q, SB_WIDTH)), lp['q_norm'])
    k = rmsnorm(heads(h @ col(OFF_K, SB_WIDTH)), lp['k_norm'])
    v = heads(h @ col(OFF_V, SB_WIDTH))
    o_c = stick_breaking(q, jnp.concatenate([k_hist.astype(k.dtype), k], axis=1),
                         jnp.concatenate([v_hist.astype(v.dtype), v], axis=1))
    gates = jax.nn.sigmoid(h @ col(OFF_GATE, N_BRANCH * D_MODEL)).reshape(bsz, T, N_BRANCH, D_MODEL)
    mix = (gates[:, :, 0] * (y_a @ lp['w_lift_a']) + gates[:, :, 1] * out_b
           + gates[:, :, 2] * (o_c.reshape(bsz, T, SB_WIDTH) @ lp['w_lift_c']))
    x = x + mix @ lp['w_out']
    h2 = rmsnorm(x, lp['norm_ffn'])
    x = x + jnp.square(jax.nn.relu(h2 @ lp['w_up'])) @ lp['w_down']
    return x, k, v, conv_new, ssd_new, s5_re, s5_im


def setup_inputs(seed: int = 0) -> dict:
    key = jax.random.key(seed)
    ks = jax.random.split(key, 40)
    f32 = jnp.float32
    L = DEPTH

    def nrm(i, shape, scale):
        return jax.random.normal(ks[i], shape, f32) * scale

    def unif(i, shape, lo, hi):
        return jax.random.uniform(ks[i], shape, f32, lo, hi)

    dt0 = jnp.exp(unif(13, (L, SSD_HEADS), math.log(1e-3), math.log(1e-1)))
    n = jnp.arange(S5_STATE, dtype=f32)
    return {
        'x_prompt': nrm(0, (BATCH, SEQ, D_MODEL), 1.0),
        'x_sample': nrm(1, (DEC_BATCH, DEC_SEQ, D_MODEL), 1.0),
        'cache_k': nrm(2, (L, DEC_BATCH, PAST_LEN, SB_HEADS, SB_HEAD_DIM), 1.0),
        'cache_v': nrm(3, (L, DEC_BATCH, PAST_LEN, SB_HEADS, SB_HEAD_DIM), 1.0),
        'state_conv': nrm(4, (L, DEC_BATCH, CONV_W - 1, CONV_DIM), 1.0),
        'state_ssd': nrm(5, (L, DEC_BATCH, SSD_GROUPS, SSD_HPG, SSD_HEAD_DIM, SSD_STATE), 0.1),
        'state_s5_re': nrm(6, (L, DEC_BATCH, S5_GROUPS, S5_STATE), 0.1),
        'state_s5_im': nrm(7, (L, DEC_BATCH, S5_GROUPS, S5_STATE), 0.1),
        'meta_tokens': nrm(8, (N_META, D_MODEL), 1.0),
        'norm_mix': 1.0 + nrm(9, (L, D_MODEL), 0.02),
        'w_in': nrm(10, (L, D_MODEL, IN_COLS), D_MODEL ** -0.5),
        'conv_w': nrm(11, (L, CONV_W, CONV_DIM), CONV_W ** -0.5),
        'conv_b': nrm(12, (L, CONV_DIM), 0.01),
        'dt_bias': dt0 + jnp.log(-jnp.expm1(-dt0)),
        'a_log': jnp.log(unif(14, (L, SSD_HEADS), 1.0, 16.0)),
        'd_ssd': 1.0 + nrm(15, (L, SSD_HEADS), 0.02),
        'norm_ssd': 1.0 + nrm(16, (L, SSD_WIDTH), 0.02),
        'lam_re': -0.5 + nrm(17, (L, S5_GROUPS, S5_STATE), 0.01),
        'lam_im': math.pi * n + nrm(18, (L, S5_GROUPS, S5_STATE), 0.01),
        'log_step': unif(19, (L, S5_GROUPS), math.log(1e-3), math.log(1e-1)),
        'b_re': nrm(20, (L, S5_GROUPS, S5_STATE, S5_GROUP_CH), (2 * S5_GROUP_CH) ** -0.5),
        'b_im': nrm(21, (L, S5_GROUPS, S5_STATE, S5_GROUP_CH), (2 * S5_GROUP_CH) ** -0.5),
        'c_re': nrm(22, (L, S5_GROUPS, S5_GROUP_CH, S5_STATE), (2 * S5_STATE) ** -0.5),
        'c_im': nrm(23, (L, S5_GROUPS, S5_GROUP_CH, S5_STATE), (2 * S5_STATE) ** -0.5),
        'd_s5': nrm(24, (L, S5_GROUPS, S5_GROUP_CH), 1.0),
        'w_glu': nrm(25, (L, S5_WIDTH, 2 * D_MODEL), S5_WIDTH ** -0.5),
        'q_norm': 1.0 + nrm(26, (L, SB_HEAD_DIM), 0.02),
        'k_norm': 1.0 + nrm(27, (L, SB_HEAD_DIM), 0.02),
        'w_lift_a': nrm(28, (L, SSD_WIDTH, D_MODEL), SSD_WIDTH ** -0.5),
        'w_lift_c': nrm(29, (L, SB_WIDTH, D_MODEL), SB_WIDTH ** -0.5),
        'w_out': nrm(30, (L, D_MODEL, D_MODEL), D_MODEL ** -0.5),
        'norm_ffn': 1.0 + nrm(31, (L, D_MODEL), 0.02),
        'w_up': nrm(32, (L, D_MODEL, D_FF), D_MODEL ** -0.5),
        'w_down': nrm(33, (L, D_FF, D_MODEL), D_FF ** -0.5),
    }


def reference(x_prompt, x_sample, cache_k, cache_v, state_conv, state_ssd, state_s5_re, state_s5_im,
              meta_tokens, norm_mix, w_in, conv_w, conv_b, dt_bias, a_log, d_ssd, norm_ssd,
              lam_re, lam_im, log_step, b_re, b_im, c_re, c_im, d_s5, w_glu, q_norm, k_norm,
              w_lift_a, w_lift_c, w_out, norm_ffn, w_up, w_down):
    bp = x_prompt.shape[0]
    dtype = x_prompt.dtype
    xp = jnp.concatenate([jnp.broadcast_to(meta_tokens.astype(dtype)[None], (bp, N_META, D_MODEL)), x_prompt], axis=1)
    xs = x_sample
    front_p = (-N_META) % CHUNK
    front_s = PAST_LEN % CHUNK
    kv_empty = jnp.zeros((bp, 0, SB_HEADS, SB_HEAD_DIM), dtype)
    conv_p0 = jnp.zeros((bp, CONV_W - 1, CONV_DIM), dtype)
    ssd_p0 = jnp.zeros((bp, SSD_GROUPS, SSD_HPG, SSD_HEAD_DIM, SSD_STATE), jnp.float32)
    s5_p0 = jnp.zeros((bp, S5_GROUPS, S5_STATE), jnp.float32)
    outs_p, outs_s = [], []
    for l in range(DEPTH):
        lp = {'norm_mix': norm_mix[l], 'w_in': w_in[l], 'conv_w': conv_w[l], 'conv_b': conv_b[l],
              'dt_bias': dt_bias[l], 'a_log': a_log[l], 'd_ssd': d_ssd[l], 'norm_ssd': norm_ssd[l],
              'lam_re': lam_re[l], 'lam_im': lam_im[l], 'log_step': log_step[l], 'b_re': b_re[l],
              'b_im': b_im[l], 'c_re': c_re[l], 'c_im': c_im[l], 'd_s5': d_s5[l], 'w_glu': w_glu[l],
              'q_norm': q_norm[l], 'k_norm': k_norm[l], 'w_lift_a': w_lift_a[l], 'w_lift_c': w_lift_c[l],
              'w_out': w_out[l], 'norm_ffn': norm_ffn[l], 'w_up': w_up[l], 'w_down': w_down[l]}
        xp, *st_p = trunk_layer(xp, kv_empty, kv_empty, conv_p0, ssd_p0, s5_p0, s5_p0, front_p, lp)
        xs, *st_s = trunk_layer(xs, cache_k[l], cache_v[l], state_conv[l], state_ssd[l],
                                state_s5_re[l], state_s5_im[l], front_s, lp)
        outs_p.append(st_p)
        outs_s.append(st_s)
    stk = lambda outs, i: jnp.stack([o[i] for o in outs], axis=0)
    return (xp[:, N_META:], xs,
            stk(outs_p, 0), stk(outs_p, 1), stk(outs_p, 2), stk(outs_p, 3), stk(outs_p, 4), stk(outs_p, 5),
            stk(outs_s, 0), stk(outs_s, 1), stk(outs_s, 2), stk(outs_s, 3), stk(outs_s, 4), stk(outs_s, 5))
```

```python
import functools
import math

import jax
import jax.numpy as jnp
from jax import lax
from jax.experimental import pallas as pl
from jax.experimental.pallas import tpu as pltpu

F32 = jnp.float32
BF16 = jnp.bfloat16

D_MODEL = 1024
N_META = 16
RMS_EPS = 1e-6
SSD_HEADS = 16
SSD_HEAD_DIM = 64
SSD_GROUPS = 2
SSD_HPG = SSD_HEADS // SSD_GROUPS
SSD_STATE = 64
SSD_WIDTH = SSD_HEADS * SSD_HEAD_DIM
CONV_W = 4
N_BC = SSD_GROUPS * SSD_STATE
CONV_DIM = SSD_WIDTH + 2 * N_BC
S5_WIDTH = D_MODEL // 2
S5_GROUP_CH = 16
S5_GROUPS = S5_WIDTH // S5_GROUP_CH
S5_STATE = 64
S5_LANES = S5_GROUPS * S5_STATE
SB_HEADS = 8
SB_HEAD_DIM = 64
SB_WIDTH = SB_HEADS * SB_HEAD_DIM
SB_SCALE = 1.0 / math.sqrt(SB_HEAD_DIM)
N_BRANCH = 3
D_FF = 4 * D_MODEL
OFF_Z = 0
OFF_XBC = OFF_Z + SSD_WIDTH
OFF_DT = OFF_XBC + CONV_DIM
OFF_U = OFF_DT + SSD_HEADS
OFF_Q = OFF_U + S5_WIDTH
OFF_K = OFF_Q + SB_WIDTH
OFF_V = OFF_K + SB_WIDTH
OFF_GATE = OFF_V + SB_WIDTH

LANES = 128
SUBLANES = 8
VMEM_LIMIT = 56 * 1024 * 1024
S5_BLK_GROUPS = LANES // S5_GROUP_CH
S5_BLK_LANES = S5_BLK_GROUPS * S5_STATE
S5_NBLK = S5_GROUPS // S5_BLK_GROUPS


def _dot(a, b):
    return jnp.dot(a, b, preferred_element_type=F32)


def _dot_nt(a, b):
    return lax.dot_general(a, b, (((1,), (1,)), ((), ())), preferred_element_type=F32)


def _split(a, terms):
    out = []
    r = a
    for _ in range(terms):
        p = r.astype(BF16)
        out.append(p)
        r = r - p.astype(F32)
    return out


def _dot_split_lhs(a, b_bf16, terms=3):
    return sum(_dot(p, b_bf16) for p in _split(a, terms))


def _dot_split_rhs(a_bf16, b, terms=3):
    return sum(_dot(a_bf16, p) for p in _split(b, terms))


def _sigmoid(x):
    return 1.0 / (1.0 + jnp.exp(-x))


def _softplus(x):
    return jnp.maximum(x, 0.0) + jnp.log1p(jnp.exp(-jnp.abs(x)))


def _rmsnorm_rows(x, g):
    return x * lax.rsqrt(jnp.mean(x * x, axis=-1, keepdims=True) + RMS_EPS) * g


def _params(*sem):
    return pltpu.CompilerParams(dimension_semantics=sem, vmem_limit_bytes=VMEM_LIMIT)


def _full(shape):
    n = len(shape)
    return pl.BlockSpec(shape, lambda *_: (0,) * n)


def _row_tile(n):
    for t in (256, 128, 64, 32, 16, 8):
        if n % t == 0:
            return t
    raise ValueError(f"token count {n} is not a multiple of {SUBLANES}")


def _proj_body(x_ref, g_ref, wz_ref, wxbc_ref, wdt_ref, wu_ref, wq_ref, wk_ref, wv_ref,
               qn_ref, kn_ref, hm_ref,
               z_ref, xbc_ref, dt_ref, u_ref, k_ref, v_ref, qb_ref, kb_ref, vb_ref):
    hb = _rmsnorm_rows(x_ref[...], g_ref[...]).astype(BF16)
    z_ref[...] = _dot(hb, wz_ref[...])
    xbc_ref[...] = _dot(hb, wxbc_ref[...])
    dt_ref[...] = _dot(hb, wdt_ref[...])
    u_ref[...] = _dot(hb, wu_ref[...])
    hm = hm_ref[...]

    def head_norm(t, w):
        return t * lax.rsqrt(_dot_split_lhs(t * t, hm, 2) + RMS_EPS) * w

    q = head_norm(_dot(hb, wq_ref[...]), qn_ref[...])
    k = head_norm(_dot(hb, wk_ref[...]), kn_ref[...])
    v = _dot(hb, wv_ref[...])
    k_ref[...] = k
    v_ref[...] = v
    qb_ref[...] = (q * SB_SCALE).astype(BF16)
    kb_ref[...] = k.astype(BF16)
    vb_ref[...] = v.astype(BF16)


def _proj(x2, lw):
    n = x2.shape[0]
    tm = _row_tile(n)
    rows = lambda w: pl.BlockSpec((tm, w), lambda i: (i, 0))
    widths = (SSD_WIDTH, CONV_DIM, LANES, S5_WIDTH, SB_WIDTH, SB_WIDTH, SB_WIDTH, SB_WIDTH, SB_WIDTH)
    dtypes = (F32, F32, F32, F32, F32, F32, BF16, BF16, BF16)
    ins = (lw['norm_mix'], lw['wz'], lw['wxbc'], lw['wdt'], lw['wu'], lw['wq'], lw['wk'], lw['wv'],
           lw['q_norm'], lw['k_norm'], lw['head_mean'])
    return pl.pallas_call(
        _proj_body,
        grid=(n // tm,),
        in_specs=[rows(D_MODEL)] + [_full(a.shape) for a in ins],
        out_specs=[rows(w) for w in widths],
        out_shape=[jax.ShapeDtypeStruct((n, w), d) for w, d in zip(widths, dtypes)],
        compiler_params=_params("parallel"),
        name="proj",
    )(x2, *ins)


def _ssd_body(z_ref, xbc_ref, dt_ref, hist_ref, s0_ref, cw_ref, cb_ref, dtb_ref, alog_ref,
              dskip_ref, nw_ref, expand_ref, eye_ref, smask_ref,
              y_ref, sout_ref, xp_ref, st_ref, yacc_ref, *, lc):
    c = pl.program_id(1)

    @pl.when(c == 0)
    def _():
        xp_ref[0:SUBLANES, :] = hist_ref[0]
        st_ref[...] = s0_ref[0]

    xp_ref[SUBLANES:SUBLANES + lc, :] = xbc_ref[0]
    conv = cb_ref[...]
    for k in range(CONV_W):
        lo = SUBLANES - (CONV_W - 1) + k
        conv = conv + xp_ref[lo:lo + lc, :] * cw_ref[k:k + 1, :]
    xp_ref[0:SUBLANES, :] = xp_ref[lc:lc + SUBLANES, :]
    act = conv * _sigmoid(conv)
    xs = act[:, :SSD_WIDTH]
    bm = act[:, SSD_WIDTH:SSD_WIDTH + N_BC]
    cm = act[:, SSD_WIDTH + N_BC:]

    expand = expand_ref[...]
    eye = eye_ref[...]
    dt = _softplus(dt_ref[0] + dtb_ref[...])
    a = -jnp.exp(alog_ref[...])
    dta = dt * a
    ri = lax.broadcasted_iota(jnp.int32, (lc, lc), 0)
    ci = lax.broadcasted_iota(jnp.int32, (lc, lc), 1)
    causal = ri >= ci
    tri = causal.astype(BF16)
    acum = _dot_split_rhs(tri, dta)
    acum_t = sum(_dot_nt(eye, p) for p in _split(acum, 3))
    a_last = acum[lc - 1:lc, :]

    bm_b = bm.astype(BF16)
    cm_b = cm.astype(BF16)
    lane = lax.broadcasted_iota(jnp.int32, (lc, LANES), 1)
    left = lane < SSD_STATE
    xdt = (xs * _dot_split_lhs(dt, expand)).astype(BF16)
    zero_b = jnp.zeros((), BF16)
    cbs = [_dot_nt(jnp.where(left if g == 0 else ~left, cm_b, zero_b), bm_b) for g in range(SSD_GROUPS)]

    for hp in range(SSD_HEADS // 2):
        xpair = xdt[:, hp * LANES:(hp + 1) * LANES]
        outs = []
        for h in (2 * hp, 2 * hp + 1):
            seg = acum[:, h:h + 1] - acum_t[h:h + 1, :]
            m = jnp.where(causal, cbs[h // SSD_HPG] * jnp.exp(jnp.minimum(seg, 0.0)), 0.0)
            outs.append(_dot(m.astype(BF16), xpair))
        yacc_ref[:, hp * LANES:(hp + 1) * LANES] = jnp.where(left, outs[0], outs[1])

    st = st_ref[...]
    y_off = _dot(cm_b, st.astype(BF16)) * _dot_split_lhs(jnp.exp(acum), expand)
    y = (yacc_ref[...] + y_off + dskip_ref[...] * xs)
    zz = z_ref[0]
    y = y * (zz * _sigmoid(zz))
    gw = SSD_WIDTH // SSD_GROUPS
    parts = []
    for g in range(SSD_GROUPS):
        yg = y[:, g * gw:(g + 1) * gw]
        parts.append(yg * lax.rsqrt(jnp.mean(yg * yg, axis=-1, keepdims=True) + RMS_EPS))
    y_ref[0] = (jnp.concatenate(parts, axis=1) * nw_ref[...]).astype(BF16)

    to_end = jnp.exp(a_last - acum) * dt
    xw = (xs * _dot_split_lhs(to_end, expand)).astype(BF16)
    bm_t = _dot_nt(eye, bm_b).astype(BF16)
    dec = _dot_split_lhs(jnp.broadcast_to(jnp.exp(a_last), (SUBLANES, LANES)), expand)[0:1, :]
    st_new = st * dec + smask_ref[...] * _dot(bm_t, xw)
    st_ref[...] = st_new

    @pl.when(c == pl.num_programs(1) - 1)
    def _():
        sout_ref[0] = st_new


def _ssd(z3, xbc3, dt3, hist, s0, lw):
    b, t, _ = z3.shape
    lc = min(t, 128)
    shared = hist.shape[0] == 1
    per_b = (lambda i, c: (0, 0, 0)) if shared else (lambda i, c: (i, 0, 0))
    seq = lambda w: pl.BlockSpec((1, lc, w), lambda i, c: (i, c, 0))
    consts = (lw['conv_w'], lw['conv_b'], lw['dt_bias'], lw['a_log'], lw['d_ssd'], lw['norm_ssd'],
              lw['expand'], lw['eye'], lw['state_mask'])
    return pl.pallas_call(
        functools.partial(_ssd_body, lc=lc),
        grid=(b, t // lc),
        in_specs=[seq(SSD_WIDTH), seq(CONV_DIM), seq(LANES),
                  pl.BlockSpec((1, SUBLANES, CONV_DIM), per_b),
                  pl.BlockSpec((1, LANES, SSD_WIDTH), per_b)] + [_full(a.shape) for a in consts],
        out_specs=[seq(SSD_WIDTH), pl.BlockSpec((1, LANES, SSD_WIDTH), lambda i, c: (i, 0, 0))],
        out_shape=[jax.ShapeDtypeStruct((b, t, SSD_WIDTH), BF16),
                   jax.ShapeDtypeStruct((b, LANES, SSD_WIDTH), F32)],
        scratch_shapes=[pltpu.VMEM((lc + SUBLANES, CONV_DIM), F32),
                        pltpu.VMEM((LANES, SSD_WIDTH), F32),
                        pltpu.VMEM((lc, SSD_WIDTH), F32)],
        compiler_params=_params("parallel", "arbitrary"),
        name="ssd",
    )(z3, xbc3, dt3, hist, s0, *consts)


def _s5_param_body(lr_ref, li_ref, step_ref, br_ref, bi_ref, abr_ref, abi_ref, bbr_ref, bbi_ref):
    lr, li, step = lr_ref[...], li_ref[...], jnp.exp(step_ref[...])
    mag = jnp.exp(lr * step)
    ab_re = mag * jnp.cos(li * step)
    ab_im = mag * jnp.sin(li * step)
    den = lr * lr + li * li
    nr = ab_re - 1.0
    f_re = (nr * lr + ab_im * li) / den
    f_im = (ab_im * lr - nr * li) / den
    br, bi = br_ref[...], bi_ref[...]
    abr_ref[...] = ab_re
    abi_ref[...] = ab_im
    bbr_ref[...] = f_re * br - f_im * bi
    bbi_ref[...] = f_re * bi + f_im * br


def _s5_params(lam_re, lam_im, log_step, b_re, b_im):
    rep = lambda t: jnp.repeat(t, S5_GROUP_CH, axis=0)
    shape = (S5_WIDTH, S5_STATE)
    step = jnp.broadcast_to(rep(log_step[:, None]), shape)
    to_rows = lambda t: jnp.transpose(t, (0, 2, 1)).reshape(shape)
    outs = pl.pallas_call(
        _s5_param_body,
        out_shape=[jax.ShapeDtypeStruct(shape, F32)] * 4,
        name="s5_params",
    )(rep(lam_re), rep(lam_im), step, to_rows(b_re), to_rows(b_im))
    return outs


def _s5_body(u_ref, sre_ref, sim_ref, abr_ref, abi_ref, bbr_ref, bbi_ref, ccr_ref, cci_ref, dsk_ref,
             y_ref, ore_ref, oim_ref, xre_ref, xim_ref, pwr_ref, pwi_ref, cr_ref, ci_ref, *, lt):
    c = pl.program_id(1)

    @pl.when(c == 0)
    def _():
        pr, pi = abr_ref[...], abi_ref[...]
        ar, ai = pr, pi
        for r in range(SUBLANES):
            pwr_ref[r:r + 1, :] = pr
            pwi_ref[r:r + 1, :] = pi
            pr, pi = pr * ar - pi * ai, pr * ai + pi * ar
        cr_ref[...] = sre_ref[0]
        ci_ref[...] = sim_ref[0]

    row = lax.broadcasted_iota(jnp.int32, (SUBLANES, S5_BLK_LANES), 0)
    u = u_ref[0]
    for j in range(S5_NBLK):
        lanes = slice(j * S5_BLK_LANES, (j + 1) * S5_BLK_LANES)
        ub = u[:, j * LANES:(j + 1) * LANES]
        ub16 = ub.astype(BF16)
        xre_ref[...] = _dot(ub16, bbr_ref[j])
        xim_ref[...] = _dot(ub16, bbi_ref[j])
        pwr = pwr_ref[:, lanes]
        pwi = pwi_ref[:, lanes]

        def group(i, carry):
            cr, ci = carry
            rows = pl.ds(pl.multiple_of(i * SUBLANES, SUBLANES), SUBLANES)
            r, m = xre_ref[rows, :], xim_ref[rows, :]
            for sh in (1, 2, 4):
                ar, ai = pwr[sh - 1:sh, :], pwi[sh - 1:sh, :]
                rs = jnp.where(row >= sh, pltpu.roll(r, sh, 0), 0.0)
                ms = jnp.where(row >= sh, pltpu.roll(m, sh, 0), 0.0)
                r, m = r + (ar * rs - ai * ms), m + (ar * ms + ai * rs)
            r, m = r + (pwr * cr - pwi * ci), m + (pwr * ci + pwi * cr)
            xre_ref[rows, :] = r
            xim_ref[rows, :] = m
            return r[SUBLANES - 1:SUBLANES, :], m[SUBLANES - 1:SUBLANES, :]

        cr, ci = lax.fori_loop(0, lt // SUBLANES, group, (cr_ref[:, lanes], ci_ref[:, lanes]))
        cr_ref[:, lanes] = cr
        ci_ref[:, lanes] = ci
        yb = (_dot(xre_ref[...].astype(BF16), ccr_ref[j]) - _dot(xim_ref[...].astype(BF16), cci_ref[j])
              + dsk_ref[:, j * LANES:(j + 1) * LANES] * ub)
        gelu = 0.5 * yb * (1.0 + jnp.tanh(math.sqrt(2.0 / math.pi) * (yb + 0.044715 * (yb * yb * yb))))
        y_ref[0, :, j * LANES:(j + 1) * LANES] = gelu.astype(BF16)

    @pl.when(c == pl.num_programs(1) - 1)
    def _():
        ore_ref[0] = cr_ref[...]
        oim_ref[0] = ci_ref[...]


def _s5(u3, s_re, s_im, lw):
    b, t, _ = u3.shape
    lt = min(t, 256)
    shared = s_re.shape[0] == 1
    per_b = (lambda i, c: (0, 0, 0)) if shared else (lambda i, c: (i, 0, 0))
    state = pl.BlockSpec((1, 1, S5_LANES), per_b)
    state_out = pl.BlockSpec((1, 1, S5_LANES), lambda i, c: (i, 0, 0))
    consts = (lw['s5_ab_re'], lw['s5_ab_im'], lw['s5_bb_re'], lw['s5_bb_im'], lw['s5_cc_re'],
              lw['s5_cc_im'], lw['d_s5'])
    return pl.pallas_call(
        functools.partial(_s5_body, lt=lt),
        grid=(b, t // lt),
        in_specs=[pl.BlockSpec((1, lt, S5_WIDTH), lambda i, c: (i, c, 0)), state, state]
                 + [_full(a.shape) for a in consts],
        out_specs=[pl.BlockSpec((1, lt, S5_WIDTH), lambda i, c: (i, c, 0)), state_out, state_out],
        out_shape=[jax.ShapeDtypeStruct((b, t, S5_WIDTH), BF16),
                   jax.ShapeDtypeStruct((b, 1, S5_LANES), F32),
                   jax.ShapeDtypeStruct((b, 1, S5_LANES), F32)],
        scratch_shapes=[pltpu.VMEM((lt, S5_BLK_LANES), F32), pltpu.VMEM((lt, S5_BLK_LANES), F32),
                        pltpu.VMEM((SUBLANES, S5_LANES), F32), pltpu.VMEM((SUBLANES, S5_LANES), F32),
                        pltpu.VMEM((1, S5_LANES), F32), pltpu.VMEM((1, S5_LANES), F32)],
        compiler_params=_params("parallel", "arbitrary"),
        name="s5",
    )(u3, s_re, s_im, *consts)


def _sb_block(qm, kb, vb, mask, upper, acc, carry):
    z = _dot_nt(qm, kb)
    sp = jnp.log1p(jnp.exp(-jnp.abs(z)))
    log_beta = jnp.minimum(z, 0.0) - sp
    log_keep = log_beta - z
    if mask is not None:
        log_keep = jnp.where(mask, log_keep, 0.0)
    later = _dot_split_lhs(log_keep, upper, 2)
    w = jnp.exp(log_beta + later + carry)
    if mask is not None:
        w = jnp.where(mask, w, 0.0)
    acc = acc + _dot(w.astype(BF16), vb)
    carry = carry + (later[:, 0:1] + log_keep[:, 0:1])
    return acc, carry


def _upper(n):
    r = lax.broadcasted_iota(jnp.int32, (n, n), 0)
    c = lax.broadcasted_iota(jnp.int32, (n, n), 1)
    return (r > c).astype(BF16)


def _attn_body(*refs, bq, hist_blocks, hist_bk):
    if hist_blocks:
        q_ref, k_ref, v_ref, kh_ref, vh_ref, o_ref = refs
    else:
        q_ref, k_ref, v_ref, o_ref = refs
    qi = pl.program_id(2)
    q = q_ref[0]
    lane = lax.broadcasted_iota(jnp.int32, (bq, LANES), 1)
    left = lane < SB_HEAD_DIM
    ri = lax.broadcasted_iota(jnp.int32, (bq, bq), 0)
    ci = lax.broadcasted_iota(jnp.int32, (bq, bq), 1)
    strictly_earlier = ci < ri
    up_new = _upper(bq)
    up_hist = _upper(hist_bk) if hist_blocks else None
    zero_b = jnp.zeros((), BF16)
    outs = []
    for hd in range(2):
        qm = jnp.where(left if hd == 0 else ~left, q, zero_b)
        acc = jnp.zeros((bq, LANES), F32)
        carry = jnp.zeros((bq, 1), F32)
        diag = pl.ds(pl.multiple_of(qi * bq, bq), bq)
        acc, carry = _sb_block(qm, k_ref[0, diag, :], v_ref[0, diag, :], strictly_earlier, up_new, acc, carry)

        def new_block(it, ac, qm=qm):
            rows = pl.ds(pl.multiple_of((qi - 1 - it) * bq, bq), bq)
            return _sb_block(qm, k_ref[0, rows, :], v_ref[0, rows, :], None, up_new, *ac)

        acc, carry = lax.fori_loop(0, qi, new_block, (acc, carry))
        if hist_blocks:
            def hist_block(it, ac, qm=qm):
                rows = pl.ds(pl.multiple_of((hist_blocks - 1 - it) * hist_bk, hist_bk), hist_bk)
                return _sb_block(qm, kh_ref[0, rows, :], vh_ref[0, rows, :], None, up_hist, *ac)

            acc, carry = lax.fori_loop(0, hist_blocks, hist_block, (acc, carry))
        outs.append(acc)
    o_ref[0] = jnp.where(left, outs[0], outs[1]).astype(BF16)


def _attn(qb, kb, vb, kh, vh):
    b, t, _ = qb.shape
    bq = min(t, 128)
    pairs = SB_WIDTH // LANES
    new = pl.BlockSpec((1, t, LANES), lambda i, p, j: (i, 0, p))
    args, specs = [qb, kb, vb], [pl.BlockSpec((1, bq, LANES), lambda i, p, j: (i, j, p)), new, new]
    hist_blocks = hist_bk = 0
    if kh is not None:
        th = kh.shape[1]
        hist_bk = min(th, 128)
        hist_blocks = th // hist_bk
        shared = kh.shape[0] == 1
        hmap = (lambda i, p, j: (0, 0, p)) if shared else (lambda i, p, j: (i, 0, p))
        args += [kh, vh]
        specs += [pl.BlockSpec((1, th, LANES), hmap)] * 2
    return pl.pallas_call(
        functools.partial(_attn_body, bq=bq, hist_blocks=hist_blocks, hist_bk=hist_bk),
        grid=(b, pairs, t // bq),
        in_specs=specs,
        out_specs=pl.BlockSpec((1, bq, LANES), lambda i, p, j: (i, j, p)),
        out_shape=jax.ShapeDtypeStruct((b, t, SB_WIDTH), BF16),
        compiler_params=_params("parallel", "parallel", "arbitrary"),
        name="attn",
    )(*args)


def _merge_body(x_ref, ya_ref, yb_ref, oc_ref, g_ref, wg_ref, wa_ref, wglu_ref, wc_ref, wo_ref, o_ref):
    x = x_ref[...]
    hb = _rmsnorm_rows(x, g_ref[...]).astype(BF16)
    gates = _sigmoid(_dot(hb, wg_ref[...]))
    glu = _dot(yb_ref[...], wglu_ref[...])
    mix = (gates[:, :D_MODEL] * _dot(ya_ref[...], wa_ref[...])
           + gates[:, D_MODEL:2 * D_MODEL] * (glu[:, :D_MODEL] * _sigmoid(glu[:, D_MODEL:]))
           + gates[:, 2 * D_MODEL:] * _dot(oc_ref[...], wc_ref[...]))
    o_ref[...] = x + _dot(mix.astype(BF16), wo_ref[...])


def _merge(x2, ya, yb, oc, lw):
    n = x2.shape[0]
    tm = _row_tile(n)
    rows = lambda w: pl.BlockSpec((tm, w), lambda i: (i, 0))
    consts = (lw['norm_mix'], lw['wgate'], lw['w_lift_a'], lw['w_glu'], lw['w_lift_c'], lw['w_out'])
    return pl.pallas_call(
        _merge_body,
        grid=(n // tm,),
        in_specs=[rows(D_MODEL), rows(SSD_WIDTH), rows(S5_WIDTH), rows(SB_WIDTH)]
                 + [_full(a.shape) for a in consts],
        out_specs=rows(D_MODEL),
        out_shape=jax.ShapeDtypeStruct((n, D_MODEL), F32),
        compiler_params=_params("parallel"),
        name="merge",
    )(x2, ya, yb, oc, *consts)


FF_CHUNK = 1024


def _ffn_body(x_ref, g_ref, wu_ref, wd_ref, o_ref):
    x = x_ref[...]
    hb = _rmsnorm_rows(x, g_ref[...]).astype(BF16)
    acc = x
    for j in range(D_FF // FF_CHUNK):
        cols = slice(j * FF_CHUNK, (j + 1) * FF_CHUNK)
        up = jnp.maximum(_dot(hb, wu_ref[:, cols]), 0.0)
        acc = acc + _dot((up * up).astype(BF16), wd_ref[cols, :])
    o_ref[...] = acc


def _ffn(x2, lw):
    n = x2.shape[0]
    tm = _row_tile(n)
    rows = pl.BlockSpec((tm, D_MODEL), lambda i: (i, 0))
    consts = (lw['norm_ffn'], lw['w_up'], lw['w_down'])
    return pl.pallas_call(
        _ffn_body,
        grid=(n // tm,),
        in_specs=[rows] + [_full(a.shape) for a in consts],
        out_specs=rows,
        out_shape=jax.ShapeDtypeStruct((n, D_MODEL), F32),
        compiler_params=_params("parallel"),
        name="ffn",
    )(x2, *consts)


def _block_diag(blocks):
    n, r, c = blocks.shape
    eye = jnp.eye(n, dtype=blocks.dtype)
    return (eye[:, None, :, None] * blocks[:, :, None, :]).reshape(n * r, n * c)


def _layer_weights(p, l):
    w_in = p['w_in'][l]
    col = lambda off, n: w_in[:, off:off + n].astype(BF16)
    row = lambda v: v.reshape(1, -1).astype(F32)
    heads = jnp.arange(SSD_WIDTH) // SSD_HEAD_DIM
    pad_lanes = lambda v: jnp.pad(v, (0, LANES - v.shape[0])).reshape(1, LANES)
    ab_re, ab_im, bb_re, bb_im = _s5_params(p['lam_re'][l], p['lam_im'][l], p['log_step'][l],
                                            p['b_re'][l], p['b_im'][l])
    to_bb = lambda t: jnp.stack([_block_diag(blk) for blk in
                                 t.reshape(S5_NBLK, S5_BLK_GROUPS, S5_GROUP_CH, S5_STATE)]).astype(BF16)
    to_cc = lambda t: jnp.stack([_block_diag(blk) for blk in
                                 jnp.transpose(t, (0, 2, 1)).reshape(S5_NBLK, S5_BLK_GROUPS, S5_STATE, S5_GROUP_CH)]
                                ).astype(BF16)
    state_rows = jnp.arange(LANES) // SSD_STATE
    state_cols = jnp.arange(SSD_WIDTH) // (SSD_WIDTH // SSD_GROUPS)
    return {
        'norm_mix': row(p['norm_mix'][l]),
        'wz': col(OFF_Z, SSD_WIDTH), 'wxbc': col(OFF_XBC, CONV_DIM),
        'wdt': jnp.pad(col(OFF_DT, SSD_HEADS), ((0, 0), (0, LANES - SSD_HEADS))),
        'wu': col(OFF_U, S5_WIDTH), 'wq': col(OFF_Q, SB_WIDTH), 'wk': col(OFF_K, SB_WIDTH),
        'wv': col(OFF_V, SB_WIDTH), 'wgate': col(OFF_GATE, N_BRANCH * D_MODEL),
        'q_norm': row(jnp.tile(p['q_norm'][l], SB_HEADS)), 'k_norm': row(jnp.tile(p['k_norm'][l], SB_HEADS)),
        'head_mean': (_block_diag(jnp.ones((SB_HEADS, SB_HEAD_DIM, SB_HEAD_DIM), F32)) / SB_HEAD_DIM).astype(BF16),
        'conv_w': jnp.pad(p['conv_w'][l], ((0, SUBLANES - CONV_W), (0, 0))),
        'conv_b': row(p['conv_b'][l]),
        'dt_bias': pad_lanes(p['dt_bias'][l]), 'a_log': pad_lanes(p['a_log'][l]),
        'd_ssd': row(p['d_ssd'][l][heads]), 'norm_ssd': row(p['norm_ssd'][l]),
        'expand': (jnp.arange(LANES)[:, None] == heads[None, :]).astype(BF16),
        'eye': jnp.eye(LANES, dtype=BF16),
        'state_mask': (state_rows[:, None] == state_cols[None, :]).astype(F32),
        's5_ab_re': ab_re[::S5_GROUP_CH].reshape(1, S5_LANES), 's5_ab_im': ab_im[::S5_GROUP_CH].reshape(1, S5_LANES),
        's5_bb_re': to_bb(bb_re), 's5_bb_im': to_bb(bb_im),
        's5_cc_re': to_cc(p['c_re'][l]), 's5_cc_im': to_cc(p['c_im'][l]),
        'd_s5': row(p['d_s5'][l]),
        'w_glu': p['w_glu'][l].astype(BF16), 'w_lift_a': p['w_lift_a'][l].astype(BF16),
        'w_lift_c': p['w_lift_c'][l].astype(BF16), 'w_out': p['w_out'][l].astype(BF16),
        'norm_ffn': row(p['norm_ffn'][l]), 'w_up': p['w_up'][l].astype(BF16), 'w_down': p['w_down'][l].astype(BF16),
    }


def _ssd_state_in(s):
    b = s.shape[0]
    t = jnp.transpose(s.astype(F32), (0, 1, 4, 2, 3)).reshape(b, SSD_GROUPS, SSD_STATE, SSD_HPG * SSD_HEAD_DIM)
    eye = jnp.eye(SSD_GROUPS, dtype=F32)
    return (t[:, :, :, None, :] * eye[None, :, None, :, None]).reshape(b, LANES, SSD_WIDTH)


def _ssd_state_out(s):
    b = s.shape[0]
    t = s.reshape(b, SSD_GROUPS, SSD_STATE, SSD_GROUPS, SSD_HPG, SSD_HEAD_DIM)
    t = jnp.stack([t[:, g, :, g] for g in range(SSD_GROUPS)], axis=1)
    return jnp.transpose(t, (0, 1, 3, 4, 2))


def _trunk_layer(x, k_hist, v_hist, conv_hist, ssd_s0, s5_re0, s5_im0, lw):
    b, t, _ = x.shape
    x2 = x.reshape(b * t, D_MODEL)
    z, xbc, dt, u, k, v, qb, kb, vb = _proj(x2, lw)
    seq = lambda a: a.reshape(b, t, a.shape[-1])
    hist8 = jnp.pad(conv_hist.astype(F32), ((0, 0), (SUBLANES - (CONV_W - 1), 0), (0, 0)))
    y_a, ssd_new = _ssd(seq(z), seq(xbc), seq(dt), hist8, _ssd_state_in(ssd_s0), lw)
    y_b, s5_re, s5_im = _s5(seq(u), s5_re0.reshape(-1, 1, S5_LANES).astype(F32),
                            s5_im0.reshape(-1, 1, S5_LANES).astype(F32), lw)
    if k_hist is None:
        kh = vh = None
    else:
        kh = k_hist.reshape(k_hist.shape[0], -1, SB_WIDTH).astype(BF16)
        vh = v_hist.reshape(v_hist.shape[0], -1, SB_WIDTH).astype(BF16)
    o_c = _attn(seq(qb), seq(kb), seq(vb), kh, vh)
    x2 = _merge(x2, y_a.reshape(b * t, SSD_WIDTH), y_b.reshape(b * t, S5_WIDTH), o_c.reshape(b * t, SB_WIDTH), lw)
    x2 = _ffn(x2, lw)
    conv_rows = jnp.concatenate([jnp.broadcast_to(conv_hist.astype(F32), (b, CONV_W - 1, CONV_DIM)),
                                 seq(xbc)[:, -(CONV_W - 1):]], axis=1)[:, -(CONV_W - 1):]
    heads = lambda a: a.reshape(b, t, SB_HEADS, SB_HEAD_DIM)
    return (x2.reshape(b, t, D_MODEL), heads(k), heads(v), conv_rows, _ssd_state_out(ssd_new),
            s5_re.reshape(b, S5_GROUPS, S5_STATE), s5_im.reshape(b, S5_GROUPS, S5_STATE))


def kernel(x_prompt, x_sample, cache_k, cache_v, state_conv, state_ssd, state_s5_re, state_s5_im, meta_tokens, norm_mix, w_in, conv_w, conv_b, dt_bias, a_log, d_ssd, norm_ssd, lam_re, lam_im, log_step, b_re, b_im, c_re, c_im, d_s5, w_glu, q_norm, k_norm, w_lift_a, w_lift_c, w_out, norm_ffn, w_up, w_down):
    p = dict(norm_mix=norm_mix, w_in=w_in, conv_w=conv_w, conv_b=conv_b, dt_bias=dt_bias, a_log=a_log,
             d_ssd=d_ssd, norm_ssd=norm_ssd, lam_re=lam_re, lam_im=lam_im, log_step=log_step, b_re=b_re,
             b_im=b_im, c_re=c_re, c_im=c_im, d_s5=d_s5, w_glu=w_glu, q_norm=q_norm, k_norm=k_norm,
             w_lift_a=w_lift_a, w_lift_c=w_lift_c, w_out=w_out, norm_ffn=norm_ffn, w_up=w_up, w_down=w_down)
    depth = w_in.shape[0]
    bp = x_prompt.shape[0]
    xm = meta_tokens.astype(x_prompt.dtype)[None]
    xp, xs = x_prompt, x_sample
    zeros = lambda *s: jnp.zeros(s, F32)
    outs_p, outs_s = [], []
    for l in range(depth):
        lw = _layer_weights(p, l)
        xm, *st_m = _trunk_layer(xm, None, None, zeros(1, CONV_W - 1, CONV_DIM),
                                 zeros(1, SSD_GROUPS, SSD_HPG, SSD_HEAD_DIM, SSD_STATE),
                                 zeros(1, S5_GROUPS, S5_STATE), zeros(1, S5_GROUPS, S5_STATE), lw)
        k_m, v_m, conv_m, ssd_m, s5re_m, s5im_m = st_m
        xp, k_p, v_p, *st_p = _trunk_layer(xp, k_m, v_m, conv_m, ssd_m, s5re_m, s5im_m, lw)
        grow = lambda m, a: jnp.concatenate([jnp.broadcast_to(m, (bp,) + m.shape[1:]), a], axis=1)
        outs_p.append([grow(k_m, k_p), grow(v_m, v_p)] + st_p)
        xs, *st_s = _trunk_layer(xs, cache_k[l], cache_v[l], state_conv[l], state_ssd[l],
                                 state_s5_re[l], state_s5_im[l], lw)
        outs_s.append(st_s)
    stk = lambda outs, i: jnp.stack([o[i] for o in outs], axis=0)
    return (xp, xs,
            stk(outs_p, 0), stk(outs_p, 1), stk(outs_p, 2), stk(outs_p, 3), stk(outs_p, 4), stk(outs_p, 5),
            stk(outs_s, 0), stk(outs_s, 1), stk(outs_s, 2), stk(outs_s, 3), stk(outs_s, 4), stk(outs_s, 5))
```

```python
import functools
import math

import jax
import jax.numpy as jnp
from jax import lax
from jax.experimental import pallas as pl
from jax.experimental.pallas import tpu as pltpu

F32 = jnp.float32
BF16 = jnp.bfloat16

D_MODEL = 1024
N_META = 16
RMS_EPS = 1e-6
SSD_HEADS = 16
SSD_HEAD_DIM = 64
SSD_GROUPS = 2
SSD_HPG = SSD_HEADS // SSD_GROUPS
SSD_STATE = 64
SSD_WIDTH = SSD_HEADS * SSD_HEAD_DIM
CONV_W = 4
N_BC = SSD_GROUPS * SSD_STATE
CONV_DIM = SSD_WIDTH + 2 * N_BC
S5_WIDTH = D_MODEL // 2
S5_GROUP_CH = 16
S5_GROUPS = S5_WIDTH // S5_GROUP_CH
S5_STATE = 64
S5_LANES = S5_GROUPS * S5_STATE
SB_HEADS = 8
SB_HEAD_DIM = 64
SB_WIDTH = SB_HEADS * SB_HEAD_DIM
SB_SCALE = 1.0 / math.sqrt(SB_HEAD_DIM)
N_BRANCH = 3
D_FF = 4 * D_MODEL
OFF_Z = 0
OFF_XBC = OFF_Z + SSD_WIDTH
OFF_DT = OFF_XBC + CONV_DIM
OFF_U = OFF_DT + SSD_HEADS
OFF_Q = OFF_U + S5_WIDTH
OFF_K = OFF_Q + SB_WIDTH
OFF_V = OFF_K + SB_WIDTH
OFF_GATE = OFF_V + SB_WIDTH

LANES = 128
SUBLANES = 8
VMEM_LIMIT = 56 * 1024 * 1024
S5_BLK_GROUPS = LANES // S5_GROUP_CH
S5_BLK_LANES = S5_BLK_GROUPS * S5_STATE
S5_NBLK = S5_GROUPS // S5_BLK_GROUPS


def _dot(a, b):
    return jnp.dot(a, b, preferred_element_type=F32)


def _dot_nt(a, b):
    return lax.dot_general(a, b, (((1,), (1,)), ((), ())), preferred_element_type=F32)


def _split(a, terms):
    out = []
    r = a
    for _ in range(terms):
        p = r.astype(BF16)
        out.append(p)
        r = r - p.astype(F32)
    return out


def _dot_split_lhs(a, b_bf16, terms=3):
    return sum(_dot(p, b_bf16) for p in _split(a, terms))


def _dot_split_rhs(a_bf16, b, terms=3):
    return sum(_dot(a_bf16, p) for p in _split(b, terms))


def _sigmoid(x):
    return 1.0 / (1.0 + jnp.exp(-x))


def _softplus(x):
    return jnp.maximum(x, 0.0) + jnp.log1p(jnp.exp(-jnp.abs(x)))


def _rmsnorm_rows(x, g):
    return x * lax.rsqrt(jnp.mean(x * x, axis=-1, keepdims=True) + RMS_EPS) * g


def _params(*sem):
    return pltpu.CompilerParams(dimension_semantics=sem, vmem_limit_bytes=VMEM_LIMIT)


def _full(shape):
    n = len(shape)
    return pl.BlockSpec(shape, lambda *_: (0,) * n)


def _row_tile(n):
    for t in (256, 128, 64, 32, 16, 8):
        if n % t == 0:
            return t
    raise ValueError(f"token count {n} is not a multiple of {SUBLANES}")


def _proj_body(x_ref, g_ref, wz_ref, wxbc_ref, wdt_ref, wu_ref, wq_ref, wk_ref, wv_ref,
               qn_ref, kn_ref, hm_ref,
               z_ref, xbc_ref, dt_ref, u_ref, k_ref, v_ref, qb_ref, kb_ref, vb_ref):
    hb = _rmsnorm_rows(x_ref[...], g_ref[...]).astype(BF16)
    z_ref[...] = _dot(hb, wz_ref[...])
    xbc_ref[...] = _dot(hb, wxbc_ref[...])
    dt_ref[...] = _dot(hb, wdt_ref[...])
    u_ref[...] = _dot(hb, wu_ref[...])
    hm = hm_ref[...]

    def head_norm(t, w):
        return t * lax.rsqrt(_dot_split_lhs(t * t, hm, 2) + RMS_EPS) * w

    q = head_norm(_dot(hb, wq_ref[...]), qn_ref[...])
    k = head_norm(_dot(hb, wk_ref[...]), kn_ref[...])
    v = _dot(hb, wv_ref[...])
    k_ref[...] = k
    v_ref[...] = v
    qb_ref[...] = (q * SB_SCALE).astype(BF16)
    kb_ref[...] = k.astype(BF16)
    vb_ref[...] = v.astype(BF16)


def _proj(x2, lw):
    n = x2.shape[0]
    tm = _row_tile(n)
    rows = lambda w: pl.BlockSpec((tm, w), lambda i: (i, 0))
    widths = (SSD_WIDTH, CONV_DIM, LANES, S5_WIDTH, SB_WIDTH, SB_WIDTH, SB_WIDTH, SB_WIDTH, SB_WIDTH)
    dtypes = (F32, F32, F32, F32, F32, F32, BF16, BF16, BF16)
    ins = (lw['norm_mix'], lw['wz'], lw['wxbc'], lw['wdt'], lw['wu'], lw['wq'], lw['wk'], lw['wv'],
           lw['q_norm'], lw['k_norm'], lw['head_mean'])
    return pl.pallas_call(
        _proj_body,
        grid=(n // tm,),
        in_specs=[rows(D_MODEL)] + [_full(a.shape) for a in ins],
        out_specs=[rows(w) for w in widths],
        out_shape=[jax.ShapeDtypeStruct((n, w), d) for w, d in zip(widths, dtypes)],
        compiler_params=_params("parallel"),
        name="proj",
    )(x2, *ins)


def _ssd_body(z_ref, xbc_ref, dt_ref, hist_ref, s0_ref, cw_ref, cb_ref, dtb_ref, alog_ref,
              dskip_ref, nw_ref, expand_ref, eye_ref, smask_ref,
              y_ref, sout_ref, xp_ref, st_ref, yacc_ref, *, lc):
    c = pl.program_id(1)

    @pl.when(c == 0)
    def _():
        xp_ref[0:SUBLANES, :] = hist_ref[0]
        st_ref[...] = s0_ref[0]

    xp_ref[SUBLANES:SUBLANES + lc, :] = xbc_ref[0]
    conv = cb_ref[...]
    for k in range(CONV_W):
        lo = SUBLANES - (CONV_W - 1) + k
        conv = conv + xp_ref[lo:lo + lc, :] * cw_ref[k:k + 1, :]
    xp_ref[0:SUBLANES, :] = xp_ref[lc:lc + SUBLANES, :]
    act = conv * _sigmoid(conv)
    xs = act[:, :SSD_WIDTH]
    bm = act[:, SSD_WIDTH:SSD_WIDTH + N_BC]
    cm = act[:, SSD_WIDTH + N_BC:]

    expand = expand_ref[...]
    eye = eye_ref[...]
    dt = _softplus(dt_ref[0] + dtb_ref[...])
    a = -jnp.exp(alog_ref[...])
    dta = dt * a
    ri = lax.broadcasted_iota(jnp.int32, (lc, lc), 0)
    ci = lax.broadcasted_iota(jnp.int32, (lc, lc), 1)
    causal = ri >= ci
    tri = causal.astype(BF16)
    acum = _dot_split_rhs(tri, dta)
    acum_t = sum(_dot_nt(eye, p) for p in _split(acum, 3))
    a_last = acum[lc - 1:lc, :]

    bm_b = bm.astype(BF16)
    cm_b = cm.astype(BF16)
    lane = lax.broadcasted_iota(jnp.int32, (lc, LANES), 1)
    left = lane < SSD_STATE
    xdt = (xs * _dot_split_lhs(dt, expand)).astype(BF16)
    zero_b = jnp.zeros((), BF16)
    cbs = [_dot_nt(jnp.where(left if g == 0 else ~left, cm_b, zero_b), bm_b) for g in range(SSD_GROUPS)]

    for hp in range(SSD_HEADS // 2):
        xpair = xdt[:, hp * LANES:(hp + 1) * LANES]
        outs = []
        for h in (2 * hp, 2 * hp + 1):
            seg = acum[:, h:h + 1] - acum_t[h:h + 1, :]
            m = jnp.where(causal, cbs[h // SSD_HPG] * jnp.exp(jnp.minimum(seg, 0.0)), 0.0)
            outs.append(_dot(m.astype(BF16), xpair))
        yacc_ref[:, hp * LANES:(hp + 1) * LANES] = jnp.where(left, outs[0], outs[1])

    st = st_ref[...]
    y_off = _dot(cm_b, st.astype(BF16)) * _dot_split_lhs(jnp.exp(acum), expand)
    y = (yacc_ref[...] + y_off + dskip_ref[...] * xs)
    zz = z_ref[0]
    y = y * (zz * _sigmoid(zz))
    gw = SSD_WIDTH // SSD_GROUPS
    parts = []
    for g in range(SSD_GROUPS):
        yg = y[:, g * gw:(g + 1) * gw]
        parts.append(yg * lax.rsqrt(jnp.mean(yg * yg, axis=-1, keepdims=True) + RMS_EPS))
    y_ref[0] = (jnp.concatenate(parts, axis=1) * nw_ref[...]).astype(BF16)

    to_end = jnp.exp(a_last - acum) * dt
    xw = (xs * _dot_split_lhs(to_end, expand)).astype(BF16)
    bm_t = _dot_nt(eye, bm_b).astype(BF16)
    dec = _dot_split_lhs(jnp.broadcast_to(jnp.exp(a_last), (SUBLANES, LANES)), expand)[0:1, :]
    st_new = st * dec + smask_ref[...] * _dot(bm_t, xw)
    st_ref[...] = st_new

    @pl.when(c == pl.num_programs(1) - 1)
    def _():
        sout_ref[0] = st_new


def _ssd(z3, xbc3, dt3, hist, s0, lw):
    b, t, _ = z3.shape
    lc = min(t, 128)
    shared = hist.shape[0] == 1
    per_b = (lambda i, c: (0, 0, 0)) if shared else (lambda i, c: (i, 0, 0))
    seq = lambda w: pl.BlockSpec((1, lc, w), lambda i, c: (i, c, 0))
    consts = (lw['conv_w'], lw['conv_b'], lw['dt_bias'], lw['a_log'], lw['d_ssd'], lw['norm_ssd'],
              lw['expand'], lw['eye'], lw['state_mask'])
    return pl.pallas_call(
        functools.partial(_ssd_body, lc=lc),
        grid=(b, t // lc),
        in_specs=[seq(SSD_WIDTH), seq(CONV_DIM), seq(LANES),
                  pl.BlockSpec((1, SUBLANES, CONV_DIM), per_b),
                  pl.BlockSpec((1, LANES, SSD_WIDTH), per_b)] + [_full(a.shape) for a in consts],
        out_specs=[seq(SSD_WIDTH), pl.BlockSpec((1, LANES, SSD_WIDTH), lambda i, c: (i, 0, 0))],
        out_shape=[jax.ShapeDtypeStruct((b, t, SSD_WIDTH), BF16),
                   jax.ShapeDtypeStruct((b, LANES, SSD_WIDTH), F32)],
        scratch_shapes=[pltpu.VMEM((lc + SUBLANES, CONV_DIM), F32),
                        pltpu.VMEM((LANES, SSD_WIDTH), F32),
                        pltpu.VMEM((lc, SSD_WIDTH), F32)],
        compiler_params=_params("parallel", "arbitrary"),
        name="ssd",
    )(z3, xbc3, dt3, hist, s0, *consts)


def _s5_param_body(lr_ref, li_ref, step_ref, br_ref, bi_ref, abr_ref, abi_ref, bbr_ref, bbi_ref):
    lr, li, step = lr_ref[...], li_ref[...], jnp.exp(step_ref[...])
    mag = jnp.exp(lr * step)
    ab_re = mag * jnp.cos(li * step)
    ab_im = mag * jnp.sin(li * step)
    den = lr * lr + li * li
    nr = ab_re - 1.0
    f_re = (nr * lr + ab_im * li) / den
    f_im = (ab_im * lr - nr * li) / den
    br, bi = br_ref[...], bi_ref[...]
    abr_ref[...] = ab_re
    abi_ref[...] = ab_im
    bbr_ref[...] = f_re * br - f_im * bi
    bbi_ref[...] = f_re * bi + f_im * br


def _s5_params(lam_re, lam_im, log_step, b_re, b_im):
    rep = lambda t: jnp.repeat(t, S5_GROUP_CH, axis=0)
    shape = (S5_WIDTH, S5_STATE)
    step = jnp.broadcast_to(rep(log_step[:, None]), shape)
    to_rows = lambda t: jnp.transpose(t, (0, 2, 1)).reshape(shape)
    outs = pl.pallas_call(
        _s5_param_body,
        out_shape=[jax.ShapeDtypeStruct(shape, F32)] * 4,
        name="s5_params",
    )(rep(lam_re), rep(lam_im), step, to_rows(b_re), to_rows(b_im))
    return outs


def _s5_body(u_ref, sre_ref, sim_ref, abr_ref, abi_ref, bbr_ref, bbi_ref, ccr_ref, cci_ref, dsk_ref,
             y_ref, ore_ref, oim_ref, xre_ref, xim_ref, pwr_ref, pwi_ref, cr_ref, ci_ref, *, lt):
    c = pl.program_id(1)

    @pl.when(c == 0)
    def _():
        pr, pi = abr_ref[...], abi_ref[...]
        ar, ai = pr, pi
        for r in range(SUBLANES):
            pwr_ref[r:r + 1, :] = pr
            pwi_ref[r:r + 1, :] = pi
            pr, pi = pr * ar - pi * ai, pr * ai + pi * ar
        cr_ref[...] = sre_ref[0]
        ci_ref[...] = sim_ref[0]

    row = lax.broadcasted_iota(jnp.int32, (SUBLANES, S5_BLK_LANES), 0)
    u = u_ref[0]
    for j in range(S5_NBLK):
        lanes = slice(j * S5_BLK_LANES, (j + 1) * S5_BLK_LANES)
        ub = u[:, j * LANES:(j + 1) * LANES]
        ub16 = ub.astype(BF16)
        xre_ref[...] = _dot(ub16, bbr_ref[j])
        xim_ref[...] = _dot(ub16, bbi_ref[j])
        pwr = pwr_ref[:, lanes]
        pwi = pwi_ref[:, lanes]

        def group(i, carry):
            cr, ci = carry
            rows = pl.ds(pl.multiple_of(i * SUBLANES, SUBLANES), SUBLANES)
            r, m = xre_ref[rows, :], xim_ref[rows, :]
            for sh in (1, 2, 4):
                ar, ai = pwr[sh - 1:sh, :], pwi[sh - 1:sh, :]
                rs = jnp.where(row >= sh, pltpu.roll(r, sh, 0), 0.0)
                ms = jnp.where(row >= sh, pltpu.roll(m, sh, 0), 0.0)
                r, m = r + (ar * rs - ai * ms), m + (ar * ms + ai * rs)
            r, m = r + (pwr * cr - pwi * ci), m + (pwr * ci + pwi * cr)
            xre_ref[rows, :] = r
            xim_ref[rows, :] = m
            return r[SUBLANES - 1:SUBLANES, :], m[SUBLANES - 1:SUBLANES, :]

        cr, ci = lax.fori_loop(0, lt // SUBLANES, group, (cr_ref[:, lanes], ci_ref[:, lanes]))
        cr_ref[:, lanes] = cr
        ci_ref[:, lanes] = ci
        yb = (_dot(xre_ref[...].astype(BF16), ccr_ref[j]) - _dot(xim_ref[...].astype(BF16), cci_ref[j])
              + dsk_ref[:, j * LANES:(j + 1) * LANES] * ub)
        gelu = 0.5 * yb * (1.0 + jnp.tanh(math.sqrt(2.0 / math.pi) * (yb + 0.044715 * (yb * yb * yb))))
        y_ref[0, :, j * LANES:(j + 1) * LANES] = gelu.astype(BF16)

    @pl.when(c == pl.num_programs(1) - 1)
    def _():
        ore_ref[0] = cr_ref[...]
        oim_ref[0] = ci_ref[...]


def _s5(u3, s_re, s_im, lw):
    b, t, _ = u3.shape
    lt = min(t, 256)
    shared = s_re.shape[0] == 1
    per_b = (lambda i, c: (0, 0, 0)) if shared else (lambda i, c: (i, 0, 0))
    state = pl.BlockSpec((1, 1, S5_LANES), per_b)
    state_out = pl.BlockSpec((1, 1, S5_LANES), lambda i, c: (i, 0, 0))
    consts = (lw['s5_ab_re'], lw['s5_ab_im'], lw['s5_bb_re'], lw['s5_bb_im'], lw['s5_cc_re'],
              lw['s5_cc_im'], lw['d_s5'])
    return pl.pallas_call(
        functools.partial(_s5_body, lt=lt),
        grid=(b, t // lt),
        in_specs=[pl.BlockSpec((1, lt, S5_WIDTH), lambda i, c: (i, c, 0)), state, state]
                 + [_full(a.shape) for a in consts],
        out_specs=[pl.BlockSpec((1, lt, S5_WIDTH), lambda i, c: (i, c, 0)), state_out, state_out],
        out_shape=[jax.ShapeDtypeStruct((b, t, S5_WIDTH), BF16),
                   jax.ShapeDtypeStruct((b, 1, S5_LANES), F32),
                   jax.ShapeDtypeStruct((b, 1, S5_LANES), F32)],
        scratch_shapes=[pltpu.VMEM((lt, S5_BLK_LANES), F32), pltpu.VMEM((lt, S5_BLK_LANES), F32),
                        pltpu.VMEM((SUBLANES, S5_LANES), F32), pltpu.VMEM((SUBLANES, S5_LANES), F32),
                        pltpu.VMEM((1, S5_LANES), F32), pltpu.VMEM((1, S5_LANES), F32)],
        compiler_params=_params("parallel", "arbitrary"),
        name="s5",
    )(u3, s_re, s_im, *consts)


SB_KEYS = 256
SB_PAIRS = SB_WIDTH // LANES
SB_DEAD = -105.0


def _upper(n):
    r = lax.broadcasted_iota(jnp.int32, (n, n), 0)
    c = lax.broadcasted_iota(jnp.int32, (n, n), 1)
    return (r > c).astype(BF16)


def _attn_body(q_ref, k_ref, v_ref, o_ref, acc_ref, carry_ref, *, bq, n_hist):
    qi = pl.program_id(1)
    q = q_ref[0]
    lane = lax.broadcasted_iota(jnp.int32, (bq, LANES), 1)
    left = lane < SB_HEAD_DIM
    zero_b = jnp.zeros((), BF16)
    qms = [jnp.concatenate([jnp.where(left, q[:, p * LANES:(p + 1) * LANES], zero_b),
                            jnp.where(left, zero_b, q[:, p * LANES:(p + 1) * LANES])], axis=0)
           for p in range(SB_PAIRS)]
    ri = lax.broadcasted_iota(jnp.int32, (2 * bq, SB_KEYS), 0)
    ci = lax.broadcasted_iota(jnp.int32, (2 * bq, SB_KEYS), 1)
    ri = jnp.where(ri >= bq, ri - bq, ri)
    strictly_earlier = ci - (SB_KEYS - bq) < ri
    upper = _upper(SB_KEYS)
    n_before = n_hist + (qi + 1) * bq
    end = SB_KEYS + n_before
    trips = (n_before + SB_KEYS - 1) // SB_KEYS
    acc_ref[...] = jnp.zeros_like(acc_ref)
    carry_ref[...] = jnp.zeros_like(carry_ref)

    def visit(it, masked):
        rows = pl.ds(pl.multiple_of(end - (it + 1) * SB_KEYS, 16), SB_KEYS)
        zs = [_dot_nt(qms[p], k_ref[0, rows, p * LANES:(p + 1) * LANES]) for p in range(SB_PAIRS)]
        log_betas, log_keeps, laters = [], [], []
        for z in zs:
            sp = jnp.log(1.0 + jnp.exp(-jnp.abs(z)))
            log_beta = jnp.minimum(z, 0.0) - sp
            log_keep = log_beta - z
            if masked:
                log_keep = jnp.where(strictly_earlier, log_keep, 0.0)
            hi = log_keep.astype(BF16)
            lo = (log_keep - hi.astype(F32)).astype(BF16)
            both = _dot(jnp.concatenate([hi, lo], axis=0), upper)
            log_betas.append(log_beta)
            log_keeps.append(log_keep)
            laters.append(both[:2 * bq] + both[2 * bq:])
        for p in range(SB_PAIRS):
            carry = carry_ref[p]
            w = jnp.exp(log_betas[p] + laters[p] + carry)
            if masked:
                w = jnp.where(strictly_earlier, w, 0.0)
            pv = _dot(w.astype(BF16), v_ref[0, rows, p * LANES:(p + 1) * LANES])
            carry_ref[p] = carry + (laters[p][:, 0:1] + log_keeps[p][:, 0:1])
            acc_ref[:, p * LANES:(p + 1) * LANES] += jnp.where(left, pv[:bq], pv[bq:])

    def any_weight_left():
        c = carry_ref[0]
        for p in range(1, SB_PAIRS):
            c = jnp.maximum(c, carry_ref[p])
        return jnp.max(c) > SB_DEAD

    visit(0, True)

    def trip(state):
        it, _ = state
        visit(it, False)
        return it + 1, any_weight_left()

    lax.while_loop(lambda s: jnp.logical_and(s[0] < trips, s[1]), trip, (jnp.int32(1), any_weight_left()))
    o_ref[0] = acc_ref[...].astype(BF16)


def _attn(qb, kb, vb, kh, vh):
    b, t, _ = qb.shape
    bq = min(t, 128)

    def keys(new, hist):
        parts = [jnp.zeros((b, SB_KEYS, SB_WIDTH), BF16)]
        if hist is not None:
            parts.append(jnp.broadcast_to(hist, (b,) + hist.shape[1:]))
        return jnp.concatenate(parts + [new], axis=1)

    n_hist = 0 if kh is None else kh.shape[1]
    k_all, v_all = keys(kb, kh), keys(vb, vh)
    whole = pl.BlockSpec((1, k_all.shape[1], SB_WIDTH), lambda i, j: (i, 0, 0))
    block = pl.BlockSpec((1, bq, SB_WIDTH), lambda i, j: (i, j, 0))
    return pl.pallas_call(
        functools.partial(_attn_body, bq=bq, n_hist=n_hist),
        grid=(b, t // bq),
        in_specs=[block, whole, whole],
        out_specs=block,
        out_shape=jax.ShapeDtypeStruct((b, t, SB_WIDTH), BF16),
        scratch_shapes=[pltpu.VMEM((bq, SB_WIDTH), F32), pltpu.VMEM((SB_PAIRS, 2 * bq, 1), F32)],
        compiler_params=_params("parallel", "arbitrary"),
        name="attn",
    )(qb, k_all, v_all)


def _merge_body(x_ref, ya_ref, yb_ref, oc_ref, g_ref, wg_ref, wa_ref, wglu_ref, wc_ref, wo_ref, o_ref):
    x = x_ref[...]
    hb = _rmsnorm_rows(x, g_ref[...]).astype(BF16)
    gates = _sigmoid(_dot(hb, wg_ref[...]))
    glu = _dot(yb_ref[...], wglu_ref[...])
    mix = (gates[:, :D_MODEL] * _dot(ya_ref[...], wa_ref[...])
           + gates[:, D_MODEL:2 * D_MODEL] * (glu[:, :D_MODEL] * _sigmoid(glu[:, D_MODEL:]))
           + gates[:, 2 * D_MODEL:] * _dot(oc_ref[...], wc_ref[...]))
    o_ref[...] = x + _dot(mix.astype(BF16), wo_ref[...])


def _merge(x2, ya, yb, oc, lw):
    n = x2.shape[0]
    tm = _row_tile(n)
    rows = lambda w: pl.BlockSpec((tm, w), lambda i: (i, 0))
    consts = (lw['norm_mix'], lw['wgate'], lw['w_lift_a'], lw['w_glu'], lw['w_lift_c'], lw['w_out'])
    return pl.pallas_call(
        _merge_body,
        grid=(n // tm,),
        in_specs=[rows(D_MODEL), rows(SSD_WIDTH), rows(S5_WIDTH), rows(SB_WIDTH)]
                 + [_full(a.shape) for a in consts],
        out_specs=rows(D_MODEL),
        out_shape=jax.ShapeDtypeStruct((n, D_MODEL), F32),
        compiler_params=_params("parallel"),
        name="merge",
    )(x2, ya, yb, oc, *consts)


FF_CHUNK = 1024


def _ffn_body(x_ref, g_ref, wu_ref, wd_ref, o_ref):
    x = x_ref[...]
    hb = _rmsnorm_rows(x, g_ref[...]).astype(BF16)
    acc = x
    for j in range(D_FF // FF_CHUNK):
        cols = slice(j * FF_CHUNK, (j + 1) * FF_CHUNK)
        up = jnp.maximum(_dot(hb, wu_ref[:, cols]), 0.0)
        acc = acc + _dot((up * up).astype(BF16), wd_ref[cols, :])
    o_ref[...] = acc


def _ffn(x2, lw):
    n = x2.shape[0]
    tm = _row_tile(n)
    rows = pl.BlockSpec((tm, D_MODEL), lambda i: (i, 0))
    consts = (lw['norm_ffn'], lw['w_up'], lw['w_down'])
    return pl.pallas_call(
        _ffn_body,
        grid=(n // tm,),
        in_specs=[rows] + [_full(a.shape) for a in consts],
        out_specs=rows,
        out_shape=jax.ShapeDtypeStruct((n, D_MODEL), F32),
        compiler_params=_params("parallel"),
        name="ffn",
    )(x2, *consts)


def _block_diag(blocks):
    n, r, c = blocks.shape
    eye = jnp.eye(n, dtype=blocks.dtype)
    return (eye[:, None, :, None] * blocks[:, :, None, :]).reshape(n * r, n * c)


def _layer_weights(p, l):
    w_in = p['w_in'][l]
    col = lambda off, n: w_in[:, off:off + n].astype(BF16)
    row = lambda v: v.reshape(1, -1).astype(F32)
    heads = jnp.arange(SSD_WIDTH) // SSD_HEAD_DIM
    pad_lanes = lambda v: jnp.pad(v, (0, LANES - v.shape[0])).reshape(1, LANES)
    ab_re, ab_im, bb_re, bb_im = _s5_params(p['lam_re'][l], p['lam_im'][l], p['log_step'][l],
                                            p['b_re'][l], p['b_im'][l])
    to_bb = lambda t: jnp.stack([_block_diag(blk) for blk in
                                 t.reshape(S5_NBLK, S5_BLK_GROUPS, S5_GROUP_CH, S5_STATE)]).astype(BF16)
    to_cc = lambda t: jnp.stack([_block_diag(blk) for blk in
                                 jnp.transpose(t, (0, 2, 1)).reshape(S5_NBLK, S5_BLK_GROUPS, S5_STATE, S5_GROUP_CH)]
                                ).astype(BF16)
    state_rows = jnp.arange(LANES) // SSD_STATE
    state_cols = jnp.arange(SSD_WIDTH) // (SSD_WIDTH // SSD_GROUPS)
    return {
        'norm_mix': row(p['norm_mix'][l]),
        'wz': col(OFF_Z, SSD_WIDTH), 'wxbc': col(OFF_XBC, CONV_DIM),
        'wdt': jnp.pad(col(OFF_DT, SSD_HEADS), ((0, 0), (0, LANES - SSD_HEADS))),
        'wu': col(OFF_U, S5_WIDTH), 'wq': col(OFF_Q, SB_WIDTH), 'wk': col(OFF_K, SB_WIDTH),
        'wv': col(OFF_V, SB_WIDTH), 'wgate': col(OFF_GATE, N_BRANCH * D_MODEL),
        'q_norm': row(jnp.tile(p['q_norm'][l], SB_HEADS)), 'k_norm': row(jnp.tile(p['k_norm'][l], SB_HEADS)),
        'head_mean': (_block_diag(jnp.ones((SB_HEADS, SB_HEAD_DIM, SB_HEAD_DIM), F32)) / SB_HEAD_DIM).astype(BF16),
        'conv_w': jnp.pad(p['conv_w'][l], ((0, SUBLANES - CONV_W), (0, 0))),
        'conv_b': row(p['conv_b'][l]),
        'dt_bias': pad_lanes(p['dt_bias'][l]), 'a_log': pad_lanes(p['a_log'][l]),
        'd_ssd': row(p['d_ssd'][l][heads]), 'norm_ssd': row(p['norm_ssd'][l]),
        'expand': (jnp.arange(LANES)[:, None] == heads[None, :]).astype(BF16),
        'eye': jnp.eye(LANES, dtype=BF16),
        'state_mask': (state_rows[:, None] == state_cols[None, :]).astype(F32),
        's5_ab_re': ab_re[::S5_GROUP_CH].reshape(1, S5_LANES), 's5_ab_im': ab_im[::S5_GROUP_CH].reshape(1, S5_LANES),
        's5_bb_re': to_bb(bb_re), 's5_bb_im': to_bb(bb_im),
        's5_cc_re': to_cc(p['c_re'][l]), 's5_cc_im': to_cc(p['c_im'][l]),
        'd_s5': row(p['d_s5'][l]),
        'w_glu': p['w_glu'][l].astype(BF16), 'w_lift_a': p['w_lift_a'][l].astype(BF16),
        'w_lift_c': p['w_lift_c'][l].astype(BF16), 'w_out': p['w_out'][l].astype(BF16),
        'norm_ffn': row(p['norm_ffn'][l]), 'w_up': p['w_up'][l].astype(BF16), 'w_down': p['w_down'][l].astype(BF16),
    }


def _ssd_state_in(s):
    b = s.shape[0]
    t = jnp.transpose(s.astype(F32), (0, 1, 4, 2, 3)).reshape(b, SSD_GROUPS, SSD_STATE, SSD_HPG * SSD_HEAD_DIM)
    eye = jnp.eye(SSD_GROUPS, dtype=F32)
    return (t[:, :, :, None, :] * eye[None, :, None, :, None]).reshape(b, LANES, SSD_WIDTH)


def _ssd_state_out(s):
    b = s.shape[0]
    t = s.reshape(b, SSD_GROUPS, SSD_STATE, SSD_GROUPS, SSD_HPG, SSD_HEAD_DIM)
    t = jnp.stack([t[:, g, :, g] for g in range(SSD_GROUPS)], axis=1)
    return jnp.transpose(t, (0, 1, 3, 4, 2))


def _trunk_layer(x, k_hist, v_hist, conv_hist, ssd_s0, s5_re0, s5_im0, lw):
    b, t, _ = x.shape
    x2 = x.reshape(b * t, D_MODEL)
    z, xbc, dt, u, k, v, qb, kb, vb = _proj(x2, lw)
    seq = lambda a: a.reshape(b, t, a.shape[-1])
    hist8 = jnp.pad(conv_hist.astype(F32), ((0, 0), (SUBLANES - (CONV_W - 1), 0), (0, 0)))
    y_a, ssd_new = _ssd(seq(z), seq(xbc), seq(dt), hist8, _ssd_state_in(ssd_s0), lw)
    y_b, s5_re, s5_im = _s5(seq(u), s5_re0.reshape(-1, 1, S5_LANES).astype(F32),
                            s5_im0.reshape(-1, 1, S5_LANES).astype(F32), lw)
    if k_hist is None:
        kh = vh = None
    else:
        kh = k_hist.reshape(k_hist.shape[0], -1, SB_WIDTH).astype(BF16)
        vh = v_hist.reshape(v_hist.shape[0], -1, SB_WIDTH).astype(BF16)
    o_c = _attn(seq(qb), seq(kb), seq(vb), kh, vh)
    x2 = _merge(x2, y_a.reshape(b * t, SSD_WIDTH), y_b.reshape(b * t, S5_WIDTH), o_c.reshape(b * t, SB_WIDTH), lw)
    x2 = _ffn(x2, lw)
    conv_rows = jnp.concatenate([jnp.broadcast_to(conv_hist.astype(F32), (b, CONV_W - 1, CONV_DIM)),
                                 seq(xbc)[:, -(CONV_W - 1):]], axis=1)[:, -(CONV_W - 1):]
    heads = lambda a: a.reshape(b, t, SB_HEADS, SB_HEAD_DIM)
    return (x2.reshape(b, t, D_MODEL), heads(k), heads(v), conv_rows, _ssd_state_out(ssd_new),
            s5_re.reshape(b, S5_GROUPS, S5_STATE), s5_im.reshape(b, S5_GROUPS, S5_STATE))


def kernel(x_prompt, x_sample, cache_k, cache_v, state_conv, state_ssd, state_s5_re, state_s5_im, meta_tokens, norm_mix, w_in, conv_w, conv_b, dt_bias, a_log, d_ssd, norm_ssd, lam_re, lam_im, log_step, b_re, b_im, c_re, c_im, d_s5, w_glu, q_norm, k_norm, w_lift_a, w_lift_c, w_out, norm_ffn, w_up, w_down):
    p = dict(norm_mix=norm_mix, w_in=w_in, conv_w=conv_w, conv_b=conv_b, dt_bias=dt_bias, a_log=a_log,
             d_ssd=d_ssd, norm_ssd=norm_ssd, lam_re=lam_re, lam_im=lam_im, log_step=log_step, b_re=b_re,
             b_im=b_im, c_re=c_re, c_im=c_im, d_s5=d_s5, w_glu=w_glu, q_norm=q_norm, k_norm=k_norm,
             w_lift_a=w_lift_a, w_lift_c=w_lift_c, w_out=w_out, norm_ffn=norm_ffn, w_up=w_up, w_down=w_down)
    depth = w_in.shape[0]
    bp = x_prompt.shape[0]
    xm = meta_tokens.astype(x_prompt.dtype)[None]
    xp, xs = x_prompt, x_sample
    zeros = lambda *s: jnp.zeros(s, F32)
    outs_p, outs_s = [], []
    kv_shape = (depth, bp, N_META + x_prompt.shape[1], SB_HEADS, SB_HEAD_DIM)
    k_prompt, v_prompt = jnp.zeros(kv_shape, F32), jnp.zeros(kv_shape, F32)
    for l in range(depth):
        lw = _layer_weights(p, l)
        xm, *st_m = _trunk_layer(xm, None, None, zeros(1, CONV_W - 1, CONV_DIM),
                                 zeros(1, SSD_GROUPS, SSD_HPG, SSD_HEAD_DIM, SSD_STATE),
                                 zeros(1, S5_GROUPS, S5_STATE), zeros(1, S5_GROUPS, S5_STATE), lw)
        k_m, v_m, conv_m, ssd_m, s5re_m, s5im_m = st_m
        xp, k_p, v_p, *st_p = _trunk_layer(xp, k_m, v_m, conv_m, ssd_m, s5re_m, s5im_m, lw)
        k_prompt = k_prompt.at[l, :, :N_META].set(jnp.broadcast_to(k_m, (bp,) + k_m.shape[1:]))
        v_prompt = v_prompt.at[l, :, :N_META].set(jnp.broadcast_to(v_m, (bp,) + v_m.shape[1:]))
        k_prompt = k_prompt.at[l, :, N_META:].set(k_p)
        v_prompt = v_prompt.at[l, :, N_META:].set(v_p)
        outs_p.append([None, None] + st_p)
        xs, *st_s = _trunk_layer(xs, cache_k[l], cache_v[l], state_conv[l], state_ssd[l],
                                 state_s5_re[l], state_s5_im[l], lw)
        outs_s.append(st_s)
    stk = lambda outs, i: jnp.stack([o[i] for o in outs], axis=0)
    return (xp, xs,
            k_prompt, v_prompt, stk(outs_p, 2), stk(outs_p, 3), stk(outs_p, 4), stk(outs_p, 5),
            stk(outs_s, 0), stk(outs_s, 1), stk(outs_s, 2), stk(outs_s, 3), stk(outs_s, 4), stk(outs_s, 5))
```

```python
import functools
import math

import jax
import jax.numpy as jnp
from jax import lax
from jax.experimental import pallas as pl
from jax.experimental.pallas import tpu as pltpu

F32 = jnp.float32
BF16 = jnp.bfloat16

D_MODEL = 1024
N_META = 16
RMS_EPS = 1e-6
SSD_HEADS = 16
SSD_HEAD_DIM = 64
SSD_GROUPS = 2
SSD_HPG = SSD_HEADS // SSD_GROUPS
SSD_STATE = 64
SSD_WIDTH = SSD_HEADS * SSD_HEAD_DIM
CONV_W = 4
N_BC = SSD_GROUPS * SSD_STATE
CONV_DIM = SSD_WIDTH + 2 * N_BC
S5_WIDTH = D_MODEL // 2
S5_GROUP_CH = 16
S5_GROUPS = S5_WIDTH // S5_GROUP_CH
S5_STATE = 64
S5_LANES = S5_GROUPS * S5_STATE
SB_HEADS = 8
SB_HEAD_DIM = 64
SB_WIDTH = SB_HEADS * SB_HEAD_DIM
SB_SCALE = 1.0 / math.sqrt(SB_HEAD_DIM)
N_BRANCH = 3
D_FF = 4 * D_MODEL
OFF_Z = 0
OFF_XBC = OFF_Z + SSD_WIDTH
OFF_DT = OFF_XBC + CONV_DIM
OFF_U = OFF_DT + SSD_HEADS
OFF_Q = OFF_U + S5_WIDTH
OFF_K = OFF_Q + SB_WIDTH
OFF_V = OFF_K + SB_WIDTH
OFF_GATE = OFF_V + SB_WIDTH

LANES = 128
SUBLANES = 8
VMEM_LIMIT = 56 * 1024 * 1024
S5_BLK_GROUPS = LANES // S5_GROUP_CH
S5_BLK_LANES = S5_BLK_GROUPS * S5_STATE
S5_NBLK = S5_GROUPS // S5_BLK_GROUPS


def _dot(a, b):
    return jnp.dot(a, b, preferred_element_type=F32)


def _dot_nt(a, b):
    return lax.dot_general(a, b, (((1,), (1,)), ((), ())), preferred_element_type=F32)


def _split(a, terms):
    out = []
    r = a
    for _ in range(terms):
        p = r.astype(BF16)
        out.append(p)
        r = r - p.astype(F32)
    return out


def _dot_split_lhs(a, b_bf16, terms=3):
    return sum(_dot(p, b_bf16) for p in _split(a, terms))


def _dot_split_rhs(a_bf16, b, terms=3):
    return sum(_dot(a_bf16, p) for p in _split(b, terms))


def _sigmoid(x):
    return 1.0 / (1.0 + jnp.exp(-x))


def _softplus(x):
    return jnp.maximum(x, 0.0) + jnp.log1p(jnp.exp(-jnp.abs(x)))


def _rmsnorm_rows(x, g):
    return x * lax.rsqrt(jnp.mean(x * x, axis=-1, keepdims=True) + RMS_EPS) * g


def _params(*sem):
    return pltpu.CompilerParams(dimension_semantics=sem, vmem_limit_bytes=VMEM_LIMIT)


def _full(shape):
    n = len(shape)
    return pl.BlockSpec(shape, lambda *_: (0,) * n)


def _row_tile(n):
    for t in (256, 128, 64, 32, 16, 8):
        if n % t == 0:
            return t
    raise ValueError(f"token count {n} is not a multiple of {SUBLANES}")


def _proj_body(x_ref, g_ref, wz_ref, wxbc_ref, wdt_ref, wu_ref, wq_ref, wk_ref, wv_ref,
               qn_ref, kn_ref, hm_ref, kbuf_ref, vbuf_ref,
               z_ref, xbc_ref, dt_ref, u_ref, k_ref, v_ref, qb_ref, kb_ref, vb_ref):
    del kbuf_ref, vbuf_ref
    hb = _rmsnorm_rows(x_ref[...], g_ref[...]).astype(BF16)
    z_ref[...] = _dot(hb, wz_ref[...])
    xbc_ref[...] = _dot(hb, wxbc_ref[...])
    dt_ref[...] = _dot(hb, wdt_ref[...])
    u_ref[...] = _dot(hb, wu_ref[...])
    hm = hm_ref[...]

    def head_norm(t, w):
        return t * lax.rsqrt(_dot_split_lhs(t * t, hm, 2) + RMS_EPS) * w

    q = head_norm(_dot(hb, wq_ref[...]), qn_ref[...])
    k = head_norm(_dot(hb, wk_ref[...]), kn_ref[...])
    v = _dot(hb, wv_ref[...])
    k_ref[0] = k
    v_ref[0] = v
    qb_ref[...] = (q * SB_SCALE).astype(BF16)
    kb_ref[...] = k.astype(BF16)
    vb_ref[...] = v.astype(BF16)


def _proj(x2, layer, k_buf, v_buf, lw):
    n = x2.shape[0]
    tm = _row_tile(n)
    rows = lambda w: pl.BlockSpec((tm, w), lambda i: (i, 0))
    stacked = pl.BlockSpec((1, tm, SB_WIDTH), lambda i: (layer, i, 0))
    ins = (lw['norm_mix'], lw['wz'], lw['wxbc'], lw['wdt'], lw['wu'], lw['wq'], lw['wk'], lw['wv'],
           lw['q_norm'], lw['k_norm'], lw['head_mean'])
    sds = jax.ShapeDtypeStruct
    return pl.pallas_call(
        _proj_body,
        grid=(n // tm,),
        in_specs=[rows(D_MODEL)] + [_full(a.shape) for a in ins] + [pl.BlockSpec(memory_space=pl.ANY)] * 2,
        out_specs=[rows(SSD_WIDTH), rows(CONV_DIM), rows(LANES), rows(S5_WIDTH),
                   stacked, stacked, rows(SB_WIDTH), rows(SB_WIDTH), rows(SB_WIDTH)],
        out_shape=[sds((n, SSD_WIDTH), F32), sds((n, CONV_DIM), F32), sds((n, LANES), F32),
                   sds((n, S5_WIDTH), F32), sds(k_buf.shape, F32), sds(v_buf.shape, F32),
                   sds((n, SB_WIDTH), BF16), sds((n, SB_WIDTH), BF16), sds((n, SB_WIDTH), BF16)],
        input_output_aliases={1 + len(ins): 4, 2 + len(ins): 5},
        compiler_params=_params("parallel"),
        name="proj",
    )(x2, *ins, k_buf, v_buf)


def _ssd_body(z_ref, xbc_ref, dt_ref, hist_ref, s0_ref, cw_ref, cb_ref, dtb_ref, alog_ref,
              dskip_ref, nw_ref, expand_ref, eye_ref, smask_ref,
              y_ref, sout_ref, xp_ref, st_ref, yacc_ref, *, lc):
    c = pl.program_id(1)

    @pl.when(c == 0)
    def _():
        xp_ref[0:SUBLANES, :] = hist_ref[0]
        st_ref[...] = s0_ref[0]

    xp_ref[SUBLANES:SUBLANES + lc, :] = xbc_ref[0]
    conv = cb_ref[...]
    for k in range(CONV_W):
        lo = SUBLANES - (CONV_W - 1) + k
        conv = conv + xp_ref[lo:lo + lc, :] * cw_ref[k:k + 1, :]
    xp_ref[0:SUBLANES, :] = xp_ref[lc:lc + SUBLANES, :]
    act = conv * _sigmoid(conv)
    xs = act[:, :SSD_WIDTH]
    bm = act[:, SSD_WIDTH:SSD_WIDTH + N_BC]
    cm = act[:, SSD_WIDTH + N_BC:]

    expand = expand_ref[...]
    eye = eye_ref[...]
    dt = _softplus(dt_ref[0] + dtb_ref[...])
    a = -jnp.exp(alog_ref[...])
    dta = dt * a
    ri = lax.broadcasted_iota(jnp.int32, (lc, lc), 0)
    ci = lax.broadcasted_iota(jnp.int32, (lc, lc), 1)
    causal = ri >= ci
    tri = causal.astype(BF16)
    acum = _dot_split_rhs(tri, dta)
    acum_t = sum(_dot_nt(eye, p) for p in _split(acum, 3))
    a_last = acum[lc - 1:lc, :]

    bm_b = bm.astype(BF16)
    cm_b = cm.astype(BF16)
    lane = lax.broadcasted_iota(jnp.int32, (lc, LANES), 1)
    left = lane < SSD_STATE
    xdt = (xs * _dot_split_lhs(dt, expand)).astype(BF16)
    zero_b = jnp.zeros((), BF16)
    cbs = [_dot_nt(jnp.where(left if g == 0 else ~left, cm_b, zero_b), bm_b) for g in range(SSD_GROUPS)]

    for hp in range(SSD_HEADS // 2):
        xpair = xdt[:, hp * LANES:(hp + 1) * LANES]
        outs = []
        for h in (2 * hp, 2 * hp + 1):
            seg = acum[:, h:h + 1] - acum_t[h:h + 1, :]
            m = jnp.where(causal, cbs[h // SSD_HPG] * jnp.exp(jnp.minimum(seg, 0.0)), 0.0)
            outs.append(_dot(m.astype(BF16), xpair))
        yacc_ref[:, hp * LANES:(hp + 1) * LANES] = jnp.where(left, outs[0], outs[1])

    st = st_ref[...]
    y_off = _dot(cm_b, st.astype(BF16)) * _dot_split_lhs(jnp.exp(acum), expand)
    y = (yacc_ref[...] + y_off + dskip_ref[...] * xs)
    zz = z_ref[0]
    y = y * (zz * _sigmoid(zz))
    gw = SSD_WIDTH // SSD_GROUPS
    parts = []
    for g in range(SSD_GROUPS):
        yg = y[:, g * gw:(g + 1) * gw]
        parts.append(yg * lax.rsqrt(jnp.mean(yg * yg, axis=-1, keepdims=True) + RMS_EPS))
    y_ref[0] = (jnp.concatenate(parts, axis=1) * nw_ref[...]).astype(BF16)

    to_end = jnp.exp(a_last - acum) * dt
    xw = (xs * _dot_split_lhs(to_end, expand)).astype(BF16)
    bm_t = _dot_nt(eye, bm_b).astype(BF16)
    dec = _dot_split_lhs(jnp.broadcast_to(jnp.exp(a_last), (SUBLANES, LANES)), expand)[0:1, :]
    st_new = st * dec + smask_ref[...] * _dot(bm_t, xw)
    st_ref[...] = st_new

    @pl.when(c == pl.num_programs(1) - 1)
    def _():
        sout_ref[0] = st_new


def _ssd(z3, xbc3, dt3, hist, s0, lw):
    b, t, _ = z3.shape
    lc = min(t, 128)
    shared = hist.shape[0] == 1
    per_b = (lambda i, c: (0, 0, 0)) if shared else (lambda i, c: (i, 0, 0))
    seq = lambda w: pl.BlockSpec((1, lc, w), lambda i, c: (i, c, 0))
    consts = (lw['conv_w'], lw['conv_b'], lw['dt_bias'], lw['a_log'], lw['d_ssd'], lw['norm_ssd'],
              lw['expand'], lw['eye'], lw['state_mask'])
    return pl.pallas_call(
        functools.partial(_ssd_body, lc=lc),
        grid=(b, t // lc),
        in_specs=[seq(SSD_WIDTH), seq(CONV_DIM), seq(LANES),
                  pl.BlockSpec((1, SUBLANES, CONV_DIM), per_b),
                  pl.BlockSpec((1, LANES, SSD_WIDTH), per_b)] + [_full(a.shape) for a in consts],
        out_specs=[seq(SSD_WIDTH), pl.BlockSpec((1, LANES, SSD_WIDTH), lambda i, c: (i, 0, 0))],
        out_shape=[jax.ShapeDtypeStruct((b, t, SSD_WIDTH), BF16),
                   jax.ShapeDtypeStruct((b, LANES, SSD_WIDTH), F32)],
        scratch_shapes=[pltpu.VMEM((lc + SUBLANES, CONV_DIM), F32),
                        pltpu.VMEM((LANES, SSD_WIDTH), F32),
                        pltpu.VMEM((lc, SSD_WIDTH), F32)],
        compiler_params=_params("parallel", "arbitrary"),
        name="ssd",
    )(z3, xbc3, dt3, hist, s0, *consts)


def _s5_param_body(lr_ref, li_ref, step_ref, br_ref, bi_ref, abr_ref, abi_ref, bbr_ref, bbi_ref):
    lr, li, step = lr_ref[...], li_ref[...], jnp.exp(step_ref[...])
    mag = jnp.exp(lr * step)
    ab_re = mag * jnp.cos(li * step)
    ab_im = mag * jnp.sin(li * step)
    den = lr * lr + li * li
    nr = ab_re - 1.0
    f_re = (nr * lr + ab_im * li) / den
    f_im = (ab_im * lr - nr * li) / den
    br, bi = br_ref[...], bi_ref[...]
    abr_ref[...] = ab_re
    abi_ref[...] = ab_im
    bbr_ref[...] = f_re * br - f_im * bi
    bbi_ref[...] = f_re * bi + f_im * br


def _s5_params(lam_re, lam_im, log_step, b_re, b_im):
    rep = lambda t: jnp.repeat(t, S5_GROUP_CH, axis=0)
    shape = (S5_WIDTH, S5_STATE)
    step = jnp.broadcast_to(rep(log_step[:, None]), shape)
    to_rows = lambda t: jnp.transpose(t, (0, 2, 1)).reshape(shape)
    outs = pl.pallas_call(
        _s5_param_body,
        out_shape=[jax.ShapeDtypeStruct(shape, F32)] * 4,
        name="s5_params",
    )(rep(lam_re), rep(lam_im), step, to_rows(b_re), to_rows(b_im))
    return outs


S5_STEPS = 16
S5_SEQS = 16
S5_SCAN_LANES = 256


def _s5_body(u_ref, sre_ref, sim_ref, abr_ref, abi_ref, bbr_ref, bbi_ref, ccr_ref, cci_ref, dsk_ref,
             perm_ref, perm_t_ref, y_ref, ore_ref, oim_ref, xre_ref, xim_ref, cr_ref, ci_ref, *, nb):
    c = pl.program_id(1)
    rows = nb * S5_STEPS

    @pl.when(c == 0)
    def _():
        cr_ref[...] = jnp.broadcast_to(sre_ref[...], cr_ref.shape)
        ci_ref[...] = jnp.broadcast_to(sim_ref[...], ci_ref.shape)

    u = u_ref[...].reshape(rows, S5_WIDTH)
    u_tm = _dot(perm_ref[...], u.astype(BF16)).astype(BF16)
    perm_t = perm_t_ref[...]
    for j in range(S5_NBLK):
        ub = u_tm[:, j * LANES:(j + 1) * LANES]
        xre_ref[...] = _dot(ub, bbr_ref[j])
        xim_ref[...] = _dot(ub, bbi_ref[j])
        for h in range(S5_BLK_LANES // S5_SCAN_LANES):
            cols = slice(h * S5_SCAN_LANES, (h + 1) * S5_SCAN_LANES)
            lanes = slice(j * S5_BLK_LANES + h * S5_SCAN_LANES, j * S5_BLK_LANES + (h + 1) * S5_SCAN_LANES)
            ar, ai = abr_ref[:, lanes], abi_ref[:, lanes]
            cr, ci = cr_ref[:, lanes], ci_ref[:, lanes]
            for s in range(S5_STEPS):
                step = slice(s * nb, (s + 1) * nb)
                cr, ci = (ar * cr - ai * ci + xre_ref[step, cols], ar * ci + ai * cr + xim_ref[step, cols])
                xre_ref[step, cols] = cr
                xim_ref[step, cols] = ci
            cr_ref[:, lanes] = cr
            ci_ref[:, lanes] = ci
        yx = _dot(xre_ref[...].astype(BF16), ccr_ref[j]) - _dot(xim_ref[...].astype(BF16), cci_ref[j])
        yb = _dot_split_rhs(perm_t, yx, 2) + dsk_ref[:, j * LANES:(j + 1) * LANES] * u[:, j * LANES:(j + 1) * LANES]
        gelu = 0.5 * yb * (1.0 + jnp.tanh(math.sqrt(2.0 / math.pi) * (yb + 0.044715 * (yb * yb * yb))))
        y_ref[:, :, j * LANES:(j + 1) * LANES] = gelu.astype(BF16).reshape(nb, S5_STEPS, LANES)

    @pl.when(c == pl.num_programs(1) - 1)
    def _():
        ore_ref[...] = cr_ref[...]
        oim_ref[...] = ci_ref[...]


def _s5(u3, s_re, s_im, lw):
    b, t, _ = u3.shape
    shared = s_re.shape[0] == 1
    bp = -(-b // SUBLANES) * SUBLANES
    if bp != b:
        u3 = jnp.pad(u3, ((0, bp - b), (0, 0), (0, 0)))
        if not shared:
            s_re, s_im = (jnp.pad(s, ((0, bp - b), (0, 0))) for s in (s_re, s_im))
    nb = min(bp, S5_SEQS)
    rows = nb * S5_STEPS
    r = jnp.arange(rows)
    perm = ((r % S5_STEPS) * nb + r // S5_STEPS)[None, :] == r[:, None]
    perm, perm_t = perm.astype(BF16), perm.T.astype(BF16)
    state = pl.BlockSpec((1, S5_LANES), lambda i, c: (0, 0)) if shared else pl.BlockSpec((nb, S5_LANES), lambda i, c: (i, 0))
    state_out = pl.BlockSpec((nb, S5_LANES), lambda i, c: (i, 0))
    seq = pl.BlockSpec((nb, S5_STEPS, S5_WIDTH), lambda i, c: (i, c, 0))
    consts = (lw['s5_ab_re'], lw['s5_ab_im'], lw['s5_bb_re'], lw['s5_bb_im'], lw['s5_cc_re'],
              lw['s5_cc_im'], lw['d_s5'], perm, perm_t)
    y, o_re, o_im = pl.pallas_call(
        functools.partial(_s5_body, nb=nb),
        grid=(bp // nb, t // S5_STEPS),
        in_specs=[seq, state, state] + [_full(a.shape) for a in consts],
        out_specs=[seq, state_out, state_out],
        out_shape=[jax.ShapeDtypeStruct((bp, t, S5_WIDTH), BF16),
                   jax.ShapeDtypeStruct((bp, S5_LANES), F32),
                   jax.ShapeDtypeStruct((bp, S5_LANES), F32)],
        scratch_shapes=[pltpu.VMEM((rows, S5_BLK_LANES), F32), pltpu.VMEM((rows, S5_BLK_LANES), F32),
                        pltpu.VMEM((nb, S5_LANES), F32), pltpu.VMEM((nb, S5_LANES), F32)],
        compiler_params=_params("parallel", "arbitrary"),
        name="s5",
    )(u3, s_re, s_im, *consts)
    return y[:b], o_re[:b], o_im[:b]


SB_KEYS = 256
SB_PAIRS = SB_WIDTH // LANES
SB_DEAD = -105.0


def _upper(n):
    r = lax.broadcasted_iota(jnp.int32, (n, n), 0)
    c = lax.broadcasted_iota(jnp.int32, (n, n), 1)
    return (r > c).astype(BF16)


def _attn_body(*refs, bq, n_hist, pad):
    if n_hist:
        q_ref, kn_ref, vn_ref, kh_ref, vh_ref, o_ref, kall_ref, vall_ref, acc_ref, carry_ref = refs
    else:
        q_ref, kn_ref, vn_ref, o_ref, kall_ref, vall_ref, acc_ref, carry_ref = refs
    qi = pl.program_id(1)

    @pl.when(qi == 0)
    def _():
        for all_ref, new_ref, hist_ref in ((kall_ref, kn_ref, kh_ref if n_hist else None),
                                           (vall_ref, vn_ref, vh_ref if n_hist else None)):
            if pad:
                all_ref[0:pad, :] = jnp.zeros((pad, SB_WIDTH), BF16)
            if n_hist:
                all_ref[pad:pad + n_hist, :] = hist_ref[0].astype(BF16)
            all_ref[pad + n_hist:, :] = new_ref[0]

    k_ref, v_ref = kall_ref, vall_ref
    q = q_ref[0]
    lane = lax.broadcasted_iota(jnp.int32, (bq, LANES), 1)
    left = lane < SB_HEAD_DIM
    zero_b = jnp.zeros((), BF16)
    qms = [jnp.concatenate([jnp.where(left, q[:, p * LANES:(p + 1) * LANES], zero_b),
                            jnp.where(left, zero_b, q[:, p * LANES:(p + 1) * LANES])], axis=0)
           for p in range(SB_PAIRS)]
    ri = lax.broadcasted_iota(jnp.int32, (2 * bq, SB_KEYS), 0)
    ci = lax.broadcasted_iota(jnp.int32, (2 * bq, SB_KEYS), 1)
    ri = jnp.where(ri >= bq, ri - bq, ri)
    strictly_earlier = ci - (SB_KEYS - bq) < ri
    upper = _upper(SB_KEYS)
    n_before = n_hist + (qi + 1) * bq
    end = pad + n_before
    trips = (n_before + SB_KEYS - 1) // SB_KEYS
    acc_ref[...] = jnp.zeros_like(acc_ref)
    carry_ref[...] = jnp.zeros_like(carry_ref)

    def visit(it, masked):
        rows = pl.ds(pl.multiple_of(end - (it + 1) * SB_KEYS, 16), SB_KEYS)
        zs = [_dot_nt(qms[p], k_ref[rows, p * LANES:(p + 1) * LANES]) for p in range(SB_PAIRS)]
        log_betas, log_keeps, laters = [], [], []
        for z in zs:
            sp = jnp.log(1.0 + jnp.exp(-jnp.abs(z)))
            log_beta = jnp.minimum(z, 0.0) - sp
            log_keep = log_beta - z
            if masked:
                log_keep = jnp.where(strictly_earlier, log_keep, 0.0)
            hi = log_keep.astype(BF16)
            lo = (log_keep - hi.astype(F32)).astype(BF16)
            both = _dot(jnp.concatenate([hi, lo], axis=0), upper)
            log_betas.append(log_beta)
            log_keeps.append(log_keep)
            laters.append(both[:2 * bq] + both[2 * bq:])
        for p in range(SB_PAIRS):
            carry = carry_ref[p]
            w = jnp.exp(log_betas[p] + laters[p] + carry)
            if masked:
                w = jnp.where(strictly_earlier, w, 0.0)
            pv = _dot(w.astype(BF16), v_ref[rows, p * LANES:(p + 1) * LANES])
            carry_ref[p] = carry + (laters[p][:, 0:1] + log_keeps[p][:, 0:1])
            acc_ref[:, p * LANES:(p + 1) * LANES] += jnp.where(left, pv[:bq], pv[bq:])

    def any_weight_left():
        c = carry_ref[0]
        for p in range(1, SB_PAIRS):
            c = jnp.maximum(c, carry_ref[p])
        return jnp.max(c) > SB_DEAD

    visit(0, True)

    def trip(state):
        it, _ = state
        visit(it, False)
        return it + 1, any_weight_left()

    lax.while_loop(lambda s: jnp.logical_and(s[0] < trips, s[1]), trip, (jnp.int32(1), any_weight_left()))
    o_ref[0] = acc_ref[...].astype(BF16)


BF16_ROWS = 16


def _attn(qb, kb, vb, kh, vh):
    b, t, _ = qb.shape
    bq = min(t, 128)
    n_hist = 0 if kh is None else kh.shape[1]
    assert n_hist % BF16_ROWS == 0 and bq % BF16_ROWS == 0
    short = max(-(n_hist + (qi + 1) * bq) % SB_KEYS for qi in range(t // bq))
    pad = -(-short // BF16_ROWS) * BF16_ROWS
    block = pl.BlockSpec((1, bq, SB_WIDTH), lambda i, j: (i, j, 0))
    new = pl.BlockSpec((1, t, SB_WIDTH), lambda i, j: (i, 0, 0))
    args, specs = [qb, kb, vb], [block, new, new]
    if n_hist:
        hmap = (lambda i, j: (0, 0, 0)) if kh.shape[0] == 1 else (lambda i, j: (i, 0, 0))
        args += [kh, vh]
        specs += [pl.BlockSpec((1, n_hist, SB_WIDTH), hmap)] * 2
    rows = pad + n_hist + t
    return pl.pallas_call(
        functools.partial(_attn_body, bq=bq, n_hist=n_hist, pad=pad),
        grid=(b, t // bq),
        in_specs=specs,
        out_specs=block,
        out_shape=jax.ShapeDtypeStruct((b, t, SB_WIDTH), BF16),
        scratch_shapes=[pltpu.VMEM((rows, SB_WIDTH), BF16), pltpu.VMEM((rows, SB_WIDTH), BF16),
                        pltpu.VMEM((bq, SB_WIDTH), F32), pltpu.VMEM((SB_PAIRS, 2 * bq, 1), F32)],
        compiler_params=_params("parallel", "arbitrary"),
        name="attn",
    )(*args)


def _merge_body(x_ref, ya_ref, yb_ref, oc_ref, g_ref, wg_ref, wa_ref, wglu_ref, wc_ref, wo_ref, o_ref):
    x = x_ref[...]
    hb = _rmsnorm_rows(x, g_ref[...]).astype(BF16)
    gates = _sigmoid(_dot(hb, wg_ref[...]))
    glu = _dot(yb_ref[...], wglu_ref[...])
    mix = (gates[:, :D_MODEL] * _dot(ya_ref[...], wa_ref[...])
           + gates[:, D_MODEL:2 * D_MODEL] * (glu[:, :D_MODEL] * _sigmoid(glu[:, D_MODEL:]))
           + gates[:, 2 * D_MODEL:] * _dot(oc_ref[...], wc_ref[...]))
    o_ref[...] = x + _dot(mix.astype(BF16), wo_ref[...])


def _merge(x2, ya, yb, oc, lw):
    n = x2.shape[0]
    tm = _row_tile(n)
    rows = lambda w: pl.BlockSpec((tm, w), lambda i: (i, 0))
    consts = (lw['norm_mix'], lw['wgate'], lw['w_lift_a'], lw['w_glu'], lw['w_lift_c'], lw['w_out'])
    return pl.pallas_call(
        _merge_body,
        grid=(n // tm,),
        in_specs=[rows(D_MODEL), rows(SSD_WIDTH), rows(S5_WIDTH), rows(SB_WIDTH)]
                 + [_full(a.shape) for a in consts],
        out_specs=rows(D_MODEL),
        out_shape=jax.ShapeDtypeStruct((n, D_MODEL), F32),
        compiler_params=_params("parallel"),
        name="merge",
    )(x2, ya, yb, oc, *consts)


FF_CHUNK = 1024


def _ffn_body(x_ref, g_ref, wu_ref, wd_ref, o_ref):
    x = x_ref[...]
    hb = _rmsnorm_rows(x, g_ref[...]).astype(BF16)
    acc = x
    for j in range(D_FF // FF_CHUNK):
        cols = slice(j * FF_CHUNK, (j + 1) * FF_CHUNK)
        up = jnp.maximum(_dot(hb, wu_ref[:, cols]), 0.0)
        acc = acc + _dot((up * up).astype(BF16), wd_ref[cols, :])
    o_ref[...] = acc


def _ffn(x2, lw):
    n = x2.shape[0]
    tm = _row_tile(n)
    rows = pl.BlockSpec((tm, D_MODEL), lambda i: (i, 0))
    consts = (lw['norm_ffn'], lw['w_up'], lw['w_down'])
    return pl.pallas_call(
        _ffn_body,
        grid=(n // tm,),
        in_specs=[rows] + [_full(a.shape) for a in consts],
        out_specs=rows,
        out_shape=jax.ShapeDtypeStruct((n, D_MODEL), F32),
        compiler_params=_params("parallel"),
        name="ffn",
    )(x2, *consts)


def _block_diag(blocks):
    n, r, c = blocks.shape
    eye = jnp.eye(n, dtype=blocks.dtype)
    return (eye[:, None, :, None] * blocks[:, :, None, :]).reshape(n * r, n * c)


def _layer_weights(p, l):
    w_in = p['w_in'][l]
    col = lambda off, n: w_in[:, off:off + n].astype(BF16)
    row = lambda v: v.reshape(1, -1).astype(F32)
    heads = jnp.arange(SSD_WIDTH) // SSD_HEAD_DIM
    pad_lanes = lambda v: jnp.pad(v, (0, LANES - v.shape[0])).reshape(1, LANES)
    ab_re, ab_im, bb_re, bb_im = _s5_params(p['lam_re'][l], p['lam_im'][l], p['log_step'][l],
                                            p['b_re'][l], p['b_im'][l])
    to_bb = lambda t: jnp.stack([_block_diag(blk) for blk in
                                 t.reshape(S5_NBLK, S5_BLK_GROUPS, S5_GROUP_CH, S5_STATE)]).astype(BF16)
    to_cc = lambda t: jnp.stack([_block_diag(blk) for blk in
                                 jnp.transpose(t, (0, 2, 1)).reshape(S5_NBLK, S5_BLK_GROUPS, S5_STATE, S5_GROUP_CH)]
                                ).astype(BF16)
    state_rows = jnp.arange(LANES) // SSD_STATE
    state_cols = jnp.arange(SSD_WIDTH) // (SSD_WIDTH // SSD_GROUPS)
    return {
        'norm_mix': row(p['norm_mix'][l]),
        'wz': col(OFF_Z, SSD_WIDTH), 'wxbc': col(OFF_XBC, CONV_DIM),
        'wdt': jnp.pad(col(OFF_DT, SSD_HEADS), ((0, 0), (0, LANES - SSD_HEADS))),
        'wu': col(OFF_U, S5_WIDTH), 'wq': col(OFF_Q, SB_WIDTH), 'wk': col(OFF_K, SB_WIDTH),
        'wv': col(OFF_V, SB_WIDTH), 'wgate': col(OFF_GATE, N_BRANCH * D_MODEL),
        'q_norm': row(jnp.tile(p['q_norm'][l], SB_HEADS)), 'k_norm': row(jnp.tile(p['k_norm'][l], SB_HEADS)),
        'head_mean': (_block_diag(jnp.ones((SB_HEADS, SB_HEAD_DIM, SB_HEAD_DIM), F32)) / SB_HEAD_DIM).astype(BF16),
        'conv_w': jnp.pad(p['conv_w'][l], ((0, SUBLANES - CONV_W), (0, 0))),
        'conv_b': row(p['conv_b'][l]),
        'dt_bias': pad_lanes(p['dt_bias'][l]), 'a_log': pad_lanes(p['a_log'][l]),
        'd_ssd': row(p['d_ssd'][l][heads]), 'norm_ssd': row(p['norm_ssd'][l]),
        'expand': (jnp.arange(LANES)[:, None] == heads[None, :]).astype(BF16),
        'eye': jnp.eye(LANES, dtype=BF16),
        'state_mask': (state_rows[:, None] == state_cols[None, :]).astype(F32),
        's5_ab_re': ab_re[::S5_GROUP_CH].reshape(1, S5_LANES), 's5_ab_im': ab_im[::S5_GROUP_CH].reshape(1, S5_LANES),
        's5_bb_re': to_bb(bb_re), 's5_bb_im': to_bb(bb_im),
        's5_cc_re': to_cc(p['c_re'][l]), 's5_cc_im': to_cc(p['c_im'][l]),
        'd_s5': row(p['d_s5'][l]),
        'w_glu': p['w_glu'][l].astype(BF16), 'w_lift_a': p['w_lift_a'][l].astype(BF16),
        'w_lift_c': p['w_lift_c'][l].astype(BF16), 'w_out': p['w_out'][l].astype(BF16),
        'norm_ffn': row(p['norm_ffn'][l]), 'w_up': p['w_up'][l].astype(BF16), 'w_down': p['w_down'][l].astype(BF16),
    }


def _ssd_state_in(s):
    b = s.shape[0]
    t = jnp.transpose(s.astype(F32), (0, 1, 4, 2, 3)).reshape(b, SSD_GROUPS, SSD_STATE, SSD_HPG * SSD_HEAD_DIM)
    eye = jnp.eye(SSD_GROUPS, dtype=F32)
    return (t[:, :, :, None, :] * eye[None, :, None, :, None]).reshape(b, LANES, SSD_WIDTH)


def _ssd_state_out(s):
    b = s.shape[0]
    t = s.reshape(b, SSD_GROUPS, SSD_STATE, SSD_GROUPS, SSD_HPG, SSD_HEAD_DIM)
    t = jnp.stack([t[:, g, :, g] for g in range(SSD_GROUPS)], axis=1)
    return jnp.transpose(t, (0, 1, 3, 4, 2))


def _trunk_layer(x, layer, k_buf, v_buf, k_hist, v_hist, conv_hist, ssd_s0, s5_re0, s5_im0, lw):
    b, t, _ = x.shape
    x2 = x.reshape(b * t, D_MODEL)
    z, xbc, dt, u, k_buf, v_buf, qb, kb, vb = _proj(x2, layer, k_buf, v_buf, lw)
    seq = lambda a: a.reshape(b, t, a.shape[-1])
    hist8 = jnp.pad(conv_hist.astype(F32), ((0, 0), (SUBLANES - (CONV_W - 1), 0), (0, 0)))
    y_a, ssd_new = _ssd(seq(z), seq(xbc), seq(dt), hist8, _ssd_state_in(ssd_s0), lw)
    y_b, s5_re, s5_im = _s5(seq(u), s5_re0.reshape(-1, S5_LANES).astype(F32),
                            s5_im0.reshape(-1, S5_LANES).astype(F32), lw)
    if k_hist is None:
        kh = vh = None
    else:
        kh = k_hist.reshape(k_hist.shape[0], -1, SB_WIDTH).astype(F32)
        vh = v_hist.reshape(v_hist.shape[0], -1, SB_WIDTH).astype(F32)
    o_c = _attn(seq(qb), seq(kb), seq(vb), kh, vh)
    x2 = _merge(x2, y_a.reshape(b * t, SSD_WIDTH), y_b.reshape(b * t, S5_WIDTH), o_c.reshape(b * t, SB_WIDTH), lw)
    x2 = _ffn(x2, lw)
    conv_rows = jnp.concatenate([jnp.broadcast_to(conv_hist.astype(F32), (b, CONV_W - 1, CONV_DIM)),
                                 seq(xbc)[:, -(CONV_W - 1):]], axis=1)[:, -(CONV_W - 1):]
    return (x2.reshape(b, t, D_MODEL), k_buf, v_buf, conv_rows, _ssd_state_out(ssd_new),
            s5_re.reshape(b, S5_GROUPS, S5_STATE), s5_im.reshape(b, S5_GROUPS, S5_STATE))


def kernel(x_prompt, x_sample, cache_k, cache_v, state_conv, state_ssd, state_s5_re, state_s5_im, meta_tokens, norm_mix, w_in, conv_w, conv_b, dt_bias, a_log, d_ssd, norm_ssd, lam_re, lam_im, log_step, b_re, b_im, c_re, c_im, d_s5, w_glu, q_norm, k_norm, w_lift_a, w_lift_c, w_out, norm_ffn, w_up, w_down):
    p = dict(norm_mix=norm_mix, w_in=w_in, conv_w=conv_w, conv_b=conv_b, dt_bias=dt_bias, a_log=a_log,
             d_ssd=d_ssd, norm_ssd=norm_ssd, lam_re=lam_re, lam_im=lam_im, log_step=log_step, b_re=b_re,
             b_im=b_im, c_re=c_re, c_im=c_im, d_s5=d_s5, w_glu=w_glu, q_norm=q_norm, k_norm=k_norm,
             w_lift_a=w_lift_a, w_lift_c=w_lift_c, w_out=w_out, norm_ffn=norm_ffn, w_up=w_up, w_down=w_down)
    depth = w_in.shape[0]
    bp, tp, _ = x_prompt.shape
    bs, ts, _ = x_sample.shape
    xm = meta_tokens.astype(x_prompt.dtype)[None]
    xp, xs = x_prompt, x_sample
    zeros = lambda *s: jnp.zeros(s, F32)
    kv_m = [zeros(depth, N_META, SB_WIDTH), zeros(depth, N_META, SB_WIDTH)]
    kv_p = [zeros(depth, bp * tp, SB_WIDTH), zeros(depth, bp * tp, SB_WIDTH)]
    kv_s = [zeros(depth, bs * ts, SB_WIDTH), zeros(depth, bs * ts, SB_WIDTH)]
    outs_p, outs_s = [], []
    for l in range(depth):
        lw = _layer_weights(p, l)
        xm, *kv_m, conv_m, ssd_m, s5re_m, s5im_m = _trunk_layer(
            xm, l, *kv_m, None, None, zeros(1, CONV_W - 1, CONV_DIM),
            zeros(1, SSD_GROUPS, SSD_HPG, SSD_HEAD_DIM, SSD_STATE),
            zeros(1, S5_GROUPS, S5_STATE), zeros(1, S5_GROUPS, S5_STATE), lw)
        xp, *kv_p, conv_p, ssd_p, s5re_p, s5im_p = _trunk_layer(
            xp, l, *kv_p, kv_m[0][l][None], kv_m[1][l][None], conv_m, ssd_m, s5re_m, s5im_m, lw)
        outs_p.append([conv_p, ssd_p, s5re_p, s5im_p])
        xs, *kv_s, conv_s, ssd_s, s5re_s, s5im_s = _trunk_layer(
            xs, l, *kv_s, cache_k[l], cache_v[l], state_conv[l], state_ssd[l], state_s5_re[l], state_s5_im[l], lw)
        outs_s.append([conv_s, ssd_s, s5re_s, s5im_s])
    stk = lambda outs, i: jnp.stack([o[i] for o in outs], axis=0)

    def prompt_rows(meta, new):
        meta = jnp.broadcast_to(meta[:, None], (depth, bp, N_META, SB_WIDTH))
        rows = jnp.concatenate([meta, new.reshape(depth, bp, tp, SB_WIDTH)], axis=2)
        return rows.reshape(depth, bp, N_META + tp, SB_HEADS, SB_HEAD_DIM)

    sample_rows = lambda new: new.reshape(depth, bs, ts, SB_HEADS, SB_HEAD_DIM)
    return (xp, xs,
            prompt_rows(kv_m[0], kv_p[0]), prompt_rows(kv_m[1], kv_p[1]),
            stk(outs_p, 0), stk(outs_p, 1), stk(outs_p, 2), stk(outs_p, 3),
            sample_rows(kv_s[0]), sample_rows(kv_s[1]),
            stk(outs_s, 0), stk(outs_s, 1), stk(outs_s, 2), stk(outs_s, 3))
```

```python
import functools
import math

import jax
import jax.numpy as jnp
from jax import lax
from jax.experimental import pallas as pl
from jax.experimental.pallas import tpu as pltpu

F32 = jnp.float32
BF16 = jnp.bfloat16

D_MODEL = 1024
N_META = 16
RMS_EPS = 1e-6
SSD_HEADS = 16
SSD_HEAD_DIM = 64
SSD_GROUPS = 2
SSD_HPG = SSD_HEADS // SSD_GROUPS
SSD_STATE = 64
SSD_WIDTH = SSD_HEADS * SSD_HEAD_DIM
CONV_W = 4
N_BC = SSD_GROUPS * SSD_STATE
CONV_DIM = SSD_WIDTH + 2 * N_BC
S5_WIDTH = D_MODEL // 2
S5_GROUP_CH = 16
S5_GROUPS = S5_WIDTH // S5_GROUP_CH
S5_STATE = 64
S5_LANES = S5_GROUPS * S5_STATE
SB_HEADS = 8
SB_HEAD_DIM = 64
SB_WIDTH = SB_HEADS * SB_HEAD_DIM
SB_SCALE = 1.0 / math.sqrt(SB_HEAD_DIM)
N_BRANCH = 3
D_FF = 4 * D_MODEL
OFF_Z = 0
OFF_XBC = OFF_Z + SSD_WIDTH
OFF_DT = OFF_XBC + CONV_DIM
OFF_U = OFF_DT + SSD_HEADS
OFF_Q = OFF_U + S5_WIDTH
OFF_K = OFF_Q + SB_WIDTH
OFF_V = OFF_K + SB_WIDTH
OFF_GATE = OFF_V + SB_WIDTH

LANES = 128
SUBLANES = 8
VMEM_LIMIT = 56 * 1024 * 1024
S5_BLK_GROUPS = LANES // S5_GROUP_CH
S5_BLK_LANES = S5_BLK_GROUPS * S5_STATE
S5_NBLK = S5_GROUPS // S5_BLK_GROUPS


def _dot(a, b):
    return jnp.dot(a, b, preferred_element_type=F32)


def _dot_nt(a, b):
    return lax.dot_general(a, b, (((1,), (1,)), ((), ())), preferred_element_type=F32)


def _split(a, terms):
    out = []
    r = a
    for _ in range(terms):
        p = r.astype(BF16)
        out.append(p)
        r = r - p.astype(F32)
    return out


def _dot_split_lhs(a, b_bf16, terms=3):
    return sum(_dot(p, b_bf16) for p in _split(a, terms))


def _dot_split_rhs(a_bf16, b, terms=3):
    return sum(_dot(a_bf16, p) for p in _split(b, terms))


def _sigmoid(x):
    return 1.0 / (1.0 + jnp.exp(-x))


def _softplus(x):
    return jnp.maximum(x, 0.0) + jnp.log(1.0 + jnp.exp(-jnp.abs(x)))


def _rmsnorm_rows(x, g):
    return x * lax.rsqrt(jnp.mean(x * x, axis=-1, keepdims=True) + RMS_EPS) * g


def _params(*sem):
    return pltpu.CompilerParams(dimension_semantics=sem, vmem_limit_bytes=VMEM_LIMIT)


def _full(shape):
    n = len(shape)
    return pl.BlockSpec(shape, lambda *_: (0,) * n)


def _row_tile(n):
    for t in (256, 128, 64, 32, 16, 8):
        if n % t == 0:
            return t
    raise ValueError(f"token count {n} is not a multiple of {SUBLANES}")


def _proj_body(x_ref, g_ref, wz_ref, wxbc_ref, wdt_ref, wu_ref, wq_ref, wk_ref, wv_ref,
               qn_ref, kn_ref, hm_ref, kbuf_ref, vbuf_ref,
               z_ref, xbc_ref, dt_ref, u_ref, k_ref, v_ref, qb_ref, kb_ref, vb_ref):
    del kbuf_ref, vbuf_ref
    hb = _rmsnorm_rows(x_ref[...], g_ref[...]).astype(BF16)
    z_ref[...] = _dot(hb, wz_ref[...])
    xbc_ref[...] = _dot(hb, wxbc_ref[...])
    dt_ref[...] = _dot(hb, wdt_ref[...])
    u_ref[...] = _dot(hb, wu_ref[...])
    hm = hm_ref[...]

    def head_norm(t, w):
        return t * lax.rsqrt(_dot_split_lhs(t * t, hm, 2) + RMS_EPS) * w

    q = head_norm(_dot(hb, wq_ref[...]), qn_ref[...])
    k = head_norm(_dot(hb, wk_ref[...]), kn_ref[...])
    v = _dot(hb, wv_ref[...])
    k_ref[0] = k
    v_ref[0] = v
    qb_ref[...] = (q * SB_SCALE).astype(BF16)
    kb_ref[...] = k.astype(BF16)
    vb_ref[...] = v.astype(BF16)


def _proj(x2, layer, k_buf, v_buf, lw):
    n = x2.shape[0]
    tm = _row_tile(n)
    rows = lambda w: pl.BlockSpec((tm, w), lambda i: (i, 0))
    stacked = pl.BlockSpec((1, tm, SB_WIDTH), lambda i: (layer, i, 0))
    ins = (lw['norm_mix'], lw['wz'], lw['wxbc'], lw['wdt'], lw['wu'], lw['wq'], lw['wk'], lw['wv'],
           lw['q_norm'], lw['k_norm'], lw['head_mean'])
    sds = jax.ShapeDtypeStruct
    return pl.pallas_call(
        _proj_body,
        grid=(n // tm,),
        in_specs=[rows(D_MODEL)] + [_full(a.shape) for a in ins] + [pl.BlockSpec(memory_space=pl.ANY)] * 2,
        out_specs=[rows(SSD_WIDTH), rows(CONV_DIM), rows(LANES), rows(S5_WIDTH),
                   stacked, stacked, rows(SB_WIDTH), rows(SB_WIDTH), rows(SB_WIDTH)],
        out_shape=[sds((n, SSD_WIDTH), F32), sds((n, CONV_DIM), F32), sds((n, LANES), F32),
                   sds((n, S5_WIDTH), F32), sds(k_buf.shape, F32), sds(v_buf.shape, F32),
                   sds((n, SB_WIDTH), BF16), sds((n, SB_WIDTH), BF16), sds((n, SB_WIDTH), BF16)],
        input_output_aliases={1 + len(ins): 4, 2 + len(ins): 5},
        compiler_params=_params("parallel"),
        name="proj",
    )(x2, *ins, k_buf, v_buf)


def _ssd_body(z_ref, xbc_ref, dt_ref, hist_ref, s0_ref, cw_ref, cb_ref, dtb_ref, alog_ref,
              dskip_ref, nw_ref, eye_ref, smask_ref, tri_ref,
              y_ref, sout_ref, xp_ref, st_ref, yacc_ref, *, lc):
    c = pl.program_id(1)

    @pl.when(c == 0)
    def _():
        xp_ref[0:SUBLANES, :] = hist_ref[0]
        st_ref[...] = s0_ref[0]

    xp_ref[SUBLANES:SUBLANES + lc, :] = xbc_ref[0]
    conv = cb_ref[...]
    for k in range(CONV_W):
        lo = SUBLANES - (CONV_W - 1) + k
        conv = conv + xp_ref[lo:lo + lc, :] * cw_ref[k:k + 1, :]
    xp_ref[0:SUBLANES, :] = xp_ref[lc:lc + SUBLANES, :]
    act = conv * _sigmoid(conv)
    xs = act[:, :SSD_WIDTH]
    bm = act[:, SSD_WIDTH:SSD_WIDTH + N_BC]
    cm = act[:, SSD_WIDTH + N_BC:]

    eye = eye_ref[...]
    dt = _softplus(dt_ref[0] + dtb_ref[...])
    a = -jnp.exp(alog_ref[...])
    ri = lax.broadcasted_iota(jnp.int32, (lc, lc), 0)
    ci = lax.broadcasted_iota(jnp.int32, (lc, lc), 1)
    causal = ri >= ci
    acum = _dot_split_rhs(tri_ref[...], dt * a)
    if lc == LANES:
        transpose = lambda t: t.T
    else:
        transpose = lambda t: sum(_dot_nt(eye, p) for p in _split(t, 3))
    acum_t = transpose(acum)
    log_dt_t = jnp.log(transpose(dt))
    a_last = acum[lc - 1:lc, :]
    shifted_t = acum_t - log_dt_t
    to_end_t = jnp.exp(acum_t[:, lc - 1:lc] - shifted_t)
    dec = jnp.exp(a_last)

    xs_b = xs.astype(BF16)
    bm_b = bm.astype(BF16)
    cm_b = cm.astype(BF16)
    bm_t = bm.T if lc == LANES else _dot_nt(eye, bm_b)
    lane = lax.broadcasted_iota(jnp.int32, (lc, LANES), 1)
    left = lane < SSD_STATE
    left_sq = lax.broadcasted_iota(jnp.int32, (LANES, LANES), 1) < SSD_HEAD_DIM
    zero_b = jnp.zeros((), BF16)
    cbs = [_dot_nt(jnp.where(left if g == 0 else ~left, cm_b, zero_b), bm_b) for g in range(SSD_GROUPS)]
    st = st_ref[...]
    st_b = st.astype(BF16)
    smask = smask_ref[...]

    for hp in range(SSD_HEADS // 2):
        pair = slice(hp * LANES, (hp + 1) * LANES)
        rhs = jnp.concatenate([xs_b[:, pair], st_b[:, pair]], axis=0)
        outs, news = [], []
        for h in (2 * hp, 2 * hp + 1):
            col = jnp.broadcast_to(acum[:, h:h + 1], (lc, LANES))
            seg = col[:, :lc] - shifted_t[h:h + 1, :]
            m = jnp.where(causal, cbs[h // SSD_HPG] * jnp.exp(seg), 0.0)
            read = cm * jnp.exp(col)
            outs.append(_dot(jnp.concatenate([m.astype(BF16), read.astype(BF16)], axis=1), rhs))
            news.append(_dot((bm_t * to_end_t[h:h + 1, :]).astype(BF16), xs_b[:, pair]))
        yacc_ref[:, pair] = jnp.where(left, outs[0], outs[1])
        decay = jnp.where(left_sq, dec[:, 2 * hp:2 * hp + 1], dec[:, 2 * hp + 1:2 * hp + 2])
        st_ref[:, pair] = st[:, pair] * decay + smask[:, pair] * jnp.where(left_sq, news[0], news[1])

    y = (yacc_ref[...] + dskip_ref[...] * xs)
    zz = z_ref[0]
    y = y * (zz * _sigmoid(zz))
    gw = SSD_WIDTH // SSD_GROUPS
    parts = []
    for g in range(SSD_GROUPS):
        yg = y[:, g * gw:(g + 1) * gw]
        parts.append(yg * lax.rsqrt(jnp.mean(yg * yg, axis=-1, keepdims=True) + RMS_EPS))
    y_ref[0] = (jnp.concatenate(parts, axis=1) * nw_ref[...]).astype(BF16)

    @pl.when(c == pl.num_programs(1) - 1)
    def _():
        sout_ref[0] = st_ref[...]


def _ssd(z3, xbc3, dt3, hist, s0, lw):
    b, t, _ = z3.shape
    lc = min(t, 128)
    shared = hist.shape[0] == 1
    per_b = (lambda i, c: (0, 0, 0)) if shared else (lambda i, c: (i, 0, 0))
    seq = lambda w: pl.BlockSpec((1, lc, w), lambda i, c: (i, c, 0))
    consts = (lw['conv_w'], lw['conv_b'], lw['dt_bias'], lw['a_log'], lw['d_ssd'], lw['norm_ssd'],
              lw['eye'], lw['state_mask'], jnp.tril(jnp.ones((lc, lc), BF16)))
    return pl.pallas_call(
        functools.partial(_ssd_body, lc=lc),
        grid=(b, t // lc),
        in_specs=[seq(SSD_WIDTH), seq(CONV_DIM), seq(LANES),
                  pl.BlockSpec((1, SUBLANES, CONV_DIM), per_b),
                  pl.BlockSpec((1, LANES, SSD_WIDTH), per_b)] + [_full(a.shape) for a in consts],
        out_specs=[seq(SSD_WIDTH), pl.BlockSpec((1, LANES, SSD_WIDTH), lambda i, c: (i, 0, 0))],
        out_shape=[jax.ShapeDtypeStruct((b, t, SSD_WIDTH), BF16),
                   jax.ShapeDtypeStruct((b, LANES, SSD_WIDTH), F32)],
        scratch_shapes=[pltpu.VMEM((lc + SUBLANES, CONV_DIM), F32),
                        pltpu.VMEM((LANES, SSD_WIDTH), F32),
                        pltpu.VMEM((lc, SSD_WIDTH), F32)],
        compiler_params=_params("parallel", "arbitrary"),
        name="ssd",
    )(z3, xbc3, dt3, hist, s0, *consts)


def _s5_param_body(lr_ref, li_ref, step_ref, br_ref, bi_ref, abr_ref, abi_ref, bbr_ref, bbi_ref):
    lr, li, step = lr_ref[...], li_ref[...], jnp.exp(step_ref[...])
    mag = jnp.exp(lr * step)
    ab_re = mag * jnp.cos(li * step)
    ab_im = mag * jnp.sin(li * step)
    den = lr * lr + li * li
    nr = ab_re - 1.0
    f_re = (nr * lr + ab_im * li) / den
    f_im = (ab_im * lr - nr * li) / den
    br, bi = br_ref[...], bi_ref[...]
    abr_ref[...] = ab_re
    abi_ref[...] = ab_im
    bbr_ref[...] = f_re * br - f_im * bi
    bbi_ref[...] = f_re * bi + f_im * br


def _s5_params(lam_re, lam_im, log_step, b_re, b_im):
    rep = lambda t: jnp.repeat(t, S5_GROUP_CH, axis=0)
    shape = (S5_WIDTH, S5_STATE)
    step = jnp.broadcast_to(rep(log_step[:, None]), shape)
    to_rows = lambda t: jnp.transpose(t, (0, 2, 1)).reshape(shape)
    outs = pl.pallas_call(
        _s5_param_body,
        out_shape=[jax.ShapeDtypeStruct(shape, F32)] * 4,
        name="s5_params",
    )(rep(lam_re), rep(lam_im), step, to_rows(b_re), to_rows(b_im))
    return outs


S5_STEPS = 16
S5_SEQS = 16
S5_SCAN_LANES = 256


def _s5_body(u_ref, sre_ref, sim_ref, abr_ref, abi_ref, bbr_ref, bbi_ref, ccr_ref, cci_ref, dsk_ref,
             perm_ref, perm_t_ref, y_ref, ore_ref, oim_ref, xre_ref, xim_ref, cr_ref, ci_ref, *, nb):
    c = pl.program_id(1)
    rows = nb * S5_STEPS

    @pl.when(c == 0)
    def _():
        cr_ref[...] = jnp.broadcast_to(sre_ref[...], cr_ref.shape)
        ci_ref[...] = jnp.broadcast_to(sim_ref[...], ci_ref.shape)

    u = u_ref[...].reshape(rows, S5_WIDTH)
    u_tm = _dot(perm_ref[...], u.astype(BF16)).astype(BF16)
    perm_t = perm_t_ref[...]
    for j in range(S5_NBLK):
        ub = u_tm[:, j * LANES:(j + 1) * LANES]
        xre, xim = xre_ref.at[j], xim_ref.at[j]
        xre[...] = _dot(ub, bbr_ref[j])
        xim[...] = _dot(ub, bbi_ref[j])
        for h in range(S5_BLK_LANES // S5_SCAN_LANES):
            cols = slice(h * S5_SCAN_LANES, (h + 1) * S5_SCAN_LANES)
            lanes = slice(j * S5_BLK_LANES + h * S5_SCAN_LANES, j * S5_BLK_LANES + (h + 1) * S5_SCAN_LANES)
            ar, ai = abr_ref[:, lanes], abi_ref[:, lanes]
            cr, ci = cr_ref[:, lanes], ci_ref[:, lanes]
            for s in range(S5_STEPS):
                step = slice(s * nb, (s + 1) * nb)
                cr, ci = (ar * cr - ai * ci + xre[step, cols], ar * ci + ai * cr + xim[step, cols])
                xre[step, cols] = cr
                xim[step, cols] = ci
            cr_ref[:, lanes] = cr
            ci_ref[:, lanes] = ci
        yx = _dot(xre[...].astype(BF16), ccr_ref[j]) - _dot(xim[...].astype(BF16), cci_ref[j])
        yb = _dot_split_rhs(perm_t, yx, 2) + dsk_ref[:, j * LANES:(j + 1) * LANES] * u[:, j * LANES:(j + 1) * LANES]
        gelu = 0.5 * yb * (1.0 + jnp.tanh(math.sqrt(2.0 / math.pi) * (yb + 0.044715 * (yb * yb * yb))))
        y_ref[:, :, j * LANES:(j + 1) * LANES] = gelu.astype(BF16).reshape(nb, S5_STEPS, LANES)

    @pl.when(c == pl.num_programs(1) - 1)
    def _():
        ore_ref[...] = cr_ref[...]
        oim_ref[...] = ci_ref[...]


def _s5(u3, s_re, s_im, lw):
    b, t, _ = u3.shape
    shared = s_re.shape[0] == 1
    bp = -(-b // SUBLANES) * SUBLANES
    if bp != b:
        u3 = jnp.pad(u3, ((0, bp - b), (0, 0), (0, 0)))
        if not shared:
            s_re, s_im = (jnp.pad(s, ((0, bp - b), (0, 0))) for s in (s_re, s_im))
    nb = min(bp, S5_SEQS)
    rows = nb * S5_STEPS
    r = jnp.arange(rows)
    perm = ((r % S5_STEPS) * nb + r // S5_STEPS)[None, :] == r[:, None]
    perm, perm_t = perm.astype(BF16), perm.T.astype(BF16)
    state = pl.BlockSpec((1, S5_LANES), lambda i, c: (0, 0)) if shared else pl.BlockSpec((nb, S5_LANES), lambda i, c: (i, 0))
    state_out = pl.BlockSpec((nb, S5_LANES), lambda i, c: (i, 0))
    seq = pl.BlockSpec((nb, S5_STEPS, S5_WIDTH), lambda i, c: (i, c, 0))
    consts = (lw['s5_ab_re'], lw['s5_ab_im'], lw['s5_bb_re'], lw['s5_bb_im'], lw['s5_cc_re'],
              lw['s5_cc_im'], lw['d_s5'], perm, perm_t)
    y, o_re, o_im = pl.pallas_call(
        functools.partial(_s5_body, nb=nb),
        grid=(bp // nb, t // S5_STEPS),
        in_specs=[seq, state, state] + [_full(a.shape) for a in consts],
        out_specs=[seq, state_out, state_out],
        out_shape=[jax.ShapeDtypeStruct((bp, t, S5_WIDTH), BF16),
                   jax.ShapeDtypeStruct((bp, S5_LANES), F32),
                   jax.ShapeDtypeStruct((bp, S5_LANES), F32)],
        scratch_shapes=[pltpu.VMEM((S5_NBLK, rows, S5_BLK_LANES), F32), pltpu.VMEM((S5_NBLK, rows, S5_BLK_LANES), F32),
                        pltpu.VMEM((nb, S5_LANES), F32), pltpu.VMEM((nb, S5_LANES), F32)],
        compiler_params=_params("parallel", "arbitrary"),
        name="s5",
    )(u3, s_re, s_im, *consts)
    return y[:b], o_re[:b], o_im[:b]


SB_KEYS = 256
SB_PAIRS = SB_WIDTH // LANES
SB_DEAD = -105.0


def _attn_body(*refs, bq, n_hist, pad):
    if n_hist:
        q_ref, upper_ref, kn_ref, vn_ref, kh_ref, vh_ref, o_ref, kall_ref, vall_ref, acc_ref, carry_ref = refs
    else:
        q_ref, upper_ref, kn_ref, vn_ref, o_ref, kall_ref, vall_ref, acc_ref, carry_ref = refs
    qi = pl.program_id(1)

    @pl.when(qi == 0)
    def _():
        for all_ref, new_ref, hist_ref in ((kall_ref, kn_ref, kh_ref if n_hist else None),
                                           (vall_ref, vn_ref, vh_ref if n_hist else None)):
            if pad:
                all_ref[0:pad, :] = jnp.zeros((pad, SB_WIDTH), BF16)
            if n_hist:
                all_ref[pad:pad + n_hist, :] = hist_ref[0].astype(BF16)
            all_ref[pad + n_hist:, :] = new_ref[0]

    k_ref, v_ref = kall_ref, vall_ref
    q = q_ref[0]
    lane = lax.broadcasted_iota(jnp.int32, (bq, LANES), 1)
    left = lane < SB_HEAD_DIM
    zero_b = jnp.zeros((), BF16)
    qms = [jnp.concatenate([jnp.where(left, q[:, p * LANES:(p + 1) * LANES], zero_b),
                            jnp.where(left, zero_b, q[:, p * LANES:(p + 1) * LANES])], axis=0)
           for p in range(SB_PAIRS)]
    ri = lax.broadcasted_iota(jnp.int32, (2 * bq, SB_KEYS), 0)
    ci = lax.broadcasted_iota(jnp.int32, (2 * bq, SB_KEYS), 1)
    ri = jnp.where(ri >= bq, ri - bq, ri)
    strictly_earlier = ci - (SB_KEYS - bq) < ri
    upper = upper_ref[...]
    n_before = n_hist + (qi + 1) * bq
    end = pad + n_before
    trips = (n_before + SB_KEYS - 1) // SB_KEYS
    acc_ref[...] = jnp.zeros_like(acc_ref)
    carry_ref[...] = jnp.zeros_like(carry_ref)

    def visit(it, masked):
        rows = pl.ds(pl.multiple_of(end - (it + 1) * SB_KEYS, 16), SB_KEYS)
        zs = [_dot_nt(qms[p], k_ref[rows, p * LANES:(p + 1) * LANES]) for p in range(SB_PAIRS)]
        log_betas, log_keeps, laters = [], [], []
        for z in zs:
            sp = jnp.log(1.0 + jnp.exp(-jnp.abs(z)))
            log_beta = jnp.minimum(z, 0.0) - sp
            log_keep = log_beta - z
            if masked:
                log_keep = jnp.where(strictly_earlier, log_keep, 0.0)
            hi = log_keep.astype(BF16)
            lo = (log_keep - hi.astype(F32)).astype(BF16)
            both = _dot(jnp.concatenate([hi, lo], axis=0), upper)
            log_betas.append(log_beta)
            log_keeps.append(log_keep)
            laters.append(both[:2 * bq] + both[2 * bq:])
        for p in range(SB_PAIRS):
            carry = carry_ref[p]
            w = jnp.exp(log_betas[p] + laters[p] + carry)
            if masked:
                w = jnp.where(strictly_earlier, w, 0.0)
            pv = _dot(w.astype(BF16), v_ref[rows, p * LANES:(p + 1) * LANES])
            carry_ref[p] = carry + (laters[p][:, 0:1] + log_keeps[p][:, 0:1])
            acc_ref[:, p * LANES:(p + 1) * LANES] += jnp.where(left, pv[:bq], pv[bq:])

    def any_weight_left():
        c = carry_ref[0]
        for p in range(1, SB_PAIRS):
            c = jnp.maximum(c, carry_ref[p])
        return jnp.max(c) > SB_DEAD

    visit(0, True)

    def trip(state):
        it, _ = state
        visit(it, False)
        return it + 1, any_weight_left()

    lax.while_loop(lambda s: jnp.logical_and(s[0] < trips, s[1]), trip, (jnp.int32(1), any_weight_left()))
    o_ref[0] = acc_ref[...].astype(BF16)


BF16_ROWS = 16


def _attn(qb, kb, vb, kh, vh):
    b, t, _ = qb.shape
    bq = min(t, 128)
    n_hist = 0 if kh is None else kh.shape[1]
    assert n_hist % BF16_ROWS == 0 and bq % BF16_ROWS == 0
    short = max(-(n_hist + (qi + 1) * bq) % SB_KEYS for qi in range(t // bq))
    pad = -(-short // BF16_ROWS) * BF16_ROWS
    block = pl.BlockSpec((1, bq, SB_WIDTH), lambda i, j: (i, j, 0))
    new = pl.BlockSpec((1, t, SB_WIDTH), lambda i, j: (i, 0, 0))
    upper = jnp.tril(jnp.ones((SB_KEYS, SB_KEYS), BF16), -1)
    args, specs = [qb, upper, kb, vb], [block, _full(upper.shape), new, new]
    if n_hist:
        hmap = (lambda i, j: (0, 0, 0)) if kh.shape[0] == 1 else (lambda i, j: (i, 0, 0))
        args += [kh, vh]
        specs += [pl.BlockSpec((1, n_hist, SB_WIDTH), hmap)] * 2
    rows = pad + n_hist + t
    return pl.pallas_call(
        functools.partial(_attn_body, bq=bq, n_hist=n_hist, pad=pad),
        grid=(b, t // bq),
        in_specs=specs,
        out_specs=block,
        out_shape=jax.ShapeDtypeStruct((b, t, SB_WIDTH), BF16),
        scratch_shapes=[pltpu.VMEM((rows, SB_WIDTH), BF16), pltpu.VMEM((rows, SB_WIDTH), BF16),
                        pltpu.VMEM((bq, SB_WIDTH), F32), pltpu.VMEM((SB_PAIRS, 2 * bq, 1), F32)],
        compiler_params=_params("parallel", "arbitrary"),
        name="attn",
    )(*args)


def _kv_rows_body(km_ref, vm_ref, kn_ref, vn_ref, ko_ref, vo_ref, *, t):
    n_meta = km_ref.shape[1]
    total = n_meta + t
    for meta_ref, new_ref, out_ref in ((km_ref, kn_ref, ko_ref), (vm_ref, vn_ref, vo_ref)):
        for c in range(-(-total // LANES)):
            lo, hi = c * LANES, min((c + 1) * LANES, total)
            parts = []
            if lo < n_meta:
                parts.append(meta_ref[0, lo:min(hi, n_meta), :])
            if hi > n_meta:
                parts.append(new_ref[0, max(lo, n_meta) - n_meta:hi - n_meta, :])
            if hi - lo < LANES:
                parts.append(jnp.zeros((LANES - (hi - lo), SB_WIDTH), F32))
            blk = parts[0] if len(parts) == 1 else jnp.concatenate(parts, axis=0)
            blk_t = blk.T.reshape(SB_HEADS, SB_HEAD_DIM, LANES)
            out_ref[0, 0, :, :, lo:hi] = blk_t[:, :, :hi - lo]


def _kv_rows(k_meta, v_meta, k_new, v_new, b):
    depth, n, _ = k_new.shape
    t = n // b
    n_meta = k_meta.shape[1]
    meta = pl.BlockSpec((1, n_meta, SB_WIDTH), lambda i: (i // b, 0, 0))
    new = pl.BlockSpec((1, t, SB_WIDTH), lambda i: (i // b, i % b, 0))
    out = pl.BlockSpec((1, 1, SB_HEADS, SB_HEAD_DIM, n_meta + t), lambda i: (i // b, i % b, 0, 0, 0))
    shape = jax.ShapeDtypeStruct((depth, b, SB_HEADS, SB_HEAD_DIM, n_meta + t), F32)
    k_out, v_out = pl.pallas_call(
        functools.partial(_kv_rows_body, t=t),
        grid=(depth * b,),
        in_specs=[meta, meta, new, new],
        out_specs=[out, out],
        out_shape=[shape, shape],
        compiler_params=_params("parallel"),
        name="kv_rows",
    )(k_meta, v_meta, k_new, v_new)
    return jnp.transpose(k_out, (0, 1, 4, 2, 3)), jnp.transpose(v_out, (0, 1, 4, 2, 3))


def _merge_body(x_ref, ya_ref, yb_ref, oc_ref, g_ref, wg_ref, wa_ref, wglu_ref, wc_ref, wo_ref, o_ref):
    x = x_ref[...]
    hb = _rmsnorm_rows(x, g_ref[...]).astype(BF16)
    gates = _sigmoid(_dot(hb, wg_ref[...]))
    glu = _dot(yb_ref[...], wglu_ref[...])
    mix = (gates[:, :D_MODEL] * _dot(ya_ref[...], wa_ref[...])
           + gates[:, D_MODEL:2 * D_MODEL] * (glu[:, :D_MODEL] * _sigmoid(glu[:, D_MODEL:]))
           + gates[:, 2 * D_MODEL:] * _dot(oc_ref[...], wc_ref[...]))
    o_ref[...] = x + _dot(mix.astype(BF16), wo_ref[...])


def _merge(x2, ya, yb, oc, lw):
    n = x2.shape[0]
    tm = _row_tile(n)
    rows = lambda w: pl.BlockSpec((tm, w), lambda i: (i, 0))
    consts = (lw['norm_mix'], lw['wgate'], lw['w_lift_a'], lw['w_glu'], lw['w_lift_c'], lw['w_out'])
    return pl.pallas_call(
        _merge_body,
        grid=(n // tm,),
        in_specs=[rows(D_MODEL), rows(SSD_WIDTH), rows(S5_WIDTH), rows(SB_WIDTH)]
                 + [_full(a.shape) for a in consts],
        out_specs=rows(D_MODEL),
        out_shape=jax.ShapeDtypeStruct((n, D_MODEL), F32),
        compiler_params=_params("parallel"),
        name="merge",
    )(x2, ya, yb, oc, *consts)


FF_CHUNK = 1024


def _ffn_body(x_ref, g_ref, wu_ref, wd_ref, o_ref):
    x = x_ref[...]
    hb = _rmsnorm_rows(x, g_ref[...]).astype(BF16)
    acc = x
    for j in range(D_FF // FF_CHUNK):
        cols = slice(j * FF_CHUNK, (j + 1) * FF_CHUNK)
        up = jnp.maximum(_dot(hb, wu_ref[:, cols]), 0.0)
        acc = acc + _dot((up * up).astype(BF16), wd_ref[cols, :])
    o_ref[...] = acc


def _ffn(x2, lw):
    n = x2.shape[0]
    tm = _row_tile(n)
    rows = pl.BlockSpec((tm, D_MODEL), lambda i: (i, 0))
    consts = (lw['norm_ffn'], lw['w_up'], lw['w_down'])
    return pl.pallas_call(
        _ffn_body,
        grid=(n // tm,),
        in_specs=[rows] + [_full(a.shape) for a in consts],
        out_specs=rows,
        out_shape=jax.ShapeDtypeStruct((n, D_MODEL), F32),
        compiler_params=_params("parallel"),
        name="ffn",
    )(x2, *consts)


def _block_diag(blocks):
    n, r, c = blocks.shape
    eye = jnp.eye(n, dtype=blocks.dtype)
    return (eye[:, None, :, None] * blocks[:, :, None, :]).reshape(n * r, n * c)


def _layer_weights(p, l):
    w_in = p['w_in'][l]
    col = lambda off, n: w_in[:, off:off + n].astype(BF16)
    row = lambda v: v.reshape(1, -1).astype(F32)
    heads = jnp.arange(SSD_WIDTH) // SSD_HEAD_DIM
    pad_lanes = lambda v: jnp.pad(v, (0, LANES - v.shape[0])).reshape(1, LANES)
    ab_re, ab_im, bb_re, bb_im = _s5_params(p['lam_re'][l], p['lam_im'][l], p['log_step'][l],
                                            p['b_re'][l], p['b_im'][l])
    to_bb = lambda t: jnp.stack([_block_diag(blk) for blk in
                                 t.reshape(S5_NBLK, S5_BLK_GROUPS, S5_GROUP_CH, S5_STATE)]).astype(BF16)
    to_cc = lambda t: jnp.stack([_block_diag(blk) for blk in
                                 jnp.transpose(t, (0, 2, 1)).reshape(S5_NBLK, S5_BLK_GROUPS, S5_STATE, S5_GROUP_CH)]
                                ).astype(BF16)
    state_rows = jnp.arange(LANES) // SSD_STATE
    state_cols = jnp.arange(SSD_WIDTH) // (SSD_WIDTH // SSD_GROUPS)
    return {
        'norm_mix': row(p['norm_mix'][l]),
        'wz': col(OFF_Z, SSD_WIDTH), 'wxbc': col(OFF_XBC, CONV_DIM),
        'wdt': jnp.pad(col(OFF_DT, SSD_HEADS), ((0, 0), (0, LANES - SSD_HEADS))),
        'wu': col(OFF_U, S5_WIDTH), 'wq': col(OFF_Q, SB_WIDTH), 'wk': col(OFF_K, SB_WIDTH),
        'wv': col(OFF_V, SB_WIDTH), 'wgate': col(OFF_GATE, N_BRANCH * D_MODEL),
        'q_norm': row(jnp.tile(p['q_norm'][l], SB_HEADS)), 'k_norm': row(jnp.tile(p['k_norm'][l], SB_HEADS)),
        'head_mean': (_block_diag(jnp.ones((SB_HEADS, SB_HEAD_DIM, SB_HEAD_DIM), F32)) / SB_HEAD_DIM).astype(BF16),
        'conv_w': jnp.pad(p['conv_w'][l], ((0, SUBLANES - CONV_W), (0, 0))),
        'conv_b': row(p['conv_b'][l]),
        'dt_bias': pad_lanes(p['dt_bias'][l]), 'a_log': pad_lanes(p['a_log'][l]),
        'd_ssd': row(p['d_ssd'][l][heads]), 'norm_ssd': row(p['norm_ssd'][l]),
        'eye': jnp.eye(LANES, dtype=BF16),
        'state_mask': (state_rows[:, None] == state_cols[None, :]).astype(F32),
        's5_ab_re': ab_re[::S5_GROUP_CH].reshape(1, S5_LANES), 's5_ab_im': ab_im[::S5_GROUP_CH].reshape(1, S5_LANES),
        's5_bb_re': to_bb(bb_re), 's5_bb_im': to_bb(bb_im),
        's5_cc_re': to_cc(p['c_re'][l]), 's5_cc_im': to_cc(p['c_im'][l]),
        'd_s5': row(p['d_s5'][l]),
        'w_glu': p['w_glu'][l].astype(BF16), 'w_lift_a': p['w_lift_a'][l].astype(BF16),
        'w_lift_c': p['w_lift_c'][l].astype(BF16), 'w_out': p['w_out'][l].astype(BF16),
        'norm_ffn': row(p['norm_ffn'][l]), 'w_up': p['w_up'][l].astype(BF16), 'w_down': p['w_down'][l].astype(BF16),
    }


def _ssd_state_in(s):
    b = s.shape[0]
    t = jnp.transpose(s.astype(F32), (0, 1, 4, 2, 3)).reshape(b, SSD_GROUPS, SSD_STATE, SSD_HPG * SSD_HEAD_DIM)
    eye = jnp.eye(SSD_GROUPS, dtype=F32)
    return (t[:, :, :, None, :] * eye[None, :, None, :, None]).reshape(b, LANES, SSD_WIDTH)


def _ssd_state_out(s):
    b = s.shape[0]
    t = s.reshape(b, SSD_GROUPS, SSD_STATE, SSD_GROUPS, SSD_HPG, SSD_HEAD_DIM)
    t = jnp.stack([t[:, g, :, g] for g in range(SSD_GROUPS)], axis=1)
    return jnp.transpose(t, (0, 1, 3, 4, 2))


def _trunk_layer(x, layer, k_buf, v_buf, k_hist, v_hist, conv_hist, ssd_s0, s5_re0, s5_im0, lw):
    b, t, _ = x.shape
    x2 = x.reshape(b * t, D_MODEL)
    z, xbc, dt, u, k_buf, v_buf, qb, kb, vb = _proj(x2, layer, k_buf, v_buf, lw)
    seq = lambda a: a.reshape(b, t, a.shape[-1])
    hist8 = jnp.pad(conv_hist.astype(F32), ((0, 0), (SUBLANES - (CONV_W - 1), 0), (0, 0)))
    y_a, ssd_new = _ssd(seq(z), seq(xbc), seq(dt), hist8, _ssd_state_in(ssd_s0), lw)
    y_b, s5_re, s5_im = _s5(seq(u), s5_re0.reshape(-1, S5_LANES).astype(F32),
                            s5_im0.reshape(-1, S5_LANES).astype(F32), lw)
    if k_hist is None:
        kh = vh = None
    else:
        kh = k_hist.reshape(k_hist.shape[0], -1, SB_WIDTH).astype(F32)
        vh = v_hist.reshape(v_hist.shape[0], -1, SB_WIDTH).astype(F32)
    o_c = _attn(seq(qb), seq(kb), seq(vb), kh, vh)
    x2 = _merge(x2, y_a.reshape(b * t, SSD_WIDTH), y_b.reshape(b * t, S5_WIDTH), o_c.reshape(b * t, SB_WIDTH), lw)
    x2 = _ffn(x2, lw)
    conv_rows = jnp.concatenate([jnp.broadcast_to(conv_hist.astype(F32), (b, CONV_W - 1, CONV_DIM)),
                                 seq(xbc)[:, -(CONV_W - 1):]], axis=1)[:, -(CONV_W - 1):]
    return (x2.reshape(b, t, D_MODEL), k_buf, v_buf, conv_rows, _ssd_state_out(ssd_new),
            s5_re.reshape(b, S5_GROUPS, S5_STATE), s5_im.reshape(b, S5_GROUPS, S5_STATE))


def kernel(x_prompt, x_sample, cache_k, cache_v, state_conv, state_ssd, state_s5_re, state_s5_im, meta_tokens, norm_mix, w_in, conv_w, conv_b, dt_bias, a_log, d_ssd, norm_ssd, lam_re, lam_im, log_step, b_re, b_im, c_re, c_im, d_s5, w_glu, q_norm, k_norm, w_lift_a, w_lift_c, w_out, norm_ffn, w_up, w_down):
    p = dict(norm_mix=norm_mix, w_in=w_in, conv_w=conv_w, conv_b=conv_b, dt_bias=dt_bias, a_log=a_log,
             d_ssd=d_ssd, norm_ssd=norm_ssd, lam_re=lam_re, lam_im=lam_im, log_step=log_step, b_re=b_re,
             b_im=b_im, c_re=c_re, c_im=c_im, d_s5=d_s5, w_glu=w_glu, q_norm=q_norm, k_norm=k_norm,
             w_lift_a=w_lift_a, w_lift_c=w_lift_c, w_out=w_out, norm_ffn=norm_ffn, w_up=w_up, w_down=w_down)
    depth = w_in.shape[0]
    bp, tp, _ = x_prompt.shape
    bs, ts, _ = x_sample.shape
    xm = meta_tokens.astype(x_prompt.dtype)[None]
    xp, xs = x_prompt, x_sample
    zeros = lambda *s: jnp.zeros(s, F32)
    kv_m = [zeros(depth, N_META, SB_WIDTH), zeros(depth, N_META, SB_WIDTH)]
    kv_p = [zeros(depth, bp * tp, SB_WIDTH), zeros(depth, bp * tp, SB_WIDTH)]
    kv_s = [zeros(depth, bs * ts, SB_WIDTH), zeros(depth, bs * ts, SB_WIDTH)]
    outs_p, outs_s = [], []
    for l in range(depth):
        lw = _layer_weights(p, l)
        xm, *kv_m, conv_m, ssd_m, s5re_m, s5im_m = _trunk_layer(
            xm, l, *kv_m, None, None, zeros(1, CONV_W - 1, CONV_DIM),
            zeros(1, SSD_GROUPS, SSD_HPG, SSD_HEAD_DIM, SSD_STATE),
            zeros(1, S5_GROUPS, S5_STATE), zeros(1, S5_GROUPS, S5_STATE), lw)
        xp, *kv_p, conv_p, ssd_p, s5re_p, s5im_p = _trunk_layer(
            xp, l, *kv_p, kv_m[0][l][None], kv_m[1][l][None], conv_m, ssd_m, s5re_m, s5im_m, lw)
        outs_p.append([conv_p, ssd_p, s5re_p, s5im_p])
        xs, *kv_s, conv_s, ssd_s, s5re_s, s5im_s = _trunk_layer(
            xs, l, *kv_s, cache_k[l], cache_v[l], state_conv[l], state_ssd[l], state_s5_re[l], state_s5_im[l], lw)
        outs_s.append([conv_s, ssd_s, s5re_s, s5im_s])
    stk = lambda outs, i: jnp.stack([o[i] for o in outs], axis=0)

    k_prompt, v_prompt = _kv_rows(kv_m[0], kv_m[1], kv_p[0], kv_p[1], bp)
    sample_rows = lambda new: new.reshape(depth, bs, ts, SB_HEADS, SB_HEAD_DIM)
    return (xp, xs,
            k_prompt, v_prompt,
            stk(outs_p, 0), stk(outs_p, 1), stk(outs_p, 2), stk(outs_p, 3),
            sample_rows(kv_s[0]), sample_rows(kv_s[1]),
            stk(outs_s, 0), stk(outs_s, 1), stk(outs_s, 2), stk(outs_s, 3))
```

```python
import functools
import math

import jax
import jax.numpy as jnp
from jax import lax
from jax.experimental import pallas as pl
from jax.experimental.pallas import tpu as pltpu

F32 = jnp.float32
BF16 = jnp.bfloat16

D_MODEL = 1024
N_META = 16
RMS_EPS = 1e-6
SSD_HEADS = 16
SSD_HEAD_DIM = 64
SSD_GROUPS = 2
SSD_HPG = SSD_HEADS // SSD_GROUPS
SSD_STATE = 64
SSD_WIDTH = SSD_HEADS * SSD_HEAD_DIM
CONV_W = 4
N_BC = SSD_GROUPS * SSD_STATE
CONV_DIM = SSD_WIDTH + 2 * N_BC
S5_WIDTH = D_MODEL // 2
S5_GROUP_CH = 16
S5_GROUPS = S5_WIDTH // S5_GROUP_CH
S5_STATE = 64
S5_LANES = S5_GROUPS * S5_STATE
SB_HEADS = 8
SB_HEAD_DIM = 64
SB_WIDTH = SB_HEADS * SB_HEAD_DIM
SB_SCALE = 1.0 / math.sqrt(SB_HEAD_DIM)
N_BRANCH = 3
D_FF = 4 * D_MODEL
OFF_Z = 0
OFF_XBC = OFF_Z + SSD_WIDTH
OFF_DT = OFF_XBC + CONV_DIM
OFF_U = OFF_DT + SSD_HEADS
OFF_Q = OFF_U + S5_WIDTH
OFF_K = OFF_Q + SB_WIDTH
OFF_V = OFF_K + SB_WIDTH
OFF_GATE = OFF_V + SB_WIDTH

LANES = 128
SUBLANES = 8
VMEM_LIMIT = 56 * 1024 * 1024
S5_BLK_GROUPS = LANES // S5_GROUP_CH
S5_BLK_LANES = S5_BLK_GROUPS * S5_STATE
S5_NBLK = S5_GROUPS // S5_BLK_GROUPS


def _dot(a, b):
    return jnp.dot(a, b, preferred_element_type=F32)


def _dot_nt(a, b):
    return lax.dot_general(a, b, (((1,), (1,)), ((), ())), preferred_element_type=F32)


def _split(a, terms):
    out = []
    r = a
    for _ in range(terms):
        p = r.astype(BF16)
        out.append(p)
        r = r - p.astype(F32)
    return out


def _dot_split_lhs(a, b_bf16, terms=3):
    return sum(_dot(p, b_bf16) for p in _split(a, terms))


def _dot_split_rhs(a_bf16, b, terms=3):
    return sum(_dot(a_bf16, p) for p in _split(b, terms))


def _sigmoid(x):
    return 1.0 / (1.0 + jnp.exp(-x))


def _softplus(x):
    return jnp.maximum(x, 0.0) + jnp.log(1.0 + jnp.exp(-jnp.abs(x)))


def _rmsnorm_rows(x, g):
    return x * lax.rsqrt(jnp.mean(x * x, axis=-1, keepdims=True) + RMS_EPS) * g


def _params(*sem):
    return pltpu.CompilerParams(dimension_semantics=sem, vmem_limit_bytes=VMEM_LIMIT)


def _full(shape):
    n = len(shape)
    return pl.BlockSpec(shape, lambda *_: (0,) * n)


def _row_tile(n):
    for t in (256, 128, 64, 32, 16, 8):
        if n % t == 0:
            return t
    raise ValueError(f"token count {n} is not a multiple of {SUBLANES}")


def _proj_body(x_ref, g_ref, wz_ref, wxbc_ref, wdt_ref, wu_ref, wq_ref, wk_ref, wv_ref,
               qn_ref, kn_ref, hm_ref, hx_ref,
               z_ref, xbc_ref, dt_ref, u_ref, k_ref, v_ref, qb_ref, kb_ref, vb_ref):
    hb =_rmsnorm_rows(x_ref[...], g_ref[...]).astype(BF16)
    z_ref[...] = _dot(hb, wz_ref[...])
    xbc_ref[...] = _dot(hb, wxbc_ref[...])
    dt_ref[...] = _dot(hb, wdt_ref[...])
    u_ref[...] = _dot(hb, wu_ref[...])
    def head_norm(t, w):
        ms = _dot((t * t).astype(BF16), hm_ref[...])
        return t * _dot_split_lhs(lax.rsqrt(ms + RMS_EPS), hx_ref[...], 2) * w

    q = head_norm(_dot(hb, wq_ref[...]), qn_ref[...])
    k = head_norm(_dot(hb, wk_ref[...]), kn_ref[...])
    v = _dot(hb, wv_ref[...])
    k_ref[...] = k
    v_ref[...] = v
    qb_ref[...] = (q * SB_SCALE).astype(BF16)
    kb_ref[...] = k.astype(BF16)
    vb_ref[...] = v.astype(BF16)


def _proj(x2, lw):
    n = x2.shape[0]
    tm = _row_tile(n)
    rows = lambda w: pl.BlockSpec((tm, w), lambda i: (i, 0))
    widths = (SSD_WIDTH, CONV_DIM, LANES, S5_WIDTH, SB_WIDTH, SB_WIDTH, SB_WIDTH, SB_WIDTH, SB_WIDTH)
    dtypes = (F32, F32, F32, F32, F32, F32, BF16, BF16, BF16)
    ins = (lw['norm_mix'], lw['wz'], lw['wxbc'], lw['wdt'], lw['wu'], lw['wq'], lw['wk'], lw['wv'],
           lw['q_norm'], lw['k_norm'], lw['head_mean'], lw['head_expand'])
    return pl.pallas_call(
        _proj_body,
        grid=(n // tm,),
        in_specs=[rows(D_MODEL)] + [_full(a.shape) for a in ins],
        out_specs=[rows(w) for w in widths],
        out_shape=[jax.ShapeDtypeStruct((n, w), d) for w, d in zip(widths, dtypes)],
        compiler_params=_params("parallel"),
        name="proj",
    )(x2, *ins)


def _ssd_body(z_ref, xbc_ref, dt_ref, hist_ref, s0_ref, cw_ref, cb_ref, dtb_ref, alog_ref,
              dskip_ref, nw_ref, eye_ref, smask_ref, tri_ref,
              y_ref, sout_ref, xp_ref, st_ref, yacc_ref, *, lc):
    c = pl.program_id(1)

    @pl.when(c == 0)
    def _():
        xp_ref[0:SUBLANES, :] = hist_ref[0]
        st_ref[...] = s0_ref[0]

    xp_ref[SUBLANES:SUBLANES + lc, :] = xbc_ref[0]
    conv = cb_ref[...]
    for k in range(CONV_W):
        lo = SUBLANES - (CONV_W - 1) + k
        conv = conv + xp_ref[lo:lo + lc, :] * cw_ref[k:k + 1, :]
    xp_ref[0:SUBLANES, :] = xp_ref[lc:lc + SUBLANES, :]
    act = conv * _sigmoid(conv)
    xs = act[:, :SSD_WIDTH]
    bm = act[:, SSD_WIDTH:SSD_WIDTH + N_BC]
    cm = act[:, SSD_WIDTH + N_BC:]

    eye = eye_ref[...]
    dt = _softplus(dt_ref[0] + dtb_ref[...])
    a = -jnp.exp(alog_ref[...])
    ri = lax.broadcasted_iota(jnp.int32, (lc, lc), 0)
    ci = lax.broadcasted_iota(jnp.int32, (lc, lc), 1)
    causal = ri >= ci
    acum = _dot_split_rhs(tri_ref[...], dt * a)
    if lc == LANES:
        transpose = lambda t: t.T
    else:
        transpose = lambda t: sum(_dot_nt(eye, p) for p in _split(t, 3))
    acum_t = transpose(acum)
    log_dt_t = jnp.log(transpose(dt))
    a_last = acum[lc - 1:lc, :]
    shifted_t = acum_t - log_dt_t
    to_end_t = jnp.exp(acum_t[:, lc - 1:lc] - shifted_t)
    dec = jnp.exp(a_last)

    xs_b = xs.astype(BF16)
    bm_b = bm.astype(BF16)
    cm_b = cm.astype(BF16)
    bm_t = bm.T if lc == LANES else _dot_nt(eye, bm_b)
    lane = lax.broadcasted_iota(jnp.int32, (lc, LANES), 1)
    left = lane < SSD_STATE
    left_sq = lax.broadcasted_iota(jnp.int32, (LANES, LANES), 1) < SSD_HEAD_DIM
    zero_b = jnp.zeros((), BF16)
    cbs = [_dot_nt(jnp.where(left if g == 0 else ~left, cm_b, zero_b), bm_b) for g in range(SSD_GROUPS)]
    st = st_ref[...]
    st_b = st.astype(BF16)
    smask = smask_ref[...]

    for hp in range(SSD_HEADS // 2):
        pair = slice(hp * LANES, (hp + 1) * LANES)
        rhs = jnp.concatenate([xs_b[:, pair], st_b[:, pair]], axis=0)
        outs, news = [], []
        for h in (2 * hp, 2 * hp + 1):
            col = jnp.broadcast_to(acum[:, h:h + 1], (lc, LANES))
            seg = col[:, :lc] - shifted_t[h:h + 1, :]
            m = jnp.where(causal, cbs[h // SSD_HPG] * jnp.exp(seg), 0.0)
            read = cm * jnp.exp(col)
            outs.append(_dot(jnp.concatenate([m.astype(BF16), read.astype(BF16)], axis=1), rhs))
            news.append(_dot((bm_t * to_end_t[h:h + 1, :]).astype(BF16), xs_b[:, pair]))
        yacc_ref[:, pair] = jnp.where(left, outs[0], outs[1])
        decay = jnp.where(left_sq, dec[:, 2 * hp:2 * hp + 1], dec[:, 2 * hp + 1:2 * hp + 2])
        st_ref[:, pair] = st[:, pair] * decay + smask[:, pair] * jnp.where(left_sq, news[0], news[1])

    y = (yacc_ref[...] + dskip_ref[...] * xs)
    zz = z_ref[0]
    y = y * (zz * _sigmoid(zz))
    gw = SSD_WIDTH // SSD_GROUPS
    parts = []
    for g in range(SSD_GROUPS):
        yg = y[:, g * gw:(g + 1) * gw]
        parts.append(yg * lax.rsqrt(jnp.mean(yg * yg, axis=-1, keepdims=True) + RMS_EPS))
    y_ref[0] = (jnp.concatenate(parts, axis=1) * nw_ref[...]).astype(BF16)

    @pl.when(c == pl.num_programs(1) - 1)
    def _():
        sout_ref[0] = st_ref[...]


def _ssd(z3, xbc3, dt3, hist, s0, lw):
    b, t, _ = z3.shape
    lc = min(t, 128)
    shared = hist.shape[0] == 1
    per_b = (lambda i, c: (0, 0, 0)) if shared else (lambda i, c: (i, 0, 0))
    seq = lambda w: pl.BlockSpec((1, lc, w), lambda i, c: (i, c, 0))
    consts = (lw['conv_w'], lw['conv_b'], lw['dt_bias'], lw['a_log'], lw['d_ssd'], lw['norm_ssd'],
              lw['eye'], lw['state_mask'], jnp.tril(jnp.ones((lc, lc), BF16)))
    return pl.pallas_call(
        functools.partial(_ssd_body, lc=lc),
        grid=(b, t // lc),
        in_specs=[seq(SSD_WIDTH), seq(CONV_DIM), seq(LANES),
                  pl.BlockSpec((1, SUBLANES, CONV_DIM), per_b),
                  pl.BlockSpec((1, LANES, SSD_WIDTH), per_b)] + [_full(a.shape) for a in consts],
        out_specs=[seq(SSD_WIDTH), pl.BlockSpec((1, LANES, SSD_WIDTH), lambda i, c: (i, 0, 0))],
        out_shape=[jax.ShapeDtypeStruct((b, t, SSD_WIDTH), BF16),
                   jax.ShapeDtypeStruct((b, LANES, SSD_WIDTH), F32)],
        scratch_shapes=[pltpu.VMEM((lc + SUBLANES, CONV_DIM), F32),
                        pltpu.VMEM((LANES, SSD_WIDTH), F32),
                        pltpu.VMEM((lc, SSD_WIDTH), F32)],
        compiler_params=_params("parallel", "arbitrary"),
        name="ssd",
    )(z3, xbc3, dt3, hist, s0, *consts)


def _s5_param_body(lr_ref, li_ref, step_ref, br_ref, bi_ref, abr_ref, abi_ref, bbr_ref, bbi_ref):
    lr, li, step = lr_ref[...], li_ref[...], jnp.exp(step_ref[...])
    mag = jnp.exp(lr * step)
    ab_re = mag * jnp.cos(li * step)
    ab_im = mag * jnp.sin(li * step)
    den = lr * lr + li * li
    nr = ab_re - 1.0
    f_re = (nr * lr + ab_im * li) / den
    f_im = (ab_im * lr - nr * li) / den
    br, bi = br_ref[...], bi_ref[...]
    abr_ref[...] = ab_re
    abi_ref[...] = ab_im
    bbr_ref[...] = f_re * br - f_im * bi
    bbi_ref[...] = f_re * bi + f_im * br


def _s5_params(lam_re, lam_im, log_step, b_re, b_im):
    rep = lambda t: jnp.repeat(t, S5_GROUP_CH, axis=0)
    shape = (S5_WIDTH, S5_STATE)
    step = jnp.broadcast_to(rep(log_step[:, None]), shape)
    to_rows = lambda t: jnp.transpose(t, (0, 2, 1)).reshape(shape)
    outs = pl.pallas_call(
        _s5_param_body,
        out_shape=[jax.ShapeDtypeStruct(shape, F32)] * 4,
        name="s5_params",
    )(rep(lam_re), rep(lam_im), step, to_rows(b_re), to_rows(b_im))
    return outs


S5_STEPS = 16
S5_SEQS = 16
S5_SCAN_LANES = 256


def _s5_body(u_ref, sre_ref, sim_ref, abr_ref, abi_ref, bbr_ref, bbi_ref, ccr_ref, cci_ref, dsk_ref,
             perm_ref, perm_t_ref, y_ref, ore_ref, oim_ref, xre_ref, xim_ref, cr_ref, ci_ref, *, nb):
    c = pl.program_id(1)
    rows = nb * S5_STEPS

    @pl.when(c == 0)
    def _():
        cr_ref[...] = jnp.broadcast_to(sre_ref[...], cr_ref.shape)
        ci_ref[...] = jnp.broadcast_to(sim_ref[...], ci_ref.shape)

    u = u_ref[...].reshape(rows, S5_WIDTH)
    u_tm = _dot(perm_ref[...], u.astype(BF16)).astype(BF16)
    perm_t = perm_t_ref[...]
    for j in range(S5_NBLK):
        ub = u_tm[:, j * LANES:(j + 1) * LANES]
        xre_ref[j] = _dot(ub, bbr_ref[j])
        xim_ref[j] = _dot(ub, bbi_ref[j])
    for j in range(S5_NBLK):
        xre, xim = xre_ref.at[j], xim_ref.at[j]
        for h in range(S5_BLK_LANES // S5_SCAN_LANES):
            cols = slice(h * S5_SCAN_LANES, (h + 1) * S5_SCAN_LANES)
            lanes = slice(j * S5_BLK_LANES + h * S5_SCAN_LANES, j * S5_BLK_LANES + (h + 1) * S5_SCAN_LANES)
            ar, ai = abr_ref[:, lanes], abi_ref[:, lanes]
            cr, ci = cr_ref[:, lanes], ci_ref[:, lanes]
            for s in range(S5_STEPS):
                step = slice(s * nb, (s + 1) * nb)
                cr, ci = (ar * cr - ai * ci + xre[step, cols], ar * ci + ai * cr + xim[step, cols])
                xre[step, cols] = cr
                xim[step, cols] = ci
            cr_ref[:, lanes] = cr
            ci_ref[:, lanes] = ci
    for j in range(S5_NBLK):
        yx = _dot(xre_ref[j].astype(BF16), ccr_ref[j]) - _dot(xim_ref[j].astype(BF16), cci_ref[j])
        yb = _dot_split_rhs(perm_t, yx, 2) + dsk_ref[:, j * LANES:(j + 1) * LANES] * u[:, j * LANES:(j + 1) * LANES]
        gelu = 0.5 * yb * (1.0 + jnp.tanh(math.sqrt(2.0 / math.pi) * (yb + 0.044715 * (yb * yb * yb))))
        y_ref[:, :, j * LANES:(j + 1) * LANES] = gelu.astype(BF16).reshape(nb, S5_STEPS, LANES)

    @pl.when(c == pl.num_programs(1) - 1)
    def _():
        ore_ref[...] = cr_ref[...]
        oim_ref[...] = ci_ref[...]


def _s5(u3, s_re, s_im, lw):
    b, t, _ = u3.shape
    shared = s_re.shape[0] == 1
    bp = -(-b // SUBLANES) * SUBLANES
    if bp != b:
        u3 = jnp.pad(u3, ((0, bp - b), (0, 0), (0, 0)))
        if not shared:
            s_re, s_im = (jnp.pad(s, ((0, bp - b), (0, 0))) for s in (s_re, s_im))
    nb = min(bp, S5_SEQS)
    rows = nb * S5_STEPS
    r = jnp.arange(rows)
    perm = ((r % S5_STEPS) * nb + r // S5_STEPS)[None, :] == r[:, None]
    perm, perm_t = perm.astype(BF16), perm.T.astype(BF16)
    state = pl.BlockSpec((1, S5_LANES), lambda i, c: (0, 0)) if shared else pl.BlockSpec((nb, S5_LANES), lambda i, c: (i, 0))
    state_out = pl.BlockSpec((nb, S5_LANES), lambda i, c: (i, 0))
    seq = pl.BlockSpec((nb, S5_STEPS, S5_WIDTH), lambda i, c: (i, c, 0))
    consts = (lw['s5_ab_re'], lw['s5_ab_im'], lw['s5_bb_re'], lw['s5_bb_im'], lw['s5_cc_re'],
              lw['s5_cc_im'], lw['d_s5'], perm, perm_t)
    y, o_re, o_im = pl.pallas_call(
        functools.partial(_s5_body, nb=nb),
        grid=(bp // nb, t // S5_STEPS),
        in_specs=[seq, state, state] + [_full(a.shape) for a in consts],
        out_specs=[seq, state_out, state_out],
        out_shape=[jax.ShapeDtypeStruct((bp, t, S5_WIDTH), BF16),
                   jax.ShapeDtypeStruct((bp, S5_LANES), F32),
                   jax.ShapeDtypeStruct((bp, S5_LANES), F32)],
        scratch_shapes=[pltpu.VMEM((S5_NBLK, rows, S5_BLK_LANES), F32), pltpu.VMEM((S5_NBLK, rows, S5_BLK_LANES), F32),
                        pltpu.VMEM((nb, S5_LANES), F32), pltpu.VMEM((nb, S5_LANES), F32)],
        compiler_params=_params("parallel", "arbitrary"),
        name="s5",
    )(u3, s_re, s_im, *consts)
    return y[:b], o_re[:b], o_im[:b]


SB_KEYS = 256
SB_PAIRS = SB_WIDTH // LANES
SB_DEAD = -105.0


def _attn_body(*refs, bq, n_hist, pad):
    if n_hist:
        q_ref, upper_ref, kn_ref, vn_ref, kh_ref, vh_ref, o_ref, kall_ref, vall_ref, acc_ref, carry_ref = refs
    else:
        q_ref, upper_ref, kn_ref, vn_ref, o_ref, kall_ref, vall_ref, acc_ref, carry_ref = refs
    qi = pl.program_id(1)

    @pl.when(qi == 0)
    def _():
        for all_ref, new_ref, hist_ref in ((kall_ref, kn_ref, kh_ref if n_hist else None),
                                           (vall_ref, vn_ref, vh_ref if n_hist else None)):
            if pad:
                all_ref[0:pad, :] = jnp.zeros((pad, SB_WIDTH), BF16)
            if n_hist:
                all_ref[pad:pad + n_hist, :] = hist_ref[0].astype(BF16)
            all_ref[pad + n_hist:, :] = new_ref[0]

    k_ref, v_ref = kall_ref, vall_ref
    q = q_ref[0]
    lane = lax.broadcasted_iota(jnp.int32, (bq, LANES), 1)
    left = lane < SB_HEAD_DIM
    zero_b = jnp.zeros((), BF16)
    qms = [jnp.concatenate([jnp.where(left, q[:, p * LANES:(p + 1) * LANES], zero_b),
                            jnp.where(left, zero_b, q[:, p * LANES:(p + 1) * LANES])], axis=0)
           for p in range(SB_PAIRS)]
    ri = lax.broadcasted_iota(jnp.int32, (2 * bq, SB_KEYS), 0)
    ci = lax.broadcasted_iota(jnp.int32, (2 * bq, SB_KEYS), 1)
    ri = jnp.where(ri >= bq, ri - bq, ri)
    strictly_earlier = ci - (SB_KEYS - bq) < ri
    upper = upper_ref[...]
    n_before = n_hist + (qi + 1) * bq
    end = pad + n_before
    trips = (n_before + SB_KEYS - 1) // SB_KEYS
    acc_ref[...] = jnp.zeros_like(acc_ref)
    carry_ref[...] = jnp.zeros_like(carry_ref)

    def visit(it, masked):
        rows = pl.ds(pl.multiple_of(end - (it + 1) * SB_KEYS, 16), SB_KEYS)
        zs = [_dot_nt(qms[p], k_ref[rows, p * LANES:(p + 1) * LANES]) for p in range(SB_PAIRS)]
        log_ws, totals = [], []
        for z in zs:
            sp = jnp.log(1.0 + jnp.exp(-jnp.abs(z)))
            log_beta = jnp.minimum(z, 0.0) - sp
            log_keep = log_beta - z
            if masked:
                log_keep = jnp.where(strictly_earlier, log_keep, 0.0)
            hi = log_keep.astype(BF16)
            lo = (log_keep - hi.astype(F32)).astype(BF16)
            both = _dot(jnp.concatenate([hi, lo], axis=0), upper)
            later = both[:2 * bq] + both[2 * bq:]
            log_ws.append(log_beta + later)
            totals.append(later[:, 0:1] + log_keep[:, 0:1])
        for p in range(SB_PAIRS):
            carry = carry_ref[p]
            w = jnp.exp(log_ws[p] + carry)
            if masked:
                w = jnp.where(strictly_earlier, w, 0.0)
            pv = _dot(w.astype(BF16), v_ref[rows, p * LANES:(p + 1) * LANES])
            carry_ref[p] = carry + totals[p]
            acc_ref[:, p * LANES:(p + 1) * LANES] += jnp.where(left, pv[:bq], pv[bq:])

    def any_weight_left():
        c = carry_ref[0]
        for p in range(1, SB_PAIRS):
            c = jnp.maximum(c, carry_ref[p])
        return jnp.max(c) > SB_DEAD

    visit(0, True)

    def trip(state):
        it, _ = state
        visit(it, False)
        return it + 1, any_weight_left()

    lax.while_loop(lambda s: jnp.logical_and(s[0] < trips, s[1]), trip, (jnp.int32(1), any_weight_left()))
    o_ref[0] = acc_ref[...].astype(BF16)


BF16_ROWS = 16


def _attn(qb, kb, vb, kh, vh):
    b, t, _ = qb.shape
    bq = min(t, 128)
    n_hist = 0 if kh is None else kh.shape[1]
    assert n_hist % BF16_ROWS == 0 and bq % BF16_ROWS == 0
    short = max(-(n_hist + (qi + 1) * bq) % SB_KEYS for qi in range(t // bq))
    pad = -(-short // BF16_ROWS) * BF16_ROWS
    block = pl.BlockSpec((1, bq, SB_WIDTH), lambda i, j: (i, j, 0))
    new = pl.BlockSpec((1, t, SB_WIDTH), lambda i, j: (i, 0, 0))
    upper = jnp.tril(jnp.ones((SB_KEYS, SB_KEYS), BF16), -1)
    args, specs = [qb, upper, kb, vb], [block, _full(upper.shape), new, new]
    if n_hist:
        hmap = (lambda i, j: (0, 0, 0)) if kh.shape[0] == 1 else (lambda i, j: (i, 0, 0))
        args += [kh, vh]
        specs += [pl.BlockSpec((1, n_hist, SB_WIDTH), hmap)] * 2
    rows = pad + n_hist + t
    return pl.pallas_call(
        functools.partial(_attn_body, bq=bq, n_hist=n_hist, pad=pad),
        grid=(b, t // bq),
        in_specs=specs,
        out_specs=block,
        out_shape=jax.ShapeDtypeStruct((b, t, SB_WIDTH), BF16),
        scratch_shapes=[pltpu.VMEM((rows, SB_WIDTH), BF16), pltpu.VMEM((rows, SB_WIDTH), BF16),
                        pltpu.VMEM((bq, SB_WIDTH), F32), pltpu.VMEM((SB_PAIRS, 2 * bq, 1), F32)],
        compiler_params=_params("parallel", "arbitrary"),
        name="attn",
    )(*args)


def _kv_rows_body(*refs, b, t, depth):
    ko_ref, vo_ref = refs[4 * depth:]
    layer = pl.program_id(0) // b
    for l in range(depth):
        km_ref, vm_ref, kn_ref, vn_ref = refs[4 * l:4 * l + 4]
        n_meta = km_ref.shape[0]
        total = n_meta + t

        @pl.when(layer == l)
        def _(km_ref=km_ref, vm_ref=vm_ref, kn_ref=kn_ref, vn_ref=vn_ref, n_meta=n_meta, total=total):
            for meta_ref, new_ref, out_ref in ((km_ref, kn_ref, ko_ref), (vm_ref, vn_ref, vo_ref)):
                for c in range(-(-total // LANES)):
                    lo, hi = c * LANES, min((c + 1) * LANES, total)
                    parts = []
                    if lo < n_meta:
                        parts.append(meta_ref[lo:min(hi, n_meta), :])
                    if hi > n_meta:
                        parts.append(new_ref[max(lo, n_meta) - n_meta:hi - n_meta, :])
                    if hi - lo < LANES:
                        parts.append(jnp.zeros((LANES - (hi - lo), SB_WIDTH), F32))
                    blk = parts[0] if len(parts) == 1 else jnp.concatenate(parts, axis=0)
                    blk_t = blk.T.reshape(SB_HEADS, SB_HEAD_DIM, LANES)
                    out_ref[0, 0, :, :, lo:hi] = blk_t[:, :, :hi - lo]


def _kv_rows(meta_rows, new_rows, b):
    depth = len(new_rows)
    n = new_rows[0][0].shape[0]
    t = n // b
    n_meta = meta_rows[0][0].shape[0]
    args, specs = [], []
    for l in range(depth):
        rows = pl.BlockSpec((t, SB_WIDTH), lambda i, l=l: (jnp.clip(i - l * b, 0, b - 1), 0))
        args += [*meta_rows[l], *new_rows[l]]
        specs += [_full((n_meta, SB_WIDTH))] * 2 + [rows] * 2
    out = pl.BlockSpec((1, 1, SB_HEADS, SB_HEAD_DIM, n_meta + t), lambda i: (i // b, i % b, 0, 0, 0))
    shape = jax.ShapeDtypeStruct((depth, b, SB_HEADS, SB_HEAD_DIM, n_meta + t), F32)
    k_out, v_out = pl.pallas_call(
        functools.partial(_kv_rows_body, b=b, t=t, depth=depth),
        grid=(depth * b,),
        in_specs=specs,
        out_specs=[out, out],
        out_shape=[shape, shape],
        compiler_params=_params("arbitrary"),
        name="kv_rows",
    )(*args)
    return jnp.transpose(k_out, (0, 1, 4, 2, 3)), jnp.transpose(v_out, (0, 1, 4, 2, 3))


def _merge_body(x_ref, ya_ref, yb_ref, oc_ref, g_ref, wg_ref, wa_ref, wglu_ref, wc_ref, wo_ref, o_ref):
    x = x_ref[...]
    hb = _rmsnorm_rows(x, g_ref[...]).astype(BF16)
    gates = _sigmoid(_dot(hb, wg_ref[...]))
    glu = _dot(yb_ref[...], wglu_ref[...])
    mix = (gates[:, :D_MODEL] * _dot(ya_ref[...], wa_ref[...])
           + gates[:, D_MODEL:2 * D_MODEL] * (glu[:, :D_MODEL] * _sigmoid(glu[:, D_MODEL:]))
           + gates[:, 2 * D_MODEL:] * _dot(oc_ref[...], wc_ref[...]))
    o_ref[...] = x + _dot(mix.astype(BF16), wo_ref[...])


def _merge(x2, ya, yb, oc, lw):
    n = x2.shape[0]
    tm = _row_tile(n)
    rows = lambda w: pl.BlockSpec((tm, w), lambda i: (i, 0))
    consts = (lw['norm_mix'], lw['wgate'], lw['w_lift_a'], lw['w_glu'], lw['w_lift_c'], lw['w_out'])
    return pl.pallas_call(
        _merge_body,
        grid=(n // tm,),
        in_specs=[rows(D_MODEL), rows(SSD_WIDTH), rows(S5_WIDTH), rows(SB_WIDTH)]
                 + [_full(a.shape) for a in consts],
        out_specs=rows(D_MODEL),
        out_shape=jax.ShapeDtypeStruct((n, D_MODEL), F32),
        compiler_params=_params("parallel"),
        name="merge",
    )(x2, ya, yb, oc, *consts)


FF_CHUNK = 1024


def _ffn_body(x_ref, g_ref, wu_ref, wd_ref, o_ref):
    x = x_ref[...]
    hb = _rmsnorm_rows(x, g_ref[...]).astype(BF16)
    acc = x
    for j in range(D_FF // FF_CHUNK):
        cols = slice(j * FF_CHUNK, (j + 1) * FF_CHUNK)
        up = jnp.maximum(_dot(hb, wu_ref[:, cols]), 0.0)
        acc = acc + _dot((up * up).astype(BF16), wd_ref[cols, :])
    o_ref[...] = acc


def _ffn(x2, lw):
    n = x2.shape[0]
    tm = _row_tile(n)
    rows = pl.BlockSpec((tm, D_MODEL), lambda i: (i, 0))
    consts = (lw['norm_ffn'], lw['w_up'], lw['w_down'])
    return pl.pallas_call(
        _ffn_body,
        grid=(n // tm,),
        in_specs=[rows] + [_full(a.shape) for a in consts],
        out_specs=rows,
        out_shape=jax.ShapeDtypeStruct((n, D_MODEL), F32),
        compiler_params=_params("parallel"),
        name="ffn",
    )(x2, *consts)


def _block_diag(blocks):
    n, r, c = blocks.shape
    eye = jnp.eye(n, dtype=blocks.dtype)
    return (eye[:, None, :, None] * blocks[:, :, None, :]).reshape(n * r, n * c)


def _layer_weights(p, l):
    w_in = p['w_in'][l]
    col = lambda off, n: w_in[:, off:off + n].astype(BF16)
    row = lambda v: v.reshape(1, -1).astype(F32)
    heads = jnp.arange(SSD_WIDTH) // SSD_HEAD_DIM
    pad_lanes = lambda v: jnp.pad(v, (0, LANES - v.shape[0])).reshape(1, LANES)
    ab_re, ab_im, bb_re, bb_im = _s5_params(p['lam_re'][l], p['lam_im'][l], p['log_step'][l],
                                            p['b_re'][l], p['b_im'][l])
    to_bb = lambda t: jnp.stack([_block_diag(blk) for blk in
                                 t.reshape(S5_NBLK, S5_BLK_GROUPS, S5_GROUP_CH, S5_STATE)]).astype(BF16)
    to_cc = lambda t: jnp.stack([_block_diag(blk) for blk in
                                 jnp.transpose(t, (0, 2, 1)).reshape(S5_NBLK, S5_BLK_GROUPS, S5_STATE, S5_GROUP_CH)]
                                ).astype(BF16)
    state_rows = jnp.arange(LANES) // SSD_STATE
    state_cols = jnp.arange(SSD_WIDTH) // (SSD_WIDTH // SSD_GROUPS)
    return {
        'norm_mix': row(p['norm_mix'][l]),
        'wz': col(OFF_Z, SSD_WIDTH), 'wxbc': col(OFF_XBC, CONV_DIM),
        'wdt': jnp.pad(col(OFF_DT, SSD_HEADS), ((0, 0), (0, LANES - SSD_HEADS))),
        'wu': col(OFF_U, S5_WIDTH), 'wq': col(OFF_Q, SB_WIDTH), 'wk': col(OFF_K, SB_WIDTH),
        'wv': col(OFF_V, SB_WIDTH), 'wgate': col(OFF_GATE, N_BRANCH * D_MODEL),
        'q_norm': row(jnp.tile(p['q_norm'][l], SB_HEADS)), 'k_norm': row(jnp.tile(p['k_norm'][l], SB_HEADS)),
        'head_mean': ((jnp.arange(SB_WIDTH)[:, None] // SB_HEAD_DIM == jnp.arange(LANES)[None, :])
                      .astype(F32) / SB_HEAD_DIM).astype(BF16),
        'head_expand': (jnp.arange(LANES)[:, None] == jnp.arange(SB_WIDTH)[None, :] // SB_HEAD_DIM).astype(BF16),
        'conv_w': jnp.pad(p['conv_w'][l], ((0, SUBLANES - CONV_W), (0, 0))),
        'conv_b': row(p['conv_b'][l]),
        'dt_bias': pad_lanes(p['dt_bias'][l]), 'a_log': pad_lanes(p['a_log'][l]),
        'd_ssd': row(p['d_ssd'][l][heads]), 'norm_ssd': row(p['norm_ssd'][l]),
        'eye': jnp.eye(LANES, dtype=BF16),
        'state_mask': (state_rows[:, None] == state_cols[None, :]).astype(F32),
        's5_ab_re': ab_re[::S5_GROUP_CH].reshape(1, S5_LANES), 's5_ab_im': ab_im[::S5_GROUP_CH].reshape(1, S5_LANES),
        's5_bb_re': to_bb(bb_re), 's5_bb_im': to_bb(bb_im),
        's5_cc_re': to_cc(p['c_re'][l]), 's5_cc_im': to_cc(p['c_im'][l]),
        'd_s5': row(p['d_s5'][l]),
        'w_glu': p['w_glu'][l].astype(BF16), 'w_lift_a': p['w_lift_a'][l].astype(BF16),
        'w_lift_c': p['w_lift_c'][l].astype(BF16), 'w_out': p['w_out'][l].astype(BF16),
        'norm_ffn': row(p['norm_ffn'][l]), 'w_up': p['w_up'][l].astype(BF16), 'w_down': p['w_down'][l].astype(BF16),
    }


def _ssd_state_in(s):
    b = s.shape[0]
    t = jnp.transpose(s.astype(F32), (0, 1, 4, 2, 3)).reshape(b, SSD_GROUPS, SSD_STATE, SSD_HPG * SSD_HEAD_DIM)
    eye = jnp.eye(SSD_GROUPS, dtype=F32)
    return (t[:, :, :, None, :] * eye[None, :, None, :, None]).reshape(b, LANES, SSD_WIDTH)


def _ssd_state_out(s):
    b = s.shape[0]
    t = s.reshape(b, SSD_GROUPS, SSD_STATE, SSD_GROUPS, SSD_HPG, SSD_HEAD_DIM)
    t = jnp.stack([t[:, g, :, g] for g in range(SSD_GROUPS)], axis=1)
    return jnp.transpose(t, (0, 1, 3, 4, 2))


def _trunk_layer(x, k_hist, v_hist, conv_hist, ssd_s0, s5_re0, s5_im0, lw):
    b, t, _ = x.shape
    x2 = x.reshape(b * t, D_MODEL)
    z, xbc, dt, u, k, v, qb, kb, vb = _proj(x2, lw)
    seq = lambda a: a.reshape(b, t, a.shape[-1])
    hist8 = jnp.pad(conv_hist.astype(F32), ((0, 0), (SUBLANES - (CONV_W - 1), 0), (0, 0)))
    y_a, ssd_new = _ssd(seq(z), seq(xbc), seq(dt), hist8, _ssd_state_in(ssd_s0), lw)
    y_b, s5_re, s5_im = _s5(seq(u), s5_re0.reshape(-1, S5_LANES).astype(F32),
                            s5_im0.reshape(-1, S5_LANES).astype(F32), lw)
    if k_hist is None:
        kh = vh = None
    else:
        kh = k_hist.reshape(k_hist.shape[0], -1, SB_WIDTH).astype(F32)
        vh = v_hist.reshape(v_hist.shape[0], -1, SB_WIDTH).astype(F32)
    o_c = _attn(seq(qb), seq(kb), seq(vb), kh, vh)
    x2 = _merge(x2, y_a.reshape(b * t, SSD_WIDTH), y_b.reshape(b * t, S5_WIDTH), o_c.reshape(b * t, SB_WIDTH), lw)
    x2 = _ffn(x2, lw)
    conv_rows = jnp.concatenate([jnp.broadcast_to(conv_hist.astype(F32), (b, CONV_W - 1, CONV_DIM)),
                                 seq(xbc)[:, -(CONV_W - 1):]], axis=1)[:, -(CONV_W - 1):]
    return (x2.reshape(b, t, D_MODEL), k, v, conv_rows, _ssd_state_out(ssd_new),
            s5_re.reshape(b, S5_GROUPS, S5_STATE), s5_im.reshape(b, S5_GROUPS, S5_STATE))


def kernel(x_prompt, x_sample, cache_k, cache_v, state_conv, state_ssd, state_s5_re, state_s5_im, meta_tokens, norm_mix, w_in, conv_w, conv_b, dt_bias, a_log, d_ssd, norm_ssd, lam_re, lam_im, log_step, b_re, b_im, c_re, c_im, d_s5, w_glu, q_norm, k_norm, w_lift_a, w_lift_c, w_out, norm_ffn, w_up, w_down):
    p = dict(norm_mix=norm_mix, w_in=w_in, conv_w=conv_w, conv_b=conv_b, dt_bias=dt_bias, a_log=a_log,
             d_ssd=d_ssd, norm_ssd=norm_ssd, lam_re=lam_re, lam_im=lam_im, log_step=log_step, b_re=b_re,
             b_im=b_im, c_re=c_re, c_im=c_im, d_s5=d_s5, w_glu=w_glu, q_norm=q_norm, k_norm=k_norm,
             w_lift_a=w_lift_a, w_lift_c=w_lift_c, w_out=w_out, norm_ffn=norm_ffn, w_up=w_up, w_down=w_down)
    depth = w_in.shape[0]
    bp = x_prompt.shape[0]
    bs, ts, _ = x_sample.shape
    xm = meta_tokens.astype(x_prompt.dtype)[None]
    xp, xs = x_prompt, x_sample
    zeros = lambda *s: jnp.zeros(s, F32)
    kv_m, kv_p, kv_s, outs_p, outs_s = [], [], [], [], []
    for l in range(depth):
        lw = _layer_weights(p, l)
        xm, k_m, v_m, conv_m, ssd_m, s5re_m, s5im_m = _trunk_layer(
            xm, None, None, zeros(1, CONV_W - 1, CONV_DIM), zeros(1, SSD_GROUPS, SSD_HPG, SSD_HEAD_DIM, SSD_STATE),
            zeros(1, S5_GROUPS, S5_STATE), zeros(1, S5_GROUPS, S5_STATE), lw)
        xp, k_p, v_p, *st_p = _trunk_layer(xp, k_m[None], v_m[None], conv_m, ssd_m, s5re_m, s5im_m, lw)
        xs, k_s, v_s, *st_s = _trunk_layer(xs, cache_k[l], cache_v[l], state_conv[l], state_ssd[l],
                                           state_s5_re[l], state_s5_im[l], lw)
        kv_m.append((k_m, v_m))
        kv_p.append((k_p, v_p))
        kv_s.append((k_s, v_s))
        outs_p.append(st_p)
        outs_s.append(st_s)
    stk = lambda outs, i: jnp.stack([o[i] for o in outs], axis=0)
    k_prompt, v_prompt = _kv_rows(kv_m, kv_p, bp)
    sample_rows = lambda i: stk(kv_s, i).reshape(depth, bs, ts, SB_HEADS, SB_HEAD_DIM)
    return (xp, xs,
            k_prompt, v_prompt, stk(outs_p, 0), stk(outs_p, 1), stk(outs_p, 2), stk(outs_p, 3),
            sample_rows(0), sample_rows(1), stk(outs_s, 0), stk(outs_s, 1), stk(outs_s, 2), stk(outs_s, 3))
```

```python
import functools
import math

import jax
import jax.numpy as jnp
from jax import lax
from jax.experimental import pallas as pl
from jax.experimental.pallas import tpu as pltpu

F32 = jnp.float32
BF16 = jnp.bfloat16

D_MODEL = 1024
N_META = 16
RMS_EPS = 1e-6
SSD_HEADS = 16
SSD_HEAD_DIM = 64
SSD_GROUPS = 2
SSD_HPG = SSD_HEADS // SSD_GROUPS
SSD_STATE = 64
SSD_WIDTH = SSD_HEADS * SSD_HEAD_DIM
CONV_W = 4
N_BC = SSD_GROUPS * SSD_STATE
CONV_DIM = SSD_WIDTH + 2 * N_BC
S5_WIDTH = D_MODEL // 2
S5_GROUP_CH = 16
S5_GROUPS = S5_WIDTH // S5_GROUP_CH
S5_STATE = 64
S5_LANES = S5_GROUPS * S5_STATE
SB_HEADS = 8
SB_HEAD_DIM = 64
SB_WIDTH = SB_HEADS * SB_HEAD_DIM
SB_SCALE = 1.0 / math.sqrt(SB_HEAD_DIM)
N_BRANCH = 3
D_FF = 4 * D_MODEL
OFF_Z = 0
OFF_XBC = OFF_Z + SSD_WIDTH
OFF_DT = OFF_XBC + CONV_DIM
OFF_U = OFF_DT + SSD_HEADS
OFF_Q = OFF_U + S5_WIDTH
OFF_K = OFF_Q + SB_WIDTH
OFF_V = OFF_K + SB_WIDTH
OFF_GATE = OFF_V + SB_WIDTH

LANES = 128
SUBLANES = 8
VMEM_LIMIT = 56 * 1024 * 1024
S5_BLK_GROUPS = LANES // S5_GROUP_CH
S5_BLK_LANES = S5_BLK_GROUPS * S5_STATE
S5_NBLK = S5_GROUPS // S5_BLK_GROUPS


def _dot(a, b):
    return jnp.dot(a, b, preferred_element_type=F32)


def _dot_nt(a, b):
    return lax.dot_general(a, b, (((1,), (1,)), ((), ())), preferred_element_type=F32)


def _split(a, terms):
    out = []
    r = a
    for _ in range(terms):
        p = r.astype(BF16)
        out.append(p)
        r = r - p.astype(F32)
    return out


def _dot_split_lhs(a, b_bf16, terms=3):
    return sum(_dot(p, b_bf16) for p in _split(a, terms))


def _dot_split_rhs(a_bf16, b, terms=3):
    return sum(_dot(a_bf16, p) for p in _split(b, terms))


def _sigmoid(x):
    return 1.0 / (1.0 + jnp.exp(-x))


def _softplus(x):
    return jnp.maximum(x, 0.0) + jnp.log(1.0 + jnp.exp(-jnp.abs(x)))


def _rmsnorm_rows(x, g):
    return x * lax.rsqrt(jnp.mean(x * x, axis=-1, keepdims=True) + RMS_EPS) * g


def _params(*sem):
    return pltpu.CompilerParams(dimension_semantics=sem, vmem_limit_bytes=VMEM_LIMIT)


def _full(shape):
    n = len(shape)
    return pl.BlockSpec(shape, lambda *_: (0,) * n)


def _row_tile(n):
    for t in (256, 128, 64, 32, 16, 8):
        if n % t == 0:
            return t
    raise ValueError(f"token count {n} is not a multiple of {SUBLANES}")


def _proj_body(x_ref, g_ref, wz_ref, wxbc_ref, wdt_ref, wu_ref, wq_ref, wk_ref, wv_ref,
               qn_ref, kn_ref, hm_ref, cw_ref, cb_ref, hist_ref,
               z_ref, act_ref, tail_ref, dt_ref, u_ref, k_ref, v_ref, qb_ref, kb_ref, vb_ref, xp_ref, hb_ref,
               *, tm, tiles_per_seq):
    hb_ref[...] = _rmsnorm_rows(x_ref[...], g_ref[...]).astype(BF16)

    @pl.when(pl.program_id(0) % tiles_per_seq == 0)
    def _():
        xp_ref[0:SUBLANES, :] = hist_ref[0]

    xp_ref[SUBLANES:SUBLANES + tm, :] = _dot(hb_ref[...], wxbc_ref[...])
    z_ref[...] = _dot(hb_ref[...], wz_ref[...])
    dt_ref[...] = _dot(hb_ref[...], wdt_ref[...])
    u_ref[...] = _dot(hb_ref[...], wu_ref[...])
    hm = hm_ref[...]

    def head_norm(t, w):
        return t * lax.rsqrt(_dot_split_lhs(t * t, hm, 2) + RMS_EPS) * w

    q = head_norm(_dot(hb_ref[...], wq_ref[...]), qn_ref[...])
    k = head_norm(_dot(hb_ref[...], wk_ref[...]), kn_ref[...])
    v = _dot(hb_ref[...], wv_ref[...])
    k_ref[...] = k
    v_ref[...] = v
    qb_ref[...] = (q * SB_SCALE).astype(BF16)
    kb_ref[...] = k.astype(BF16)
    vb_ref[...] = v.astype(BF16)

    conv = cb_ref[...]
    for tap in range(CONV_W):
        lo = SUBLANES - (CONV_W - 1) + tap
        conv = conv + xp_ref[lo:lo + tm, :] * cw_ref[tap:tap + 1, :]
    tail = xp_ref[tm:tm + SUBLANES, :]
    xp_ref[0:SUBLANES, :] = tail
    tail_ref[0] = tail
    act_ref[...] = conv * _sigmoid(conv)


def _proj(x2, t, conv_hist, lw):
    n = x2.shape[0]
    b = n // t
    tm = min(_row_tile(n), t)
    tiles_per_seq = t // tm
    rows = lambda w: pl.BlockSpec((tm, w), lambda i: (i, 0))
    per_seq = (lambda i: (0, 0, 0)) if conv_hist.shape[0] == 1 else (lambda i: (i // tiles_per_seq, 0, 0))
    ins = (lw['norm_mix'], lw['wz'], lw['wxbc'], lw['wdt'], lw['wu'], lw['wq'], lw['wk'], lw['wv'],
           lw['q_norm'], lw['k_norm'], lw['head_mean'], lw['conv_w'], lw['conv_b'])
    sds = jax.ShapeDtypeStruct
    return pl.pallas_call(
        functools.partial(_proj_body, tm=tm, tiles_per_seq=tiles_per_seq),
        grid=(n // tm,),
        in_specs=[rows(D_MODEL)] + [_full(a.shape) for a in ins] + [pl.BlockSpec((1, SUBLANES, CONV_DIM), per_seq)],
        out_specs=[rows(SSD_WIDTH), rows(CONV_DIM),
                   pl.BlockSpec((1, SUBLANES, CONV_DIM), lambda i: (i // tiles_per_seq, 0, 0)),
                   rows(LANES), rows(S5_WIDTH), rows(SB_WIDTH), rows(SB_WIDTH),
                   rows(SB_WIDTH), rows(SB_WIDTH), rows(SB_WIDTH)],
        out_shape=[sds((n, SSD_WIDTH), F32), sds((n, CONV_DIM), F32), sds((b, SUBLANES, CONV_DIM), F32),
                   sds((n, LANES), F32), sds((n, S5_WIDTH), F32), sds((n, SB_WIDTH), F32), sds((n, SB_WIDTH), F32),
                   sds((n, SB_WIDTH), BF16), sds((n, SB_WIDTH), BF16), sds((n, SB_WIDTH), BF16)],
        scratch_shapes=[pltpu.VMEM((tm + SUBLANES, CONV_DIM), F32), pltpu.VMEM((tm, D_MODEL), BF16)],
        compiler_params=_params("arbitrary"),
        name="proj",
    )(x2, *ins, conv_hist)


def _ssd_body(z_ref, act_ref, dt_ref, s0_ref, dtb_ref, alog_ref,
              dskip_ref, nw_ref, eye_ref, smask_ref, tri_ref,
              y_ref, sout_ref, st_ref, yacc_ref, *, lc):
    c = pl.program_id(1)

    @pl.when(c == 0)
    def _():
        st_ref[...] = s0_ref[0]

    act = act_ref[0]
    xs = act[:, :SSD_WIDTH]
    bm = act[:, SSD_WIDTH:SSD_WIDTH + N_BC]
    cm = act[:, SSD_WIDTH + N_BC:]

    eye = eye_ref[...]
    dt = _softplus(dt_ref[0] + dtb_ref[...])
    a = -jnp.exp(alog_ref[...])
    ri = lax.broadcasted_iota(jnp.int32, (lc, lc), 0)
    ci = lax.broadcasted_iota(jnp.int32, (lc, lc), 1)
    causal = ri >= ci
    acum = _dot_split_rhs(tri_ref[...], dt * a)
    if lc == LANES:
        transpose = lambda t: t.T
    else:
        transpose = lambda t: sum(_dot_nt(eye, p) for p in _split(t, 3))
    acum_t = transpose(acum)
    log_dt_t = jnp.log(transpose(dt))
    a_last = acum[lc - 1:lc, :]
    shifted_t = acum_t - log_dt_t
    to_end_t = jnp.exp(acum_t[:, lc - 1:lc] - shifted_t)
    dec = jnp.exp(a_last)

    xs_b = xs.astype(BF16)
    bm_b = bm.astype(BF16)
    cm_b = cm.astype(BF16)
    bm_t = bm.T if lc == LANES else _dot_nt(eye, bm_b)
    lane = lax.broadcasted_iota(jnp.int32, (lc, LANES), 1)
    left = lane < SSD_STATE
    left_sq = lax.broadcasted_iota(jnp.int32, (LANES, LANES), 1) < SSD_HEAD_DIM
    zero_b = jnp.zeros((), BF16)
    cbs = [_dot_nt(jnp.where(left if g == 0 else ~left, cm_b, zero_b), bm_b) for g in range(SSD_GROUPS)]
    st = st_ref[...]
    st_b = st.astype(BF16)
    smask = smask_ref[...]

    for hp in range(SSD_HEADS // 2):
        pair = slice(hp * LANES, (hp + 1) * LANES)
        rhs = jnp.concatenate([xs_b[:, pair], st_b[:, pair]], axis=0)
        outs, news = [], []
        for h in (2 * hp, 2 * hp + 1):
            col = jnp.broadcast_to(acum[:, h:h + 1], (lc, LANES))
            seg = col[:, :lc] - shifted_t[h:h + 1, :]
            m = jnp.where(causal, cbs[h // SSD_HPG] * jnp.exp(seg), 0.0)
            read = cm * jnp.exp(col)
            outs.append(_dot(jnp.concatenate([m.astype(BF16), read.astype(BF16)], axis=1), rhs))
            news.append(_dot((bm_t * to_end_t[h:h + 1, :]).astype(BF16), xs_b[:, pair]))
        yacc_ref[:, pair] = jnp.where(left, outs[0], outs[1])
        decay = jnp.where(left_sq, dec[:, 2 * hp:2 * hp + 1], dec[:, 2 * hp + 1:2 * hp + 2])
        st_ref[:, pair] = st[:, pair] * decay + smask[:, pair] * jnp.where(left_sq, news[0], news[1])

    y = (yacc_ref[...] + dskip_ref[...] * xs)
    zz = z_ref[0]
    y = y * (zz * _sigmoid(zz))
    gw = SSD_WIDTH // SSD_GROUPS
    parts = []
    for g in range(SSD_GROUPS):
        yg = y[:, g * gw:(g + 1) * gw]
        parts.append(yg * lax.rsqrt(jnp.mean(yg * yg, axis=-1, keepdims=True) + RMS_EPS))
    y_ref[0] = (jnp.concatenate(parts, axis=1) * nw_ref[...]).astype(BF16)

    @pl.when(c == pl.num_programs(1) - 1)
    def _():
        sout_ref[0] = st_ref[...]


def _ssd(z3, act3, dt3, s0, lw):
    b, t, _ = z3.shape
    lc = min(t, 128)
    per_b = (lambda i, c: (0, 0, 0)) if s0.shape[0] == 1 else (lambda i, c: (i, 0, 0))
    seq = lambda w: pl.BlockSpec((1, lc, w), lambda i, c: (i, c, 0))
    consts = (lw['dt_bias'], lw['a_log'], lw['d_ssd'], lw['norm_ssd'],
              lw['eye'], lw['state_mask'], jnp.tril(jnp.ones((lc, lc), BF16)))
    return pl.pallas_call(
        functools.partial(_ssd_body, lc=lc),
        grid=(b, t // lc),
        in_specs=[seq(SSD_WIDTH), seq(CONV_DIM), seq(LANES),
                  pl.BlockSpec((1, LANES, SSD_WIDTH), per_b)] + [_full(a.shape) for a in consts],
        out_specs=[seq(SSD_WIDTH), pl.BlockSpec((1, LANES, SSD_WIDTH), lambda i, c: (i, 0, 0))],
        out_shape=[jax.ShapeDtypeStruct((b, t, SSD_WIDTH), BF16),
                   jax.ShapeDtypeStruct((b, LANES, SSD_WIDTH), F32)],
        scratch_shapes=[pltpu.VMEM((LANES, SSD_WIDTH), F32),
                        pltpu.VMEM((lc, SSD_WIDTH), F32)],
        compiler_params=_params("parallel", "arbitrary"),
        name="ssd",
    )(z3, act3, dt3, s0, *consts)


def _s5_param_body(lr_ref, li_ref, step_ref, br_ref, bi_ref, abr_ref, abi_ref, bbr_ref, bbi_ref):
    lr, li, step = lr_ref[...], li_ref[...], jnp.exp(step_ref[...])
    mag = jnp.exp(lr * step)
    ab_re = mag * jnp.cos(li * step)
    ab_im = mag * jnp.sin(li * step)
    den = lr * lr + li * li
    nr = ab_re - 1.0
    f_re = (nr * lr + ab_im * li) / den
    f_im = (ab_im * lr - nr * li) / den
    br, bi = br_ref[...], bi_ref[...]
    abr_ref[...] = ab_re
    abi_ref[...] = ab_im
    bbr_ref[...] = f_re * br - f_im * bi
    bbi_ref[...] = f_re * bi + f_im * br


def _s5_params(lam_re, lam_im, log_step, b_re, b_im):
    rep = lambda t: jnp.repeat(t, S5_GROUP_CH, axis=0)
    shape = (S5_WIDTH, S5_STATE)
    step = jnp.broadcast_to(rep(log_step[:, None]), shape)
    to_rows = lambda t: jnp.transpose(t, (0, 2, 1)).reshape(shape)
    outs = pl.pallas_call(
        _s5_param_body,
        out_shape=[jax.ShapeDtypeStruct(shape, F32)] * 4,
        name="s5_params",
    )(rep(lam_re), rep(lam_im), step, to_rows(b_re), to_rows(b_im))
    return outs


S5_STEPS = 16
S5_SEQS = 16
S5_SCAN_LANES = 256


def _s5_body(u_ref, sre_ref, sim_ref, abr_ref, abi_ref, bbr_ref, bbi_ref, ccr_ref, cci_ref, dsk_ref,
             perm_ref, perm_t_ref, y_ref, ore_ref, oim_ref, xre_ref, xim_ref, cr_ref, ci_ref, *, nb):
    c = pl.program_id(1)
    rows = nb * S5_STEPS

    @pl.when(c == 0)
    def _():
        cr_ref[...] = jnp.broadcast_to(sre_ref[...], cr_ref.shape)
        ci_ref[...] = jnp.broadcast_to(sim_ref[...], ci_ref.shape)

    u = u_ref[...].reshape(rows, S5_WIDTH)
    u_tm = _dot(perm_ref[...], u.astype(BF16)).astype(BF16)
    perm_t = perm_t_ref[...]
    for j in range(S5_NBLK):
        ub = u_tm[:, j * LANES:(j + 1) * LANES]
        xre_ref[j] = _dot(ub, bbr_ref[j])
        xim_ref[j] = _dot(ub, bbi_ref[j])
    for j in range(S5_NBLK):
        xre, xim = xre_ref.at[j], xim_ref.at[j]
        for h in range(S5_BLK_LANES // S5_SCAN_LANES):
            cols = slice(h * S5_SCAN_LANES, (h + 1) * S5_SCAN_LANES)
            lanes = slice(j * S5_BLK_LANES + h * S5_SCAN_LANES, j * S5_BLK_LANES + (h + 1) * S5_SCAN_LANES)
            ar, ai = abr_ref[:, lanes], abi_ref[:, lanes]
            cr, ci = cr_ref[:, lanes], ci_ref[:, lanes]
            for s in range(S5_STEPS):
                step = slice(s * nb, (s + 1) * nb)
                cr, ci = (ar * cr - ai * ci + xre[step, cols], ar * ci + ai * cr + xim[step, cols])
                xre[step, cols] = cr
                xim[step, cols] = ci
            cr_ref[:, lanes] = cr
            ci_ref[:, lanes] = ci
    for j in range(S5_NBLK):
        yx = _dot(xre_ref[j].astype(BF16), ccr_ref[j]) - _dot(xim_ref[j].astype(BF16), cci_ref[j])
        yb = _dot_split_rhs(perm_t, yx, 2) + dsk_ref[:, j * LANES:(j + 1) * LANES] * u[:, j * LANES:(j + 1) * LANES]
        gelu = 0.5 * yb * (1.0 + jnp.tanh(math.sqrt(2.0 / math.pi) * (yb + 0.044715 * (yb * yb * yb))))
        y_ref[:, :, j * LANES:(j + 1) * LANES] = gelu.astype(BF16).reshape(nb, S5_STEPS, LANES)

    @pl.when(c == pl.num_programs(1) - 1)
    def _():
        ore_ref[...] = cr_ref[...]
        oim_ref[...] = ci_ref[...]


def _s5(u3, s_re, s_im, lw):
    b, t, _ = u3.shape
    shared = s_re.shape[0] == 1
    bp = -(-b // SUBLANES) * SUBLANES
    if bp != b:
        u3 = jnp.pad(u3, ((0, bp - b), (0, 0), (0, 0)))
        if not shared:
            s_re, s_im = (jnp.pad(s, ((0, bp - b), (0, 0))) for s in (s_re, s_im))
    nb = min(bp, S5_SEQS)
    rows = nb * S5_STEPS
    r = jnp.arange(rows)
    perm = ((r % S5_STEPS) * nb + r // S5_STEPS)[None, :] == r[:, None]
    perm, perm_t = perm.astype(BF16), perm.T.astype(BF16)
    state = pl.BlockSpec((1, S5_LANES), lambda i, c: (0, 0)) if shared else pl.BlockSpec((nb, S5_LANES), lambda i, c: (i, 0))
    state_out = pl.BlockSpec((nb, S5_LANES), lambda i, c: (i, 0))
    seq = pl.BlockSpec((nb, S5_STEPS, S5_WIDTH), lambda i, c: (i, c, 0))
    consts = (lw['s5_ab_re'], lw['s5_ab_im'], lw['s5_bb_re'], lw['s5_bb_im'], lw['s5_cc_re'],
              lw['s5_cc_im'], lw['d_s5'], perm, perm_t)
    y, o_re, o_im = pl.pallas_call(
        functools.partial(_s5_body, nb=nb),
        grid=(bp // nb, t // S5_STEPS),
        in_specs=[seq, state, state] + [_full(a.shape) for a in consts],
        out_specs=[seq, state_out, state_out],
        out_shape=[jax.ShapeDtypeStruct((bp, t, S5_WIDTH), BF16),
                   jax.ShapeDtypeStruct((bp, S5_LANES), F32),
                   jax.ShapeDtypeStruct((bp, S5_LANES), F32)],
        scratch_shapes=[pltpu.VMEM((S5_NBLK, rows, S5_BLK_LANES), F32), pltpu.VMEM((S5_NBLK, rows, S5_BLK_LANES), F32),
                        pltpu.VMEM((nb, S5_LANES), F32), pltpu.VMEM((nb, S5_LANES), F32)],
        compiler_params=_params("parallel", "arbitrary"),
        name="s5",
    )(u3, s_re, s_im, *consts)
    return y[:b], o_re[:b], o_im[:b]


SB_KEYS = 256
SB_PAIRS = SB_WIDTH // LANES
SB_DEAD = -105.0


def _attn_body(*refs, bq, n_hist, pad):
    if n_hist:
        q_ref, upper_ref, kn_ref, vn_ref, kh_ref, vh_ref, o_ref, kall_ref, vall_ref, acc_ref, carry_ref = refs
    else:
        q_ref, upper_ref, kn_ref, vn_ref, o_ref, kall_ref, vall_ref, acc_ref, carry_ref = refs
    qi = pl.program_id(1)

    @pl.when(qi == 0)
    def _():
        for all_ref, new_ref, hist_ref in ((kall_ref, kn_ref, kh_ref if n_hist else None),
                                           (vall_ref, vn_ref, vh_ref if n_hist else None)):
            if pad:
                all_ref[0:pad, :] = jnp.zeros((pad, SB_WIDTH), BF16)
            if n_hist:
                all_ref[pad:pad + n_hist, :] = hist_ref[0].astype(BF16)
            all_ref[pad + n_hist:, :] = new_ref[0]

    k_ref, v_ref = kall_ref, vall_ref
    q = q_ref[0]
    lane = lax.broadcasted_iota(jnp.int32, (bq, LANES), 1)
    left = lane < SB_HEAD_DIM
    zero_b = jnp.zeros((), BF16)
    qms = [jnp.concatenate([jnp.where(left, q[:, p * LANES:(p + 1) * LANES], zero_b),
                            jnp.where(left, zero_b, q[:, p * LANES:(p + 1) * LANES])], axis=0)
           for p in range(SB_PAIRS)]
    ri = lax.broadcasted_iota(jnp.int32, (2 * bq, SB_KEYS), 0)
    ci = lax.broadcasted_iota(jnp.int32, (2 * bq, SB_KEYS), 1)
    ri = jnp.where(ri >= bq, ri - bq, ri)
    strictly_earlier = ci - (SB_KEYS - bq) < ri
    upper = upper_ref[...]
    n_before = n_hist + (qi + 1) * bq
    end = pad + n_before
    trips = (n_before + SB_KEYS - 1) // SB_KEYS
    acc_ref[...] = jnp.zeros_like(acc_ref)
    carry_ref[...] = jnp.zeros_like(carry_ref)

    def visit(it, masked):
        rows = pl.ds(pl.multiple_of(end - (it + 1) * SB_KEYS, 16), SB_KEYS)
        _sb_visit([_dot_nt(qms[p], k_ref[rows, p * LANES:(p + 1) * LANES]) for p in range(SB_PAIRS)],
                  lambda p, w: _dot(w, v_ref[rows, p * LANES:(p + 1) * LANES]),
                  strictly_earlier if masked else None, upper, acc_ref, carry_ref, left, bq)

    def any_weight_left():
        c = carry_ref[0]
        for p in range(1, SB_PAIRS):
            c = jnp.maximum(c, carry_ref[p])
        return jnp.max(c) > SB_DEAD

    visit(0, True)

    def trip(state):
        it, _ = state
        visit(it, False)
        return it + 1, any_weight_left()

    lax.while_loop(lambda s: jnp.logical_and(s[0] < trips, s[1]), trip, (jnp.int32(1), any_weight_left()))
    o_ref[0] = acc_ref[...].astype(BF16)


BF16_ROWS = 16


def _attn(qb, kb, vb, kh, vh):
    b, t, _ = qb.shape
    bq = min(t, 128)
    n_hist = 0 if kh is None else kh.shape[1]
    assert n_hist % BF16_ROWS == 0 and bq % BF16_ROWS == 0
    short = max(-(n_hist + (qi + 1) * bq) % SB_KEYS for qi in range(t // bq))
    pad = -(-short // BF16_ROWS) * BF16_ROWS
    block = pl.BlockSpec((1, bq, SB_WIDTH), lambda i, j: (i, j, 0))
    new = pl.BlockSpec((1, t, SB_WIDTH), lambda i, j: (i, 0, 0))
    upper = jnp.tril(jnp.ones((SB_KEYS, SB_KEYS), BF16), -1)
    args, specs = [qb, upper, kb, vb], [block, _full(upper.shape), new, new]
    if n_hist:
        hmap = (lambda i, j: (0, 0, 0)) if kh.shape[0] == 1 else (lambda i, j: (i, 0, 0))
        args += [kh, vh]
        specs += [pl.BlockSpec((1, n_hist, SB_WIDTH), hmap)] * 2
    rows = pad + n_hist + t
    return pl.pallas_call(
        functools.partial(_attn_body, bq=bq, n_hist=n_hist, pad=pad),
        grid=(b, t // bq),
        in_specs=specs,
        out_specs=block,
        out_shape=jax.ShapeDtypeStruct((b, t, SB_WIDTH), BF16),
        scratch_shapes=[pltpu.VMEM((rows, SB_WIDTH), BF16), pltpu.VMEM((rows, SB_WIDTH), BF16),
                        pltpu.VMEM((bq, SB_WIDTH), F32), pltpu.VMEM((SB_PAIRS, 2 * bq, 1), F32)],
        compiler_params=_params("parallel", "arbitrary"),
        name="attn",
    )(*args)


def _sb_visit(zs, values, mask, upper, acc_ref, carry_ref, left, bq):
    log_ws, totals = [], []
    for z in zs:
        sp = jnp.log(1.0 + jnp.exp(-jnp.abs(z)))
        log_beta = jnp.minimum(z, 0.0) - sp
        log_keep = log_beta - z
        if mask is not None:
            log_keep = jnp.where(mask, log_keep, 0.0)
        hi = log_keep.astype(BF16)
        lo = (log_keep - hi.astype(F32)).astype(BF16)
        both = _dot(jnp.concatenate([hi, lo], axis=0), upper)
        later = both[:2 * bq] + both[2 * bq:]
        log_ws.append(log_beta + later)
        totals.append(later[:, 0:1] + log_keep[:, 0:1])
    for p in range(SB_PAIRS):
        carry = carry_ref[p]
        w = jnp.exp(log_ws[p] + carry)
        if mask is not None:
            w = jnp.where(mask, w, 0.0)
        pv = values(p, w.astype(BF16))
        carry_ref[p] = carry + totals[p]
        acc_ref[:, p * LANES:(p + 1) * LANES] += jnp.where(left, pv[:bq], pv[bq:])


def _attn_cached_body(q_ref, upper_ref, upper_new_ref, kn_ref, vn_ref, kh_ref, vh_ref, o_ref, acc_ref, carry_ref, *, bq, n_hist):
    q = q_ref[0]
    lane = lax.broadcasted_iota(jnp.int32, (bq, LANES), 1)
    left = lane < SB_HEAD_DIM
    zero_b = jnp.zeros((), BF16)
    qms = [jnp.concatenate([jnp.where(left, q[:, p * LANES:(p + 1) * LANES], zero_b),
                            jnp.where(left, zero_b, q[:, p * LANES:(p + 1) * LANES])], axis=0)
           for p in range(SB_PAIRS)]
    ri = lax.broadcasted_iota(jnp.int32, (2 * bq, bq), 0)
    ci = lax.broadcasted_iota(jnp.int32, (2 * bq, bq), 1)
    strictly_earlier = ci < jnp.where(ri >= bq, ri - bq, ri)
    upper = upper_ref[...]
    acc_ref[...] = jnp.zeros_like(acc_ref)
    carry_ref[...] = jnp.zeros_like(carry_ref)
    pair = lambda p: slice(p * LANES, (p + 1) * LANES)

    _sb_visit([_dot_nt(qms[p], kn_ref[0, :, pair(p)]) for p in range(SB_PAIRS)],
              lambda p, w: _dot(w, vn_ref[0, :, pair(p)]),
              strictly_earlier, upper_new_ref[...], acc_ref, carry_ref, left, bq)

    def any_weight_left():
        c = carry_ref[0]
        for p in range(1, SB_PAIRS):
            c = jnp.maximum(c, carry_ref[p])
        return jnp.max(c) > SB_DEAD

    def trip(state):
        it, _ = state
        window = pl.ds(pl.multiple_of(n_hist - (it + 1) * SB_KEYS, SB_KEYS), SB_KEYS)
        kt = kh_ref[0, 0, :, :, window].reshape(SB_WIDTH, SB_KEYS).astype(BF16)
        vt = vh_ref[0, 0, :, :, window].reshape(SB_WIDTH, SB_KEYS).astype(BF16)
        _sb_visit([_dot(qms[p], kt[pair(p)]) for p in range(SB_PAIRS)],
                  lambda p, w: _dot_nt(w, vt[pair(p)]),
                  None, upper, acc_ref, carry_ref, left, bq)
        return it + 1, any_weight_left()

    lax.while_loop(lambda s: jnp.logical_and(s[0] < n_hist // SB_KEYS, s[1]), trip, (jnp.int32(0), any_weight_left()))
    o_ref[0] = acc_ref[...].astype(BF16)


def _attn_cached(qb, kb, vb, cache_k, cache_v, layer):
    b, t, _ = qb.shape
    n_hist = cache_k.shape[2]
    assert t <= 128 and t % BF16_ROWS == 0 and n_hist % SB_KEYS == 0
    upper = jnp.tril(jnp.ones((SB_KEYS, SB_KEYS), BF16), -1)
    upper_new = jnp.tril(jnp.ones((t, t), BF16), -1)
    block = pl.BlockSpec((1, t, SB_WIDTH), lambda i: (i, 0, 0))
    hist =pl.BlockSpec((1, 1, SB_HEADS, SB_HEAD_DIM, n_hist), lambda i: (layer, i, 0, 0, 0))
    minor = lambda c: jnp.transpose(c, (0, 1, 3, 4, 2))
    return pl.pallas_call(
        functools.partial(_attn_cached_body, bq=t, n_hist=n_hist),
        grid=(b,),
        in_specs=[block, _full(upper.shape), _full(upper_new.shape), block, block, hist, hist],
        out_specs=block,
        out_shape=jax.ShapeDtypeStruct((b, t, SB_WIDTH), BF16),
        scratch_shapes=[pltpu.VMEM((t, SB_WIDTH), F32), pltpu.VMEM((SB_PAIRS, 2 * t, 1), F32)],
        compiler_params=_params("parallel"),
        name="attn_cached",
    )(qb, upper, upper_new, kb, vb, minor(cache_k), minor(cache_v))


def _kv_rows_body(*refs, b, t, depth):
    ko_ref, vo_ref = refs[4 * depth:]
    layer = pl.program_id(0) // b
    for l in range(depth):
        km_ref, vm_ref, kn_ref, vn_ref = refs[4 * l:4 * l + 4]
        n_meta = km_ref.shape[0]
        total = n_meta + t

        @pl.when(layer == l)
        def _(km_ref=km_ref, vm_ref=vm_ref, kn_ref=kn_ref, vn_ref=vn_ref, n_meta=n_meta, total=total):
            for meta_ref, new_ref, out_ref in ((km_ref, kn_ref, ko_ref), (vm_ref, vn_ref, vo_ref)):
                for c in range(-(-total // LANES)):
                    lo, hi = c * LANES, min((c + 1) * LANES, total)
                    parts = []
                    if lo < n_meta:
                        parts.append(meta_ref[lo:min(hi, n_meta), :])
                    if hi > n_meta:
                        parts.append(new_ref[max(lo, n_meta) - n_meta:hi - n_meta, :])
                    if hi - lo < LANES:
                        parts.append(jnp.zeros((LANES - (hi - lo), SB_WIDTH), F32))
                    blk = parts[0] if len(parts) == 1 else jnp.concatenate(parts, axis=0)
                    blk_t = blk.T.reshape(SB_HEADS, SB_HEAD_DIM, LANES)
                    out_ref[0, 0, :, :, lo:hi] = blk_t[:, :, :hi - lo]


def _kv_rows(meta_rows, new_rows, b):
    depth = len(new_rows)
    n = new_rows[0][0].shape[0]
    t = n // b
    n_meta = meta_rows[0][0].shape[0]
    args, specs = [], []
    for l in range(depth):
        rows = pl.BlockSpec((t, SB_WIDTH), lambda i, l=l: (jnp.clip(i - l * b, 0, b - 1), 0))
        args += [*meta_rows[l], *new_rows[l]]
        specs += [_full((n_meta, SB_WIDTH))] * 2 + [rows] * 2
    out = pl.BlockSpec((1, 1, SB_HEADS, SB_HEAD_DIM, n_meta + t), lambda i: (i // b, i % b, 0, 0, 0))
    shape = jax.ShapeDtypeStruct((depth, b, SB_HEADS, SB_HEAD_DIM, n_meta + t), F32)
    k_out, v_out = pl.pallas_call(
        functools.partial(_kv_rows_body, b=b, t=t, depth=depth),
        grid=(depth * b,),
        in_specs=specs,
        out_specs=[out, out],
        out_shape=[shape, shape],
        compiler_params=_params("arbitrary"),
        name="kv_rows",
    )(*args)
    return jnp.transpose(k_out, (0, 1, 4, 2, 3)), jnp.transpose(v_out, (0, 1, 4, 2, 3))


def _merge_body(x_ref, ya_ref, yb_ref, oc_ref, g_ref, wg_ref, wa_ref, wglu_ref, wc_ref, wo_ref, o_ref):
    x = x_ref[...]
    hb = _rmsnorm_rows(x, g_ref[...]).astype(BF16)
    gates = _sigmoid(_dot(hb, wg_ref[...]))
    glu = _dot(yb_ref[...], wglu_ref[...])
    mix = (gates[:, :D_MODEL] * _dot(ya_ref[...], wa_ref[...])
           + gates[:, D_MODEL:2 * D_MODEL] * (glu[:, :D_MODEL] * _sigmoid(glu[:, D_MODEL:]))
           + gates[:, 2 * D_MODEL:] * _dot(oc_ref[...], wc_ref[...]))
    o_ref[...] = x + _dot(mix.astype(BF16), wo_ref[...])


def _merge(x2, ya, yb, oc, lw):
    n = x2.shape[0]
    tm = _row_tile(n)
    rows = lambda w: pl.BlockSpec((tm, w), lambda i: (i, 0))
    consts = (lw['norm_mix'], lw['wgate'], lw['w_lift_a'], lw['w_glu'], lw['w_lift_c'], lw['w_out'])
    return pl.pallas_call(
        _merge_body,
        grid=(n // tm,),
        in_specs=[rows(D_MODEL), rows(SSD_WIDTH), rows(S5_WIDTH), rows(SB_WIDTH)]
                 + [_full(a.shape) for a in consts],
        out_specs=rows(D_MODEL),
        out_shape=jax.ShapeDtypeStruct((n, D_MODEL), F32),
        compiler_params=_params("parallel"),
        name="merge",
    )(x2, ya, yb, oc, *consts)


FF_CHUNK = 1024


def _ffn_body(x_ref, g_ref, wu_ref, wd_ref, o_ref):
    x = x_ref[...]
    hb = _rmsnorm_rows(x, g_ref[...]).astype(BF16)
    acc = x
    for j in range(D_FF // FF_CHUNK):
        cols = slice(j * FF_CHUNK, (j + 1) * FF_CHUNK)
        up = jnp.maximum(_dot(hb, wu_ref[:, cols]), 0.0)
        acc = acc + _dot((up * up).astype(BF16), wd_ref[cols, :])
    o_ref[...] = acc


def _ffn(x2, lw):
    n = x2.shape[0]
    tm = _row_tile(n)
    rows = pl.BlockSpec((tm, D_MODEL), lambda i: (i, 0))
    consts = (lw['norm_ffn'], lw['w_up'], lw['w_down'])
    return pl.pallas_call(
        _ffn_body,
        grid=(n // tm,),
        in_specs=[rows] + [_full(a.shape) for a in consts],
        out_specs=rows,
        out_shape=jax.ShapeDtypeStruct((n, D_MODEL), F32),
        compiler_params=_params("parallel"),
        name="ffn",
    )(x2, *consts)


def _block_diag(blocks):
    n, r, c = blocks.shape
    eye = jnp.eye(n, dtype=blocks.dtype)
    return (eye[:, None, :, None] * blocks[:, :, None, :]).reshape(n * r, n * c)


def _layer_weights(p, l):
    w_in = p['w_in'][l]
    col = lambda off, n: w_in[:, off:off + n].astype(BF16)
    row = lambda v: v.reshape(1, -1).astype(F32)
    heads = jnp.arange(SSD_WIDTH) // SSD_HEAD_DIM
    pad_lanes = lambda v: jnp.pad(v, (0, LANES - v.shape[0])).reshape(1, LANES)
    ab_re, ab_im, bb_re, bb_im = _s5_params(p['lam_re'][l], p['lam_im'][l], p['log_step'][l],
                                            p['b_re'][l], p['b_im'][l])
    to_bb = lambda t: jnp.stack([_block_diag(blk) for blk in
                                 t.reshape(S5_NBLK, S5_BLK_GROUPS, S5_GROUP_CH, S5_STATE)]).astype(BF16)
    to_cc = lambda t: jnp.stack([_block_diag(blk) for blk in
                                 jnp.transpose(t, (0, 2, 1)).reshape(S5_NBLK, S5_BLK_GROUPS, S5_STATE, S5_GROUP_CH)]
                                ).astype(BF16)
    state_rows = jnp.arange(LANES) // SSD_STATE
    state_cols = jnp.arange(SSD_WIDTH) // (SSD_WIDTH // SSD_GROUPS)
    return {
        'norm_mix': row(p['norm_mix'][l]),
        'wz': col(OFF_Z, SSD_WIDTH), 'wxbc': col(OFF_XBC, CONV_DIM),
        'wdt': jnp.pad(col(OFF_DT, SSD_HEADS), ((0, 0), (0, LANES - SSD_HEADS))),
        'wu': col(OFF_U, S5_WIDTH), 'wq': col(OFF_Q, SB_WIDTH), 'wk': col(OFF_K, SB_WIDTH),
        'wv': col(OFF_V, SB_WIDTH), 'wgate': col(OFF_GATE, N_BRANCH * D_MODEL),
        'q_norm': row(jnp.tile(p['q_norm'][l], SB_HEADS)), 'k_norm': row(jnp.tile(p['k_norm'][l], SB_HEADS)),
        'head_mean': (_block_diag(jnp.ones((SB_HEADS, SB_HEAD_DIM, SB_HEAD_DIM), F32)) / SB_HEAD_DIM).astype(BF16),
        'conv_w': jnp.pad(p['conv_w'][l], ((0, SUBLANES - CONV_W), (0, 0))),
        'conv_b': row(p['conv_b'][l]),
        'dt_bias': pad_lanes(p['dt_bias'][l]), 'a_log': pad_lanes(p['a_log'][l]),
        'd_ssd': row(p['d_ssd'][l][heads]), 'norm_ssd': row(p['norm_ssd'][l]),
        'eye': jnp.eye(LANES, dtype=BF16),
        'state_mask': (state_rows[:, None] == state_cols[None, :]).astype(F32),
        's5_ab_re': ab_re[::S5_GROUP_CH].reshape(1, S5_LANES), 's5_ab_im': ab_im[::S5_GROUP_CH].reshape(1, S5_LANES),
        's5_bb_re': to_bb(bb_re), 's5_bb_im': to_bb(bb_im),
        's5_cc_re': to_cc(p['c_re'][l]), 's5_cc_im': to_cc(p['c_im'][l]),
        'd_s5': row(p['d_s5'][l]),
        'w_glu': p['w_glu'][l].astype(BF16), 'w_lift_a': p['w_lift_a'][l].astype(BF16),
        'w_lift_c': p['w_lift_c'][l].astype(BF16), 'w_out': p['w_out'][l].astype(BF16),
        'norm_ffn': row(p['norm_ffn'][l]), 'w_up': p['w_up'][l].astype(BF16), 'w_down': p['w_down'][l].astype(BF16),
    }


def _ssd_state_in(s):
    b = s.shape[0]
    t = jnp.transpose(s.astype(F32), (0, 1, 4, 2, 3)).reshape(b, SSD_GROUPS, SSD_STATE, SSD_HPG * SSD_HEAD_DIM)
    eye = jnp.eye(SSD_GROUPS, dtype=F32)
    return (t[:, :, :, None, :] * eye[None, :, None, :, None]).reshape(b, LANES, SSD_WIDTH)


def _ssd_state_out(s):
    b = s.shape[0]
    t = s.reshape(b, SSD_GROUPS, SSD_STATE, SSD_GROUPS, SSD_HPG, SSD_HEAD_DIM)
    t = jnp.stack([t[:, g, :, g] for g in range(SSD_GROUPS)], axis=1)
    return jnp.transpose(t, (0, 1, 3, 4, 2))


def _trunk_layer(x, k_hist, v_hist, conv_hist, ssd_s0, s5_re0, s5_im0, lw, cache_layer=None):
    b, t, _ = x.shape
    x2 = x.reshape(b * t, D_MODEL)
    assert t >= SUBLANES
    hist8 = jnp.pad(conv_hist.astype(F32), ((0, 0), (SUBLANES - (CONV_W - 1), 0), (0, 0)))
    z, act, tail, dt, u, k, v, qb, kb, vb = _proj(x2, t, hist8, lw)
    seq = lambda a: a.reshape(b, t, a.shape[-1])
    y_a, ssd_new = _ssd(seq(z), seq(act), seq(dt), _ssd_state_in(ssd_s0), lw)
    y_b, s5_re, s5_im = _s5(seq(u), s5_re0.reshape(-1, S5_LANES).astype(F32),
                            s5_im0.reshape(-1, S5_LANES).astype(F32), lw)
    if cache_layer is not None:
        o_c = _attn_cached(seq(qb), seq(kb), seq(vb), k_hist.astype(F32), v_hist.astype(F32), cache_layer)
    elif k_hist is None:
        o_c = _attn(seq(qb), seq(kb), seq(vb), None, None)
    else:
        o_c = _attn(seq(qb), seq(kb), seq(vb), k_hist.reshape(k_hist.shape[0], -1, SB_WIDTH).astype(F32),
                    v_hist.reshape(v_hist.shape[0], -1, SB_WIDTH).astype(F32))
    x2 = _merge(x2, y_a.reshape(b * t, SSD_WIDTH), y_b.reshape(b * t, S5_WIDTH), o_c.reshape(b * t, SB_WIDTH), lw)
    x2 = _ffn(x2, lw)
    conv_rows = tail[:, SUBLANES - (CONV_W - 1):]
    return (x2.reshape(b, t, D_MODEL), k, v, conv_rows, _ssd_state_out(ssd_new),
            s5_re.reshape(b, S5_GROUPS, S5_STATE), s5_im.reshape(b, S5_GROUPS, S5_STATE))


def kernel(x_prompt, x_sample, cache_k, cache_v, state_conv, state_ssd, state_s5_re, state_s5_im, meta_tokens, norm_mix, w_in, conv_w, conv_b, dt_bias, a_log, d_ssd, norm_ssd, lam_re, lam_im, log_step, b_re, b_im, c_re, c_im, d_s5, w_glu, q_norm, k_norm, w_lift_a, w_lift_c, w_out, norm_ffn, w_up, w_down):
    p = dict(norm_mix=norm_mix, w_in=w_in, conv_w=conv_w, conv_b=conv_b, dt_bias=dt_bias, a_log=a_log,
             d_ssd=d_ssd, norm_ssd=norm_ssd, lam_re=lam_re, lam_im=lam_im, log_step=log_step, b_re=b_re,
             b_im=b_im, c_re=c_re, c_im=c_im, d_s5=d_s5, w_glu=w_glu, q_norm=q_norm, k_norm=k_norm,
             w_lift_a=w_lift_a, w_lift_c=w_lift_c, w_out=w_out, norm_ffn=norm_ffn, w_up=w_up, w_down=w_down)
    depth = w_in.shape[0]
    bp = x_prompt.shape[0]
    bs, ts, _ = x_sample.shape
    xm = meta_tokens.astype(x_prompt.dtype)[None]
    xp, xs = x_prompt, x_sample
    zeros = lambda *s: jnp.zeros(s, F32)
    kv_m, kv_p, kv_s, outs_p, outs_s = [], [], [], [], []
    for l in range(depth):
        lw = _layer_weights(p, l)
        xm, k_m, v_m, conv_m, ssd_m, s5re_m, s5im_m = _trunk_layer(
            xm, None, None, zeros(1, CONV_W - 1, CONV_DIM), zeros(1, SSD_GROUPS, SSD_HPG, SSD_HEAD_DIM, SSD_STATE),
            zeros(1, S5_GROUPS, S5_STATE), zeros(1, S5_GROUPS, S5_STATE), lw)
        xp, k_p, v_p, *st_p = _trunk_layer(xp, k_m[None], v_m[None], conv_m, ssd_m, s5re_m, s5im_m, lw)
        xs, k_s, v_s, *st_s = _trunk_layer(xs, cache_k, cache_v, state_conv[l], state_ssd[l],
                                           state_s5_re[l], state_s5_im[l], lw, cache_layer=l)
        kv_m.append((k_m, v_m))
        kv_p.append((k_p, v_p))
        kv_s.append((k_s, v_s))
        outs_p.append(st_p)
        outs_s.append(st_s)
    stk = lambda outs, i: jnp.stack([o[i] for o in outs], axis=0)
    k_prompt, v_prompt = _kv_rows(kv_m, kv_p, bp)
    sample_rows = lambda i: stk(kv_s, i).reshape(depth, bs, ts, SB_HEADS, SB_HEAD_DIM)
    return (xp, xs,
            k_prompt, v_prompt, stk(outs_p, 0), stk(outs_p, 1), stk(outs_p, 2), stk(outs_p, 3),
            sample_rows(0), sample_rows(1), stk(outs_s, 0), stk(outs_s, 1), stk(outs_s, 2), stk(outs_s, 3))
```

```python
import functools
import math

import jax
import jax.numpy as jnp
from jax import lax
from jax.experimental import pallas as pl
from jax.experimental.pallas import tpu as pltpu

F32 = jnp.float32
BF16 = jnp.bfloat16

D_MODEL = 1024
N_META = 16
RMS_EPS = 1e-6
SSD_HEADS = 16
SSD_HEAD_DIM = 64
SSD_GROUPS = 2
SSD_HPG = SSD_HEADS // SSD_GROUPS
SSD_STATE = 64
SSD_WIDTH = SSD_HEADS * SSD_HEAD_DIM
CONV_W = 4
N_BC = SSD_GROUPS * SSD_STATE
CONV_DIM = SSD_WIDTH + 2 * N_BC
S5_WIDTH = D_MODEL // 2
S5_GROUP_CH = 16
S5_GROUPS = S5_WIDTH // S5_GROUP_CH
S5_STATE = 64
S5_LANES = S5_GROUPS * S5_STATE
SB_HEADS = 8
SB_HEAD_DIM = 64
SB_WIDTH = SB_HEADS * SB_HEAD_DIM
SB_SCALE = 1.0 / math.sqrt(SB_HEAD_DIM)
N_BRANCH = 3
D_FF = 4 * D_MODEL
OFF_Z = 0
OFF_XBC = OFF_Z + SSD_WIDTH
OFF_DT = OFF_XBC + CONV_DIM
OFF_U = OFF_DT + SSD_HEADS
OFF_Q = OFF_U + S5_WIDTH
OFF_K = OFF_Q + SB_WIDTH
OFF_V = OFF_K + SB_WIDTH
OFF_GATE = OFF_V + SB_WIDTH

LANES = 128
SUBLANES = 8
VMEM_LIMIT = 56 * 1024 * 1024
S5_BLK_GROUPS = LANES // S5_GROUP_CH
S5_BLK_LANES = S5_BLK_GROUPS * S5_STATE
S5_NBLK = S5_GROUPS // S5_BLK_GROUPS


def _dot(a, b):
    return jnp.dot(a, b, preferred_element_type=F32)


def _dot_nt(a, b):
    return lax.dot_general(a, b, (((1,), (1,)), ((), ())), preferred_element_type=F32)


def _split(a, terms):
    out = []
    r = a
    for _ in range(terms):
        p = r.astype(BF16)
        out.append(p)
        r = r - p.astype(F32)
    return out


def _dot_split_lhs(a, b_bf16, terms=3):
    return sum(_dot(p, b_bf16) for p in _split(a, terms))


def _dot_split_rhs(a_bf16, b, terms=3):
    return sum(_dot(a_bf16, p) for p in _split(b, terms))


def _sigmoid(x):
    return 1.0 / (1.0 + jnp.exp(-x))


def _softplus(x):
    return jnp.maximum(x, 0.0) + jnp.log(1.0 + jnp.exp(-jnp.abs(x)))


def _rmsnorm_rows(x, g):
    return x * lax.rsqrt(jnp.mean(x * x, axis=-1, keepdims=True) + RMS_EPS) * g


def _params(*sem):
    return pltpu.CompilerParams(dimension_semantics=sem, vmem_limit_bytes=VMEM_LIMIT)


def _full(shape):
    n = len(shape)
    return pl.BlockSpec(shape, lambda *_: (0,) * n)


def _row_tile(n):
    for t in (256, 128, 64, 32, 16, 8):
        if n % t == 0:
            return t
    raise ValueError(f"token count {n} is not a multiple of {SUBLANES}")


def _proj_body(x_ref, g_ref, wz_ref, wxbc_ref, wdt_ref, wu_ref, wq_ref, wk_ref, wv_ref,
               qn_ref, kn_ref, hm_ref,
               z_ref, xbc_ref, dt_ref, u_ref, k_ref, v_ref, qb_ref, kb_ref, vb_ref, hb_ref):
    hb_ref[...] = _rmsnorm_rows(x_ref[...], g_ref[...]).astype(BF16)
    z_ref[...] = _dot(hb_ref[...], wz_ref[...])
    xbc_ref[...] = _dot(hb_ref[...], wxbc_ref[...])
    dt_ref[...] = _dot(hb_ref[...], wdt_ref[...])
    u_ref[...] = _dot(hb_ref[...], wu_ref[...])
    hm = hm_ref[...]

    def head_norm(t, w):
        return t * lax.rsqrt(_dot_split_lhs(t * t, hm, 2) + RMS_EPS) * w

    q = head_norm(_dot(hb_ref[...], wq_ref[...]), qn_ref[...])
    k = head_norm(_dot(hb_ref[...], wk_ref[...]), kn_ref[...])
    v = _dot(hb_ref[...], wv_ref[...])
    k_ref[...] = k
    v_ref[...] = v
    qb_ref[...] = (q * SB_SCALE).astype(BF16)
    kb_ref[...] = k.astype(BF16)
    vb_ref[...] = v.astype(BF16)


def _proj(x2, lw):
    n = x2.shape[0]
    tm = _row_tile(n)
    rows = lambda w: pl.BlockSpec((tm, w), lambda i: (i, 0))
    widths = (SSD_WIDTH, CONV_DIM, LANES, S5_WIDTH, SB_WIDTH, SB_WIDTH, SB_WIDTH, SB_WIDTH, SB_WIDTH)
    dtypes = (F32, F32, F32, F32, F32, F32, BF16, BF16, BF16)
    ins = (lw['norm_mix'], lw['wz'], lw['wxbc'], lw['wdt'], lw['wu'], lw['wq'], lw['wk'], lw['wv'],
           lw['q_norm'], lw['k_norm'], lw['head_mean'])
    return pl.pallas_call(
        _proj_body,
        grid=(n // tm,),
        in_specs=[rows(D_MODEL)] + [_full(a.shape) for a in ins],
        out_specs=[rows(w) for w in widths],
        out_shape=[jax.ShapeDtypeStruct((n, w), d) for w, d in zip(widths, dtypes)],
        scratch_shapes=[pltpu.VMEM((tm, D_MODEL), BF16)],
        compiler_params=_params("parallel"),
        name="proj",
    )(x2, *ins)


def _ssd_body(z_ref, xbc_ref, dt_ref, hist_ref, s0_ref, cw_ref, cb_ref, dtb_ref, alog_ref,
              dskip_ref, nw_ref, eye_ref, smask_ref, tri_ref,
              y_ref, sout_ref, xp_ref, st_ref, yacc_ref, *, lc):
    c = pl.program_id(1)

    @pl.when(c == 0)
    def _():
        xp_ref[0:SUBLANES, :] = hist_ref[0]
        st_ref[...] = s0_ref[0]

    xp_ref[SUBLANES:SUBLANES + lc, :] = xbc_ref[0]
    conv = cb_ref[...]
    for k in range(CONV_W):
        lo = SUBLANES - (CONV_W - 1) + k
        conv = conv + xp_ref[lo:lo + lc, :] * cw_ref[k:k + 1, :]
    xp_ref[0:SUBLANES, :] = xp_ref[lc:lc + SUBLANES, :]
    act = conv * _sigmoid(conv)
    xs = act[:, :SSD_WIDTH]
    bm = act[:, SSD_WIDTH:SSD_WIDTH + N_BC]
    cm = act[:, SSD_WIDTH + N_BC:]

    eye = eye_ref[...]
    dt = _softplus(dt_ref[0] + dtb_ref[...])
    a = -jnp.exp(alog_ref[...])
    ri = lax.broadcasted_iota(jnp.int32, (lc, lc), 0)
    ci = lax.broadcasted_iota(jnp.int32, (lc, lc), 1)
    causal = ri >= ci
    acum = _dot_split_rhs(tri_ref[...], dt * a)
    if lc == LANES:
        transpose = lambda t: t.T
    else:
        transpose = lambda t: sum(_dot_nt(eye, p) for p in _split(t, 3))
    acum_t = transpose(acum)
    log_dt_t = jnp.log(transpose(dt))
    a_last = acum[lc - 1:lc, :]
    shifted_t = acum_t - log_dt_t
    to_end_t = jnp.exp(acum_t[:, lc - 1:lc] - shifted_t)
    dec = jnp.exp(a_last)

    xs_b = xs.astype(BF16)
    bm_b = bm.astype(BF16)
    cm_b = cm.astype(BF16)
    bm_t = bm.T if lc == LANES else _dot_nt(eye, bm_b)
    lane = lax.broadcasted_iota(jnp.int32, (lc, LANES), 1)
    left = lane < SSD_STATE
    left_sq = lax.broadcasted_iota(jnp.int32, (LANES, LANES), 1) < SSD_HEAD_DIM
    zero_b = jnp.zeros((), BF16)
    cbs = [_dot_nt(jnp.where(left if g == 0 else ~left, cm_b, zero_b), bm_b) for g in range(SSD_GROUPS)]
    st = st_ref[...]
    st_b = st.astype(BF16)
    smask = smask_ref[...]

    for hp in range(SSD_HEADS // 2):
        pair = slice(hp * LANES, (hp + 1) * LANES)
        rhs = jnp.concatenate([xs_b[:, pair], st_b[:, pair]], axis=0)
        outs, news = [], []
        for h in (2 * hp, 2 * hp + 1):
            col = jnp.broadcast_to(acum[:, h:h + 1], (lc, LANES))
            seg = col[:, :lc] - shifted_t[h:h + 1, :]
            m = jnp.where(causal, cbs[h // SSD_HPG] * jnp.exp(seg), 0.0)
            read = cm * jnp.exp(col)
            outs.append(_dot(jnp.concatenate([m.astype(BF16), read.astype(BF16)], axis=1), rhs))
            news.append(_dot((bm_t * to_end_t[h:h + 1, :]).astype(BF16), xs_b[:, pair]))
        yacc_ref[:, pair] = jnp.where(left, outs[0], outs[1])
        decay = jnp.where(left_sq, dec[:, 2 * hp:2 * hp + 1], dec[:, 2 * hp + 1:2 * hp + 2])
        st_ref[:, pair] = st[:, pair] * decay + smask[:, pair] * jnp.where(left_sq, news[0], news[1])

    y = (yacc_ref[...] + dskip_ref[...] * xs)
    zz = z_ref[0]
    y = y * (zz * _sigmoid(zz))
    gw = SSD_WIDTH // SSD_GROUPS
    parts = []
    for g in range(SSD_GROUPS):
        yg = y[:, g * gw:(g + 1) * gw]
        parts.append(yg * lax.rsqrt(jnp.mean(yg * yg, axis=-1, keepdims=True) + RMS_EPS))
    y_ref[0] = (jnp.concatenate(parts, axis=1) * nw_ref[...]).astype(BF16)

    @pl.when(c == pl.num_programs(1) - 1)
    def _():
        sout_ref[0] = st_ref[...]


def _ssd(z3, xbc3, dt3, hist, s0, lw):
    b, t, _ = z3.shape
    lc = min(t, 128)
    per_b = (lambda i, c: (0, 0, 0)) if hist.shape[0] == 1 else (lambda i, c: (i, 0, 0))
    seq = lambda w: pl.BlockSpec((1, lc, w), lambda i, c: (i, c, 0))
    consts = (lw['conv_w'], lw['conv_b'], lw['dt_bias'], lw['a_log'], lw['d_ssd'], lw['norm_ssd'],
              lw['eye'], lw['state_mask'], jnp.tril(jnp.ones((lc, lc), BF16)))
    return pl.pallas_call(
        functools.partial(_ssd_body, lc=lc),
        grid=(b, t // lc),
        in_specs=[seq(SSD_WIDTH), seq(CONV_DIM), seq(LANES),
                  pl.BlockSpec((1, SUBLANES, CONV_DIM), per_b),
                  pl.BlockSpec((1, LANES, SSD_WIDTH), per_b)] + [_full(a.shape) for a in consts],
        out_specs=[seq(SSD_WIDTH), pl.BlockSpec((1, LANES, SSD_WIDTH), lambda i, c: (i, 0, 0))],
        out_shape=[jax.ShapeDtypeStruct((b, t, SSD_WIDTH), BF16),
                   jax.ShapeDtypeStruct((b, LANES, SSD_WIDTH), F32)],
        scratch_shapes=[pltpu.VMEM((lc + SUBLANES, CONV_DIM), F32),
                        pltpu.VMEM((LANES, SSD_WIDTH), F32),
                        pltpu.VMEM((lc, SSD_WIDTH), F32)],
        compiler_params=_params("parallel", "arbitrary"),
        name="ssd",
    )(z3, xbc3, dt3, hist, s0, *consts)


def _s5_param_body(lr_ref, li_ref, step_ref, br_ref, bi_ref, abr_ref, abi_ref, bbr_ref, bbi_ref):
    lr, li, step = lr_ref[...], li_ref[...], jnp.exp(step_ref[...])
    mag = jnp.exp(lr * step)
    ab_re = mag * jnp.cos(li * step)
    ab_im = mag * jnp.sin(li * step)
    den = lr * lr + li * li
    nr = ab_re - 1.0
    f_re = (nr * lr + ab_im * li) / den
    f_im = (ab_im * lr - nr * li) / den
    br, bi = br_ref[...], bi_ref[...]
    abr_ref[...] = ab_re
    abi_ref[...] = ab_im
    bbr_ref[...] = f_re * br - f_im * bi
    bbi_ref[...] = f_re * bi + f_im * br


def _s5_params(lam_re, lam_im, log_step, b_re, b_im):
    rep = lambda t: jnp.repeat(t, S5_GROUP_CH, axis=0)
    shape = (S5_WIDTH, S5_STATE)
    step = jnp.broadcast_to(rep(log_step[:, None]), shape)
    to_rows = lambda t: jnp.transpose(t, (0, 2, 1)).reshape(shape)
    outs = pl.pallas_call(
        _s5_param_body,
        out_shape=[jax.ShapeDtypeStruct(shape, F32)] * 4,
        name="s5_params",
    )(rep(lam_re), rep(lam_im), step, to_rows(b_re), to_rows(b_im))
    return outs


S5_STEPS = 16
S5_SEQS = 16
S5_SCAN_LANES = 256


def _s5_body(u_ref, sre_ref, sim_ref, abr_ref, abi_ref, bbr_ref, bbi_ref, ccr_ref, cci_ref, dsk_ref,
             perm_ref, perm_t_ref, y_ref, ore_ref, oim_ref, xre_ref, xim_ref, cr_ref, ci_ref, *, nb):
    c = pl.program_id(1)
    rows = nb * S5_STEPS

    @pl.when(c == 0)
    def _():
        cr_ref[...] = jnp.broadcast_to(sre_ref[...], cr_ref.shape)
        ci_ref[...] = jnp.broadcast_to(sim_ref[...], ci_ref.shape)

    u = u_ref[...].reshape(rows, S5_WIDTH)
    u_tm = _dot(perm_ref[...], u.astype(BF16)).astype(BF16)
    perm_t = perm_t_ref[...]
    for j in range(S5_NBLK):
        ub = u_tm[:, j * LANES:(j + 1) * LANES]
        xre_ref[j] = _dot(ub, bbr_ref[j])
        xim_ref[j] = _dot(ub, bbi_ref[j])
    for j in range(S5_NBLK):
        xre, xim = xre_ref.at[j], xim_ref.at[j]
        for h in range(S5_BLK_LANES // S5_SCAN_LANES):
            cols = slice(h * S5_SCAN_LANES, (h + 1) * S5_SCAN_LANES)
            lanes = slice(j * S5_BLK_LANES + h * S5_SCAN_LANES, j * S5_BLK_LANES + (h + 1) * S5_SCAN_LANES)
            ar, ai = abr_ref[:, lanes], abi_ref[:, lanes]
            cr, ci = cr_ref[:, lanes], ci_ref[:, lanes]
            for s in range(S5_STEPS):
                step = slice(s * nb, (s + 1) * nb)
                cr, ci = (ar * cr - ai * ci + xre[step, cols], ar * ci + ai * cr + xim[step, cols])
                xre[step, cols] = cr
                xim[step, cols] = ci
            cr_ref[:, lanes] = cr
            ci_ref[:, lanes] = ci
    for j in range(S5_NBLK):
        yx = _dot(xre_ref[j].astype(BF16), ccr_ref[j]) - _dot(xim_ref[j].astype(BF16), cci_ref[j])
        yb = _dot_split_rhs(perm_t, yx, 2) + dsk_ref[:, j * LANES:(j + 1) * LANES] * u[:, j * LANES:(j + 1) * LANES]
        gelu = 0.5 * yb * (1.0 + jnp.tanh(math.sqrt(2.0 / math.pi) * (yb + 0.044715 * (yb * yb * yb))))
        y_ref[:, :, j * LANES:(j + 1) * LANES] = gelu.astype(BF16).reshape(nb, S5_STEPS, LANES)

    @pl.when(c == pl.num_programs(1) - 1)
    def _():
        ore_ref[...] = cr_ref[...]
        oim_ref[...] = ci_ref[...]


def _s5(u3, s_re, s_im, lw):
    b, t, _ = u3.shape
    shared = s_re.shape[0] == 1
    bp = -(-b // SUBLANES) * SUBLANES
    if bp != b:
        u3 = jnp.pad(u3, ((0, bp - b), (0, 0), (0, 0)))
        if not shared:
            s_re, s_im = (jnp.pad(s, ((0, bp - b), (0, 0))) for s in (s_re, s_im))
    nb = min(bp, S5_SEQS)
    rows = nb * S5_STEPS
    r = jnp.arange(rows)
    perm = ((r % S5_STEPS) * nb + r // S5_STEPS)[None, :] == r[:, None]
    perm, perm_t = perm.astype(BF16), perm.T.astype(BF16)
    state = pl.BlockSpec((1, S5_LANES), lambda i, c: (0, 0)) if shared else pl.BlockSpec((nb, S5_LANES), lambda i, c: (i, 0))
    state_out = pl.BlockSpec((nb, S5_LANES), lambda i, c: (i, 0))
    seq = pl.BlockSpec((nb, S5_STEPS, S5_WIDTH), lambda i, c: (i, c, 0))
    consts = (lw['s5_ab_re'], lw['s5_ab_im'], lw['s5_bb_re'], lw['s5_bb_im'], lw['s5_cc_re'],
              lw['s5_cc_im'], lw['d_s5'], perm, perm_t)
    y, o_re, o_im = pl.pallas_call(
        functools.partial(_s5_body, nb=nb),
        grid=(bp // nb, t // S5_STEPS),
        in_specs=[seq, state, state] + [_full(a.shape) for a in consts],
        out_specs=[seq, state_out, state_out],
        out_shape=[jax.ShapeDtypeStruct((bp, t, S5_WIDTH), BF16),
                   jax.ShapeDtypeStruct((bp, S5_LANES), F32),
                   jax.ShapeDtypeStruct((bp, S5_LANES), F32)],
        scratch_shapes=[pltpu.VMEM((S5_NBLK, rows, S5_BLK_LANES), F32), pltpu.VMEM((S5_NBLK, rows, S5_BLK_LANES), F32),
                        pltpu.VMEM((nb, S5_LANES), F32), pltpu.VMEM((nb, S5_LANES), F32)],
        compiler_params=_params("parallel", "arbitrary"),
        name="s5",
    )(u3, s_re, s_im, *consts)
    return y[:b], o_re[:b], o_im[:b]


SB_KEYS = 384
SB_CACHE_KEYS = 256
SB_PAIRS = SB_WIDTH // LANES
SB_DEAD = -105.0


def _attn_body(*refs, bq, n_hist, pad):
    if n_hist:
        q_ref, upper_ref, kn_ref, vn_ref, kh_ref, vh_ref, o_ref, kall_ref, vall_ref, acc_ref, carry_ref = refs
    else:
        q_ref, upper_ref, kn_ref, vn_ref, o_ref, kall_ref, vall_ref, acc_ref, carry_ref = refs
    qi = pl.program_id(1)

    @pl.when(qi == 0)
    def _():
        for all_ref, new_ref, hist_ref in ((kall_ref, kn_ref, kh_ref if n_hist else None),
                                           (vall_ref, vn_ref, vh_ref if n_hist else None)):
            if pad:
                all_ref[0:pad, :] = jnp.zeros((pad, SB_WIDTH), BF16)
            if n_hist:
                all_ref[pad:pad + n_hist, :] = hist_ref[0].astype(BF16)
            all_ref[pad + n_hist:, :] = new_ref[0]

    k_ref, v_ref = kall_ref, vall_ref
    q = q_ref[0]
    lane = lax.broadcasted_iota(jnp.int32, (bq, LANES), 1)
    left = lane < SB_HEAD_DIM
    zero_b = jnp.zeros((), BF16)
    qms = [jnp.concatenate([jnp.where(left, q[:, p * LANES:(p + 1) * LANES], zero_b),
                            jnp.where(left, zero_b, q[:, p * LANES:(p + 1) * LANES])], axis=0)
           for p in range(SB_PAIRS)]
    ri = lax.broadcasted_iota(jnp.int32, (2 * bq, SB_KEYS), 0)
    ci = lax.broadcasted_iota(jnp.int32, (2 * bq, SB_KEYS), 1)
    ri = jnp.where(ri >= bq, ri - bq, ri)
    strictly_earlier = ci - (SB_KEYS - bq) < ri
    upper = upper_ref[...]
    n_before = n_hist + (qi + 1) * bq
    end = pad + n_before
    trips = (n_before + SB_KEYS - 1) // SB_KEYS
    acc_ref[...] = jnp.zeros_like(acc_ref)
    carry_ref[...] = jnp.zeros_like(carry_ref)

    def visit(it, masked):
        rows = pl.ds(pl.multiple_of(end - (it + 1) * SB_KEYS, 16), SB_KEYS)
        _sb_visit([_dot_nt(qms[p], k_ref[rows, p * LANES:(p + 1) * LANES]) for p in range(SB_PAIRS)],
                  lambda p, w: _dot(w, v_ref[rows, p * LANES:(p + 1) * LANES]),
                  strictly_earlier if masked else None, upper, acc_ref, carry_ref, left, bq)

    def any_weight_left():
        c = carry_ref[0]
        for p in range(1, SB_PAIRS):
            c = jnp.maximum(c, carry_ref[p])
        return jnp.max(c) > SB_DEAD

    visit(0, True)

    def trip(state):
        it, _ = state
        visit(it, False)
        return it + 1, any_weight_left()

    lax.while_loop(lambda s: jnp.logical_and(s[0] < trips, s[1]), trip, (jnp.int32(1), any_weight_left()))
    o_ref[0] = acc_ref[...].astype(BF16)


BF16_ROWS = 16


def _attn(qb, kb, vb, kh, vh):
    b, t, _ = qb.shape
    bq = min(t, 128)
    n_hist = 0 if kh is None else kh.shape[1]
    assert n_hist % BF16_ROWS == 0 and bq % BF16_ROWS == 0
    short = max(-(n_hist + (qi + 1) * bq) % SB_KEYS for qi in range(t // bq))
    pad = -(-short // BF16_ROWS) * BF16_ROWS
    block = pl.BlockSpec((1, bq, SB_WIDTH), lambda i, j: (i, j, 0))
    new = pl.BlockSpec((1, t, SB_WIDTH), lambda i, j: (i, 0, 0))
    upper = jnp.tril(jnp.ones((SB_KEYS, SB_KEYS), BF16), -1)
    args, specs = [qb, upper, kb, vb], [block, _full(upper.shape), new, new]
    if n_hist:
        hmap = (lambda i, j: (0, 0, 0)) if kh.shape[0] == 1 else (lambda i, j: (i, 0, 0))
        args += [kh, vh]
        specs += [pl.BlockSpec((1, n_hist, SB_WIDTH), hmap)] * 2
    rows = pad + n_hist + t
    return pl.pallas_call(
        functools.partial(_attn_body, bq=bq, n_hist=n_hist, pad=pad),
        grid=(b, t // bq),
        in_specs=specs,
        out_specs=block,
        out_shape=jax.ShapeDtypeStruct((b, t, SB_WIDTH), BF16),
        scratch_shapes=[pltpu.VMEM((rows, SB_WIDTH), BF16), pltpu.VMEM((rows, SB_WIDTH), BF16),
                        pltpu.VMEM((bq, SB_WIDTH), F32), pltpu.VMEM((SB_PAIRS, 2 * bq, 1), F32)],
        compiler_params=_params("parallel", "arbitrary"),
        name="attn",
    )(*args)


def _sb_visit(zs, values, mask, upper, acc_ref, carry_ref, left, bq):
    log_ws, totals = [], []
    for z in zs:
        sp = jnp.log(1.0 + jnp.exp(-jnp.abs(z)))
        log_beta = jnp.minimum(z, 0.0) - sp
        log_keep = log_beta - z
        if mask is not None:
            log_keep = jnp.where(mask, log_keep, 0.0)
        hi = log_keep.astype(BF16)
        lo = (log_keep - hi.astype(F32)).astype(BF16)
        both = _dot(jnp.concatenate([hi, lo], axis=0), upper)
        later = both[:2 * bq] + both[2 * bq:]
        log_ws.append(log_beta + later)
        totals.append(later[:, 0:1] + log_keep[:, 0:1])
    for p in range(SB_PAIRS):
        carry = carry_ref[p]
        w = jnp.exp(log_ws[p] + carry)
        if mask is not None:
            w = jnp.where(mask, w, 0.0)
        pv = values(p, w.astype(BF16))
        carry_ref[p] = carry + totals[p]
        acc_ref[:, p * LANES:(p + 1) * LANES] += jnp.where(left, pv[:bq], pv[bq:])


def _attn_cached_body(q_ref, upper_ref, upper_new_ref, kn_ref, vn_ref, kh_ref, vh_ref, o_ref, acc_ref, carry_ref, *, bq, n_hist):
    q = q_ref[0]
    lane = lax.broadcasted_iota(jnp.int32, (bq, LANES), 1)
    left = lane < SB_HEAD_DIM
    zero_b = jnp.zeros((), BF16)
    qms = [jnp.concatenate([jnp.where(left, q[:, p * LANES:(p + 1) * LANES], zero_b),
                            jnp.where(left, zero_b, q[:, p * LANES:(p + 1) * LANES])], axis=0)
           for p in range(SB_PAIRS)]
    ri = lax.broadcasted_iota(jnp.int32, (2 * bq, bq), 0)
    ci = lax.broadcasted_iota(jnp.int32, (2 * bq, bq), 1)
    strictly_earlier = ci < jnp.where(ri >= bq, ri - bq, ri)
    upper = upper_ref[...]
    acc_ref[...] = jnp.zeros_like(acc_ref)
    carry_ref[...] = jnp.zeros_like(carry_ref)
    pair = lambda p: slice(p * LANES, (p + 1) * LANES)

    _sb_visit([_dot_nt(qms[p], kn_ref[0, :, pair(p)]) for p in range(SB_PAIRS)],
              lambda p, w: _dot(w, vn_ref[0, :, pair(p)]),
              strictly_earlier, upper_new_ref[...], acc_ref, carry_ref, left, bq)

    def any_weight_left():
        c = carry_ref[0]
        for p in range(1, SB_PAIRS):
            c = jnp.maximum(c, carry_ref[p])
        return jnp.max(c) > SB_DEAD

    def trip(state):
        it, _ = state
        window = pl.ds(pl.multiple_of(n_hist - (it + 1) * SB_CACHE_KEYS, SB_CACHE_KEYS), SB_CACHE_KEYS)
        kt = kh_ref[0, 0, :, :, window].reshape(SB_WIDTH, SB_CACHE_KEYS).astype(BF16)
        vt = vh_ref[0, 0, :, :, window].reshape(SB_WIDTH, SB_CACHE_KEYS).astype(BF16)
        _sb_visit([_dot(qms[p], kt[pair(p)]) for p in range(SB_PAIRS)],
                  lambda p, w: _dot_nt(w, vt[pair(p)]),
                  None, upper, acc_ref, carry_ref, left, bq)
        return it + 1, any_weight_left()

    lax.while_loop(lambda s: jnp.logical_and(s[0] < n_hist // SB_CACHE_KEYS, s[1]), trip,
                   (jnp.int32(0), any_weight_left()))
    o_ref[0] = acc_ref[...].astype(BF16)


def _attn_cached(qb, kb, vb, cache_k, cache_v, layer):
    b, t, _ = qb.shape
    n_hist = cache_k.shape[2]
    assert t <= 128 and t % BF16_ROWS == 0 and n_hist % SB_CACHE_KEYS == 0
    upper = jnp.tril(jnp.ones((SB_CACHE_KEYS, SB_CACHE_KEYS), BF16), -1)
    upper_new = jnp.tril(jnp.ones((t, t), BF16), -1)
    block = pl.BlockSpec((1, t, SB_WIDTH), lambda i: (i, 0, 0))
    hist =pl.BlockSpec((1, 1, SB_HEADS, SB_HEAD_DIM, n_hist), lambda i: (layer, i, 0, 0, 0))
    minor = lambda c: jnp.transpose(c, (0, 1, 3, 4, 2))
    return pl.pallas_call(
        functools.partial(_attn_cached_body, bq=t, n_hist=n_hist),
        grid=(b,),
        in_specs=[block, _full(upper.shape), _full(upper_new.shape), block, block, hist, hist],
        out_specs=block,
        out_shape=jax.ShapeDtypeStruct((b, t, SB_WIDTH), BF16),
        scratch_shapes=[pltpu.VMEM((t, SB_WIDTH), F32), pltpu.VMEM((SB_PAIRS, 2 * t, 1), F32)],
        compiler_params=_params("parallel"),
        name="attn_cached",
    )(qb, upper, upper_new, kb, vb, minor(cache_k), minor(cache_v))


def _kv_rows_body(*refs, b, t, depth):
    ko_ref, vo_ref = refs[4 * depth:]
    layer = pl.program_id(0) // b
    for l in range(depth):
        km_ref, vm_ref, kn_ref, vn_ref = refs[4 * l:4 * l + 4]
        n_meta = km_ref.shape[0]
        total = n_meta + t

        @pl.when(layer == l)
        def _(km_ref=km_ref, vm_ref=vm_ref, kn_ref=kn_ref, vn_ref=vn_ref, n_meta=n_meta, total=total):
            for meta_ref, new_ref, out_ref in ((km_ref, kn_ref, ko_ref), (vm_ref, vn_ref, vo_ref)):
                for c in range(-(-total // LANES)):
                    lo, hi = c * LANES, min((c + 1) * LANES, total)
                    parts = []
                    if lo < n_meta:
                        parts.append(meta_ref[lo:min(hi, n_meta), :])
                    if hi > n_meta:
                        parts.append(new_ref[max(lo, n_meta) - n_meta:hi - n_meta, :])
                    if hi - lo < LANES:
                        parts.append(jnp.zeros((LANES - (hi - lo), SB_WIDTH), F32))
                    blk = parts[0] if len(parts) == 1 else jnp.concatenate(parts, axis=0)
                    blk_t = blk.T.reshape(SB_HEADS, SB_HEAD_DIM, LANES)
                    out_ref[0, 0, :, :, lo:hi] = blk_t[:, :, :hi - lo]


def _kv_rows(meta_rows, new_rows, b):
    depth = len(new_rows)
    n = new_rows[0][0].shape[0]
    t = n // b
    n_meta = meta_rows[0][0].shape[0]
    args, specs = [], []
    for l in range(depth):
        rows = pl.BlockSpec((t, SB_WIDTH), lambda i, l=l: (jnp.clip(i - l * b, 0, b - 1), 0))
        args += [*meta_rows[l], *new_rows[l]]
        specs += [_full((n_meta, SB_WIDTH))] * 2 + [rows] * 2
    out = pl.BlockSpec((1, 1, SB_HEADS, SB_HEAD_DIM, n_meta + t), lambda i: (i // b, i % b, 0, 0, 0))
    shape = jax.ShapeDtypeStruct((depth, b, SB_HEADS, SB_HEAD_DIM, n_meta + t), F32)
    k_out, v_out = pl.pallas_call(
        functools.partial(_kv_rows_body, b=b, t=t, depth=depth),
        grid=(depth * b,),
        in_specs=specs,
        out_specs=[out, out],
        out_shape=[shape, shape],
        compiler_params=_params("arbitrary"),
        name="kv_rows",
    )(*args)
    return jnp.transpose(k_out, (0, 1, 4, 2, 3)), jnp.transpose(v_out, (0, 1, 4, 2, 3))


def _merge_body(x_ref, ya_ref, yb_ref, oc_ref, g_ref, wg_ref, wa_ref, wglu_ref, wc_ref, wo_ref, o_ref):
    x = x_ref[...]
    hb = _rmsnorm_rows(x, g_ref[...]).astype(BF16)
    gates = _sigmoid(_dot(hb, wg_ref[...]))
    glu = _dot(yb_ref[...], wglu_ref[...])
    mix = (gates[:, :D_MODEL] * _dot(ya_ref[...], wa_ref[...])
           + gates[:, D_MODEL:2 * D_MODEL] * (glu[:, :D_MODEL] * _sigmoid(glu[:, D_MODEL:]))
           + gates[:, 2 * D_MODEL:] * _dot(oc_ref[...], wc_ref[...]))
    o_ref[...] = x + _dot(mix.astype(BF16), wo_ref[...])


def _merge(x2, ya, yb, oc, lw):
    n = x2.shape[0]
    tm = _row_tile(n)
    rows = lambda w: pl.BlockSpec((tm, w), lambda i: (i, 0))
    consts = (lw['norm_mix'], lw['wgate'], lw['w_lift_a'], lw['w_glu'], lw['w_lift_c'], lw['w_out'])
    return pl.pallas_call(
        _merge_body,
        grid=(n // tm,),
        in_specs=[rows(D_MODEL), rows(SSD_WIDTH), rows(S5_WIDTH), rows(SB_WIDTH)]
                 + [_full(a.shape) for a in consts],
        out_specs=rows(D_MODEL),
        out_shape=jax.ShapeDtypeStruct((n, D_MODEL), F32),
        compiler_params=_params("parallel"),
        name="merge",
    )(x2, ya, yb, oc, *consts)


FF_CHUNK = 1024


def _ffn_body(x_ref, g_ref, wu_ref, wd_ref, o_ref):
    x = x_ref[...]
    hb = _rmsnorm_rows(x, g_ref[...]).astype(BF16)
    acc = x
    for j in range(D_FF // FF_CHUNK):
        cols = slice(j * FF_CHUNK, (j + 1) * FF_CHUNK)
        up = jnp.maximum(_dot(hb, wu_ref[:, cols]), 0.0)
        acc = acc + _dot((up * up).astype(BF16), wd_ref[cols, :])
    o_ref[...] = acc


def _ffn(x2, lw):
    n = x2.shape[0]
    tm = _row_tile(n)
    rows = pl.BlockSpec((tm, D_MODEL), lambda i: (i, 0))
    consts = (lw['norm_ffn'], lw['w_up'], lw['w_down'])
    return pl.pallas_call(
        _ffn_body,
        grid=(n // tm,),
        in_specs=[rows] + [_full(a.shape) for a in consts],
        out_specs=rows,
        out_shape=jax.ShapeDtypeStruct((n, D_MODEL), F32),
        compiler_params=_params("parallel"),
        name="ffn",
    )(x2, *consts)


def _block_diag(blocks):
    n, r, c = blocks.shape
    eye = jnp.eye(n, dtype=blocks.dtype)
    return (eye[:, None, :, None] * blocks[:, :, None, :]).reshape(n * r, n * c)


def _layer_weights(p, l):
    w_in = p['w_in'][l]
    col = lambda off, n: w_in[:, off:off + n].astype(BF16)
    row = lambda v: v.reshape(1, -1).astype(F32)
    heads = jnp.arange(SSD_WIDTH) // SSD_HEAD_DIM
    pad_lanes = lambda v: jnp.pad(v, (0, LANES - v.shape[0])).reshape(1, LANES)
    ab_re, ab_im, bb_re, bb_im = _s5_params(p['lam_re'][l], p['lam_im'][l], p['log_step'][l],
                                            p['b_re'][l], p['b_im'][l])
    to_bb = lambda t: jnp.stack([_block_diag(blk) for blk in
                                 t.reshape(S5_NBLK, S5_BLK_GROUPS, S5_GROUP_CH, S5_STATE)]).astype(BF16)
    to_cc = lambda t: jnp.stack([_block_diag(blk) for blk in
                                 jnp.transpose(t, (0, 2, 1)).reshape(S5_NBLK, S5_BLK_GROUPS, S5_STATE, S5_GROUP_CH)]
                                ).astype(BF16)
    state_rows = jnp.arange(LANES) // SSD_STATE
    state_cols = jnp.arange(SSD_WIDTH) // (SSD_WIDTH // SSD_GROUPS)
    return {
        'norm_mix': row(p['norm_mix'][l]),
        'wz': col(OFF_Z, SSD_WIDTH), 'wxbc': col(OFF_XBC, CONV_DIM),
        'wdt': jnp.pad(col(OFF_DT, SSD_HEADS), ((0, 0), (0, LANES - SSD_HEADS))),
        'wu': col(OFF_U, S5_WIDTH), 'wq': col(OFF_Q, SB_WIDTH), 'wk': col(OFF_K, SB_WIDTH),
        'wv': col(OFF_V, SB_WIDTH), 'wgate': col(OFF_GATE, N_BRANCH * D_MODEL),
        'q_norm': row(jnp.tile(p['q_norm'][l], SB_HEADS)), 'k_norm': row(jnp.tile(p['k_norm'][l], SB_HEADS)),
        'head_mean': (_block_diag(jnp.ones((SB_HEADS, SB_HEAD_DIM, SB_HEAD_DIM), F32)) / SB_HEAD_DIM).astype(BF16),
        'conv_w': jnp.pad(p['conv_w'][l], ((0, SUBLANES - CONV_W), (0, 0))),
        'conv_b': row(p['conv_b'][l]),
        'dt_bias': pad_lanes(p['dt_bias'][l]), 'a_log': pad_lanes(p['a_log'][l]),
        'd_ssd': row(p['d_ssd'][l][heads]), 'norm_ssd': row(p['norm_ssd'][l]),
        'eye': jnp.eye(LANES, dtype=BF16),
        'state_mask': (state_rows[:, None] == state_cols[None, :]).astype(F32),
        's5_ab_re': ab_re[::S5_GROUP_CH].reshape(1, S5_LANES), 's5_ab_im': ab_im[::S5_GROUP_CH].reshape(1, S5_LANES),
        's5_bb_re': to_bb(bb_re), 's5_bb_im': to_bb(bb_im),
        's5_cc_re': to_cc(p['c_re'][l]), 's5_cc_im': to_cc(p['c_im'][l]),
        'd_s5': row(p['d_s5'][l]),
        'w_glu': p['w_glu'][l].astype(BF16), 'w_lift_a': p['w_lift_a'][l].astype(BF16),
        'w_lift_c': p['w_lift_c'][l].astype(BF16), 'w_out': p['w_out'][l].astype(BF16),
        'norm_ffn': row(p['norm_ffn'][l]), 'w_up': p['w_up'][l].astype(BF16), 'w_down': p['w_down'][l].astype(BF16),
    }


def _ssd_state_in(s):
    b = s.shape[0]
    t = jnp.transpose(s.astype(F32), (0, 1, 4, 2, 3)).reshape(b, SSD_GROUPS, SSD_STATE, SSD_HPG * SSD_HEAD_DIM)
    eye = jnp.eye(SSD_GROUPS, dtype=F32)
    return (t[:, :, :, None, :] * eye[None, :, None, :, None]).reshape(b, LANES, SSD_WIDTH)


def _ssd_state_out(s):
    b = s.shape[0]
    t = s.reshape(b, SSD_GROUPS, SSD_STATE, SSD_GROUPS, SSD_HPG, SSD_HEAD_DIM)
    t = jnp.stack([t[:, g, :, g] for g in range(SSD_GROUPS)], axis=1)
    return jnp.transpose(t, (0, 1, 3, 4, 2))


def _trunk_layer(x, k_hist, v_hist, conv_hist, ssd_s0, s5_re0, s5_im0, lw, cache_layer=None):
    b, t, _ = x.shape
    x2 = x.reshape(b * t, D_MODEL)
    z, xbc, dt, u, k, v, qb, kb, vb = _proj(x2, lw)
    seq = lambda a: a.reshape(b, t, a.shape[-1])
    hist8 = jnp.pad(conv_hist.astype(F32), ((0, 0), (SUBLANES - (CONV_W - 1), 0), (0, 0)))
    y_a, ssd_new = _ssd(seq(z), seq(xbc), seq(dt), hist8, _ssd_state_in(ssd_s0), lw)
    y_b, s5_re, s5_im = _s5(seq(u), s5_re0.reshape(-1, S5_LANES).astype(F32),
                            s5_im0.reshape(-1, S5_LANES).astype(F32), lw)
    if cache_layer is not None:
        o_c = _attn_cached(seq(qb), seq(kb), seq(vb), k_hist.astype(F32), v_hist.astype(F32), cache_layer)
    elif k_hist is None:
        o_c = _attn(seq(qb), seq(kb), seq(vb), None, None)
    else:
        o_c = _attn(seq(qb), seq(kb), seq(vb), k_hist.reshape(k_hist.shape[0], -1, SB_WIDTH).astype(F32),
                    v_hist.reshape(v_hist.shape[0], -1, SB_WIDTH).astype(F32))
    x2 = _merge(x2, y_a.reshape(b * t, SSD_WIDTH), y_b.reshape(b * t, S5_WIDTH), o_c.reshape(b * t, SB_WIDTH), lw)
    x2 = _ffn(x2, lw)
    conv_rows = jnp.concatenate([jnp.broadcast_to(conv_hist.astype(F32), (b, CONV_W - 1, CONV_DIM)),
                                 seq(xbc)[:, -(CONV_W - 1):]], axis=1)[:, -(CONV_W - 1):]
    return (x2.reshape(b, t, D_MODEL), k, v, conv_rows, _ssd_state_out(ssd_new),
            s5_re.reshape(b, S5_GROUPS, S5_STATE), s5_im.reshape(b, S5_GROUPS, S5_STATE))


def kernel(x_prompt, x_sample, cache_k, cache_v, state_conv, state_ssd, state_s5_re, state_s5_im, meta_tokens, norm_mix, w_in, conv_w, conv_b, dt_bias, a_log, d_ssd, norm_ssd, lam_re, lam_im, log_step, b_re, b_im, c_re, c_im, d_s5, w_glu, q_norm, k_norm, w_lift_a, w_lift_c, w_out, norm_ffn, w_up, w_down):
    p = dict(norm_mix=norm_mix, w_in=w_in, conv_w=conv_w, conv_b=conv_b, dt_bias=dt_bias, a_log=a_log,
             d_ssd=d_ssd, norm_ssd=norm_ssd, lam_re=lam_re, lam_im=lam_im, log_step=log_step, b_re=b_re,
             b_im=b_im, c_re=c_re, c_im=c_im, d_s5=d_s5, w_glu=w_glu, q_norm=q_norm, k_norm=k_norm,
             w_lift_a=w_lift_a, w_lift_c=w_lift_c, w_out=w_out, norm_ffn=norm_ffn, w_up=w_up, w_down=w_down)
    depth = w_in.shape[0]
    bp = x_prompt.shape[0]
    bs, ts, _ = x_sample.shape
    xm = meta_tokens.astype(x_prompt.dtype)[None]
    xp, xs = x_prompt, x_sample
    zeros = lambda *s: jnp.zeros(s, F32)
    kv_m, kv_p, kv_s, outs_p, outs_s = [], [], [], [], []
    for l in range(depth):
        lw = _layer_weights(p, l)
        xm, k_m, v_m, conv_m, ssd_m, s5re_m, s5im_m = _trunk_layer(
            xm, None, None, zeros(1, CONV_W - 1, CONV_DIM), zeros(1, SSD_GROUPS, SSD_HPG, SSD_HEAD_DIM, SSD_STATE),
            zeros(1, S5_GROUPS, S5_STATE), zeros(1, S5_GROUPS, S5_STATE), lw)
        xp, k_p, v_p, *st_p = _trunk_layer(xp, k_m[None], v_m[None], conv_m, ssd_m, s5re_m, s5im_m, lw)
        xs, k_s, v_s, *st_s = _trunk_layer(xs, cache_k, cache_v, state_conv[l], state_ssd[l],
                                           state_s5_re[l], state_s5_im[l], lw, cache_layer=l)
        kv_m.append((k_m, v_m))
        kv_p.append((k_p, v_p))
        kv_s.append((k_s, v_s))
        outs_p.append(st_p)
        outs_s.append(st_s)
    stk = lambda outs, i: jnp.stack([o[i] for o in outs], axis=0)
    k_prompt, v_prompt = _kv_rows(kv_m, kv_p, bp)
    sample_rows = lambda i: stk(kv_s, i).reshape(depth, bs, ts, SB_HEADS, SB_HEAD_DIM)
    return (xp, xs,
            k_prompt, v_prompt, stk(outs_p, 0), stk(outs_p, 1), stk(outs_p, 2), stk(outs_p, 3),
            sample_rows(0), sample_rows(1), stk(outs_s, 0), stk(outs_s, 1), stk(outs_s, 2), stk(outs_s, 3))
```

```python
import functools
import math

import jax
import jax.numpy as jnp
from jax import lax
from jax.experimental import pallas as pl
from jax.experimental.pallas import tpu as pltpu

F32 = jnp.float32
BF16 = jnp.bfloat16

D_MODEL = 1024
N_META = 16
RMS_EPS = 1e-6
SSD_HEADS = 16
SSD_HEAD_DIM = 64
SSD_GROUPS = 2
SSD_HPG = SSD_HEADS // SSD_GROUPS
SSD_STATE = 64
SSD_WIDTH = SSD_HEADS * SSD_HEAD_DIM
CONV_W = 4
N_BC = SSD_GROUPS * SSD_STATE
CONV_DIM = SSD_WIDTH + 2 * N_BC
S5_WIDTH = D_MODEL // 2
S5_GROUP_CH = 16
S5_GROUPS = S5_WIDTH // S5_GROUP_CH
S5_STATE = 64
S5_LANES = S5_GROUPS * S5_STATE
SB_HEADS = 8
SB_HEAD_DIM = 64
SB_WIDTH = SB_HEADS * SB_HEAD_DIM
SB_SCALE = 1.0 / math.sqrt(SB_HEAD_DIM)
N_BRANCH = 3
D_FF = 4 * D_MODEL
OFF_Z = 0
OFF_XBC = OFF_Z + SSD_WIDTH
OFF_DT = OFF_XBC + CONV_DIM
OFF_U = OFF_DT + SSD_HEADS
OFF_Q = OFF_U + S5_WIDTH
OFF_K = OFF_Q + SB_WIDTH
OFF_V = OFF_K + SB_WIDTH
OFF_GATE = OFF_V + SB_WIDTH

LANES = 128
SUBLANES = 8
VMEM_LIMIT = 56 * 1024 * 1024
S5_BLK_GROUPS = LANES // S5_GROUP_CH
S5_BLK_LANES = S5_BLK_GROUPS * S5_STATE
S5_NBLK = S5_GROUPS // S5_BLK_GROUPS


def _dot(a, b):
    return jnp.dot(a, b, preferred_element_type=F32)


def _dot_nt(a, b):
    return lax.dot_general(a, b, (((1,), (1,)), ((), ())), preferred_element_type=F32)


def _split(a, terms):
    out = []
    r = a
    for _ in range(terms):
        p = r.astype(BF16)
        out.append(p)
        r = r - p.astype(F32)
    return out


def _dot_split_lhs(a, b_bf16, terms=3):
    return sum(_dot(p, b_bf16) for p in _split(a, terms))


def _dot_split_rhs(a_bf16, b, terms=3):
    return sum(_dot(a_bf16, p) for p in _split(b, terms))


def _sigmoid(x):
    return 1.0 / (1.0 + jnp.exp(-x))


def _softplus(x):
    return jnp.maximum(x, 0.0) + jnp.log(1.0 + jnp.exp(-jnp.abs(x)))


def _rmsnorm_rows(x, g):
    return x * lax.rsqrt(jnp.mean(x * x, axis=-1, keepdims=True) + RMS_EPS) * g


def _params(*sem):
    return pltpu.CompilerParams(dimension_semantics=sem, vmem_limit_bytes=VMEM_LIMIT)


def _full(shape):
    n = len(shape)
    return pl.BlockSpec(shape, lambda *_: (0,) * n)


def _row_tile(n):
    for t in (256, 128, 64, 32, 16, 8):
        if n % t == 0:
            return t
    raise ValueError(f"token count {n} is not a multiple of {SUBLANES}")


def _proj_body(x_ref, g_ref, wz_ref, wxbc_ref, wdt_ref, wu_ref, wq_ref, wk_ref, wv_ref,
               qn_ref, kn_ref, hm_ref,
               z_ref, xbc_ref, dt_ref, u_ref, k_ref, v_ref, qb_ref, kb_ref, vb_ref, hb_ref):
    hb_ref[...] = _rmsnorm_rows(x_ref[...], g_ref[...]).astype(BF16)
    z_ref[...] = _dot(hb_ref[...], wz_ref[...])
    xbc_ref[...] = _dot(hb_ref[...], wxbc_ref[...])
    dt_ref[...] = _dot(hb_ref[...], wdt_ref[...])
    u_ref[...] = _dot(hb_ref[...], wu_ref[...])
    hm = hm_ref[...]

    def head_norm(t, w):
        return t * lax.rsqrt(_dot_split_lhs(t * t, hm, 2) + RMS_EPS) * w

    q = head_norm(_dot(hb_ref[...], wq_ref[...]), qn_ref[...])
    k = head_norm(_dot(hb_ref[...], wk_ref[...]), kn_ref[...])
    v = _dot(hb_ref[...], wv_ref[...])
    k_ref[...] = k
    v_ref[...] = v
    qb_ref[...] = (q * SB_SCALE).astype(BF16)
    kb_ref[...] = k.astype(BF16)
    vb_ref[...] = v.astype(BF16)


def _proj(x2, lw):
    n = x2.shape[0]
    tm = _row_tile(n)
    rows = lambda w: pl.BlockSpec((tm, w), lambda i: (i, 0))
    widths = (SSD_WIDTH, CONV_DIM, LANES, S5_WIDTH, SB_WIDTH, SB_WIDTH, SB_WIDTH, SB_WIDTH, SB_WIDTH)
    dtypes = (F32, F32, F32, F32, F32, F32, BF16, BF16, BF16)
    ins = (lw['norm_mix'], lw['wz'], lw['wxbc'], lw['wdt'], lw['wu'], lw['wq'], lw['wk'], lw['wv'],
           lw['q_norm'], lw['k_norm'], lw['head_mean'])
    return pl.pallas_call(
        _proj_body,
        grid=(n // tm,),
        in_specs=[rows(D_MODEL)] + [_full(a.shape) for a in ins],
        out_specs=[rows(w) for w in widths],
        out_shape=[jax.ShapeDtypeStruct((n, w), d) for w, d in zip(widths, dtypes)],
        scratch_shapes=[pltpu.VMEM((tm, D_MODEL), BF16)],
        compiler_params=_params("parallel"),
        name="proj",
    )(x2, *ins)


def _ssd_body(z_ref, xbc_ref, dt_ref, hist_ref, s0_ref, cw_ref, cb_ref, dtb_ref, alog_ref,
              dskip_ref, nw_ref, eye_ref, smask_ref, tri_ref,
              y_ref, sout_ref, xp_ref, st_ref, yacc_ref, *, lc):
    c = pl.program_id(1)

    @pl.when(c == 0)
    def _():
        xp_ref[0:SUBLANES, :] = hist_ref[0]
        none = jnp.zeros((SSD_HEAD_DIM, SSD_STATE), F32)
        for hp in range(SSD_HEADS // 2):
            rows = [jnp.concatenate([s0_ref[0, h], none] if h < SSD_HPG else [none, s0_ref[0, h]], axis=1)
                    for h in (2 * hp, 2 * hp + 1)]
            st_ref[:, hp * LANES:(hp + 1) * LANES] = jnp.concatenate(rows, axis=0).T

    xp_ref[SUBLANES:SUBLANES + lc, :] = xbc_ref[0]
    conv = cb_ref[...]
    for k in range(CONV_W):
        lo = SUBLANES - (CONV_W - 1) + k
        conv = conv + xp_ref[lo:lo + lc, :] * cw_ref[k:k + 1, :]
    xp_ref[0:SUBLANES, :] = xp_ref[lc:lc + SUBLANES, :]
    act = conv * _sigmoid(conv)
    xs = act[:, :SSD_WIDTH]
    bm = act[:, SSD_WIDTH:SSD_WIDTH + N_BC]
    cm = act[:, SSD_WIDTH + N_BC:]

    eye = eye_ref[...]
    dt = _softplus(dt_ref[0] + dtb_ref[...])
    a = -jnp.exp(alog_ref[...])
    ri = lax.broadcasted_iota(jnp.int32, (lc, lc), 0)
    ci = lax.broadcasted_iota(jnp.int32, (lc, lc), 1)
    causal = ri >= ci
    acum = _dot_split_rhs(tri_ref[...], dt * a)
    if lc == LANES:
        transpose = lambda t: t.T
    else:
        transpose = lambda t: sum(_dot_nt(eye, p) for p in _split(t, 3))
    acum_t = transpose(acum)
    log_dt_t = jnp.log(transpose(dt))
    a_last = acum[lc - 1:lc, :]
    shifted_t = acum_t - log_dt_t
    to_end_t = jnp.exp(acum_t[:, lc - 1:lc] - shifted_t)
    dec = jnp.exp(a_last)

    xs_b = xs.astype(BF16)
    bm_b = bm.astype(BF16)
    cm_b = cm.astype(BF16)
    bm_t = bm.T if lc == LANES else _dot_nt(eye, bm_b)
    lane = lax.broadcasted_iota(jnp.int32, (lc, LANES), 1)
    left = lane < SSD_STATE
    left_sq = lax.broadcasted_iota(jnp.int32, (LANES, LANES), 1) < SSD_HEAD_DIM
    zero_b = jnp.zeros((), BF16)
    cbs = [_dot_nt(jnp.where(left if g == 0 else ~left, cm_b, zero_b), bm_b) for g in range(SSD_GROUPS)]
    st = st_ref[...]
    st_b = st.astype(BF16)
    smask = smask_ref[...]

    for hp in range(SSD_HEADS // 2):
        pair = slice(hp * LANES, (hp + 1) * LANES)
        rhs = jnp.concatenate([xs_b[:, pair], st_b[:, pair]], axis=0)
        outs, news = [], []
        for h in (2 * hp, 2 * hp + 1):
            col = jnp.broadcast_to(acum[:, h:h + 1], (lc, LANES))
            seg = col[:, :lc] - shifted_t[h:h + 1, :]
            m = jnp.where(causal, cbs[h // SSD_HPG] * jnp.exp(seg), 0.0)
            read = cm * jnp.exp(col)
            outs.append(_dot(jnp.concatenate([m.astype(BF16), read.astype(BF16)], axis=1), rhs))
            news.append(_dot((bm_t * to_end_t[h:h + 1, :]).astype(BF16), xs_b[:, pair]))
        yacc_ref[:, pair] = jnp.where(left, outs[0], outs[1])
        decay = jnp.where(left_sq, dec[:, 2 * hp:2 * hp + 1], dec[:, 2 * hp + 1:2 * hp + 2])
        st_ref[:, pair] = st[:, pair] * decay + smask[:, pair] * jnp.where(left_sq, news[0], news[1])

    y = (yacc_ref[...] + dskip_ref[...] * xs)
    zz = z_ref[0]
    y = y * (zz * _sigmoid(zz))
    gw = SSD_WIDTH // SSD_GROUPS
    parts = []
    for g in range(SSD_GROUPS):
        yg = y[:, g * gw:(g + 1) * gw]
        parts.append(yg * lax.rsqrt(jnp.mean(yg * yg, axis=-1, keepdims=True) + RMS_EPS))
    y_ref[0] = (jnp.concatenate(parts, axis=1) * nw_ref[...]).astype(BF16)

    @pl.when(c == pl.num_programs(1) - 1)
    def _():
        for hp in range(SSD_HEADS // 2):
            both = st_ref[:, hp * LANES:(hp + 1) * LANES].T
            for i, h in enumerate((2 * hp, 2 * hp + 1)):
                g = h // SSD_HPG
                sout_ref[0, h] = both[i * SSD_HEAD_DIM:(i + 1) * SSD_HEAD_DIM, g * SSD_STATE:(g + 1) * SSD_STATE]


def _ssd(z3, xbc3, dt3, hist, s0, lw):
    b, t, _ = z3.shape
    lc = min(t, 128)
    per_b = (lambda i, c: (0, 0, 0)) if hist.shape[0] == 1 else (lambda i, c: (i, 0, 0))
    per_b4 = (lambda i, c: (0, 0, 0, 0)) if hist.shape[0] == 1 else (lambda i, c: (i, 0, 0, 0))
    state = (1, SSD_HEADS, SSD_HEAD_DIM, SSD_STATE)
    seq = lambda w: pl.BlockSpec((1, lc, w), lambda i, c: (i, c, 0))
    consts = (lw['conv_w'], lw['conv_b'], lw['dt_bias'], lw['a_log'], lw['d_ssd'], lw['norm_ssd'],
              lw['eye'], lw['state_mask'], jnp.tril(jnp.ones((lc, lc), BF16)))
    return pl.pallas_call(
        functools.partial(_ssd_body, lc=lc),
        grid=(b, t // lc),
        in_specs=[seq(SSD_WIDTH), seq(CONV_DIM), seq(LANES),
                  pl.BlockSpec((1, SUBLANES, CONV_DIM), per_b),
                  pl.BlockSpec(state, per_b4)] + [_full(a.shape) for a in consts],
        out_specs=[seq(SSD_WIDTH), pl.BlockSpec(state, lambda i, c: (i, 0, 0, 0))],
        out_shape=[jax.ShapeDtypeStruct((b, t, SSD_WIDTH), BF16),
                   jax.ShapeDtypeStruct((b,) + state[1:], F32)],
        scratch_shapes=[pltpu.VMEM((lc + SUBLANES, CONV_DIM), F32),
                        pltpu.VMEM((LANES, SSD_WIDTH), F32),
                        pltpu.VMEM((lc, SSD_WIDTH), F32)],
        compiler_params=_params("parallel", "arbitrary"),
        name="ssd",
    )(z3, xbc3, dt3, hist, s0, *consts)


def _s5_param_body(lr_ref, li_ref, step_ref, br_ref, bi_ref, abr_ref, abi_ref, bbr_ref, bbi_ref):
    lr, li, step = lr_ref[...], li_ref[...], jnp.exp(step_ref[...])
    mag = jnp.exp(lr * step)
    ab_re = mag * jnp.cos(li * step)
    ab_im = mag * jnp.sin(li * step)
    den = lr * lr + li * li
    nr = ab_re - 1.0
    f_re = (nr * lr + ab_im * li) / den
    f_im = (ab_im * lr - nr * li) / den
    br, bi = br_ref[...], bi_ref[...]
    abr_ref[...] = ab_re
    abi_ref[...] = ab_im
    bbr_ref[...] = f_re * br - f_im * bi
    bbi_ref[...] = f_re * bi + f_im * br


def _s5_params(lam_re, lam_im, log_step, b_re, b_im):
    rep = lambda t: jnp.repeat(t, S5_GROUP_CH, axis=0)
    shape = (S5_WIDTH, S5_STATE)
    step = jnp.broadcast_to(rep(log_step[:, None]), shape)
    to_rows = lambda t: jnp.transpose(t, (0, 2, 1)).reshape(shape)
    outs = pl.pallas_call(
        _s5_param_body,
        out_shape=[jax.ShapeDtypeStruct(shape, F32)] * 4,
        name="s5_params",
    )(rep(lam_re), rep(lam_im), step, to_rows(b_re), to_rows(b_im))
    return outs


S5_STEPS = 16
S5_SEQS = 16
S5_SCAN_LANES = 256


def _s5_body(u_ref, sre_ref, sim_ref, abr_ref, abi_ref, bbr_ref, bbi_ref, ccr_ref, cci_ref, dsk_ref,
             perm_ref, perm_t_ref, y_ref, ore_ref, oim_ref, xre_ref, xim_ref, cr_ref, ci_ref, *, nb):
    c = pl.program_id(1)
    rows = nb * S5_STEPS

    @pl.when(c == 0)
    def _():
        cr_ref[...] = jnp.broadcast_to(sre_ref[...], cr_ref.shape)
        ci_ref[...] = jnp.broadcast_to(sim_ref[...], ci_ref.shape)

    u = u_ref[...].reshape(rows, S5_WIDTH)
    u_tm = _dot(perm_ref[...], u.astype(BF16)).astype(BF16)
    perm_t = perm_t_ref[...]
    for j in range(S5_NBLK):
        ub = u_tm[:, j * LANES:(j + 1) * LANES]
        xre_ref[j] = _dot(ub, bbr_ref[j])
        xim_ref[j] = _dot(ub, bbi_ref[j])
    for j in range(S5_NBLK):
        xre, xim = xre_ref.at[j], xim_ref.at[j]
        for h in range(S5_BLK_LANES // S5_SCAN_LANES):
            cols = slice(h * S5_SCAN_LANES, (h + 1) * S5_SCAN_LANES)
            lanes = slice(j * S5_BLK_LANES + h * S5_SCAN_LANES, j * S5_BLK_LANES + (h + 1) * S5_SCAN_LANES)
            ar, ai = abr_ref[:, lanes], abi_ref[:, lanes]
            cr, ci = cr_ref[:, lanes], ci_ref[:, lanes]
            for s in range(S5_STEPS):
                step = slice(s * nb, (s + 1) * nb)
                cr, ci = (ar * cr - ai * ci + xre[step, cols], ar * ci + ai * cr + xim[step, cols])
                xre[step, cols] = cr
                xim[step, cols] = ci
            cr_ref[:, lanes] = cr
            ci_ref[:, lanes] = ci
    for j in range(S5_NBLK):
        yx = _dot(xre_ref[j].astype(BF16), ccr_ref[j]) - _dot(xim_ref[j].astype(BF16), cci_ref[j])
        yb = _dot_split_rhs(perm_t, yx, 2) + dsk_ref[:, j * LANES:(j + 1) * LANES] * u[:, j * LANES:(j + 1) * LANES]
        gelu = 0.5 * yb * (1.0 + jnp.tanh(math.sqrt(2.0 / math.pi) * (yb + 0.044715 * (yb * yb * yb))))
        y_ref[:, :, j * LANES:(j + 1) * LANES] = gelu.astype(BF16).reshape(nb, S5_STEPS, LANES)

    @pl.when(c == pl.num_programs(1) - 1)
    def _():
        ore_ref[...] = cr_ref[...]
        oim_ref[...] = ci_ref[...]


def _s5(u3, s_re, s_im, lw):
    b, t, _ = u3.shape
    shared = s_re.shape[0] == 1
    bp = -(-b // SUBLANES) * SUBLANES
    if bp != b:
        u3 = jnp.pad(u3, ((0, bp - b), (0, 0), (0, 0)))
        if not shared:
            s_re, s_im = (jnp.pad(s, ((0, bp - b), (0, 0))) for s in (s_re, s_im))
    nb = min(bp, S5_SEQS)
    rows = nb * S5_STEPS
    r = jnp.arange(rows)
    perm = ((r % S5_STEPS) * nb + r // S5_STEPS)[None, :] == r[:, None]
    perm, perm_t = perm.astype(BF16), perm.T.astype(BF16)
    state = pl.BlockSpec((1, S5_LANES), lambda i, c: (0, 0)) if shared else pl.BlockSpec((nb, S5_LANES), lambda i, c: (i, 0))
    state_out = pl.BlockSpec((nb, S5_LANES), lambda i, c: (i, 0))
    seq = pl.BlockSpec((nb, S5_STEPS, S5_WIDTH), lambda i, c: (i, c, 0))
    consts = (lw['s5_ab_re'], lw['s5_ab_im'], lw['s5_bb_re'], lw['s5_bb_im'], lw['s5_cc_re'],
              lw['s5_cc_im'], lw['d_s5'], perm, perm_t)
    y, o_re, o_im = pl.pallas_call(
        functools.partial(_s5_body, nb=nb),
        grid=(bp // nb, t // S5_STEPS),
        in_specs=[seq, state, state] + [_full(a.shape) for a in consts],
        out_specs=[seq, state_out, state_out],
        out_shape=[jax.ShapeDtypeStruct((bp, t, S5_WIDTH), BF16),
                   jax.ShapeDtypeStruct((bp, S5_LANES), F32),
                   jax.ShapeDtypeStruct((bp, S5_LANES), F32)],
        scratch_shapes=[pltpu.VMEM((S5_NBLK, rows, S5_BLK_LANES), F32), pltpu.VMEM((S5_NBLK, rows, S5_BLK_LANES), F32),
                        pltpu.VMEM((nb, S5_LANES), F32), pltpu.VMEM((nb, S5_LANES), F32)],
        compiler_params=_params("parallel", "arbitrary"),
        name="s5",
    )(u3, s_re, s_im, *consts)
    return y[:b], o_re[:b], o_im[:b]


SB_KEYS = 384
SB_CACHE_KEYS = 256
SB_PAIRS = SB_WIDTH // LANES
SB_DEAD = -105.0


def _attn_body(*refs, bq, n_hist, pad):
    if n_hist:
        q_ref, upper_ref, kn_ref, vn_ref, kh_ref, vh_ref, o_ref, kall_ref, vall_ref, acc_ref, carry_ref = refs
    else:
        q_ref, upper_ref, kn_ref, vn_ref, o_ref, kall_ref, vall_ref, acc_ref, carry_ref = refs
    qi = pl.program_id(1)

    @pl.when(qi == 0)
    def _():
        for all_ref, new_ref, hist_ref in ((kall_ref, kn_ref, kh_ref if n_hist else None),
                                           (vall_ref, vn_ref, vh_ref if n_hist else None)):
            if pad:
                all_ref[0:pad, :] = jnp.zeros((pad, SB_WIDTH), BF16)
            if n_hist:
                all_ref[pad:pad + n_hist, :] = hist_ref[0].astype(BF16)
            all_ref[pad + n_hist:, :] = new_ref[0]

    k_ref, v_ref = kall_ref, vall_ref
    q = q_ref[0]
    lane = lax.broadcasted_iota(jnp.int32, (bq, LANES), 1)
    left = lane < SB_HEAD_DIM
    zero_b = jnp.zeros((), BF16)
    qms = [jnp.concatenate([jnp.where(left, q[:, p * LANES:(p + 1) * LANES], zero_b),
                            jnp.where(left, zero_b, q[:, p * LANES:(p + 1) * LANES])], axis=0)
           for p in range(SB_PAIRS)]
    ri = lax.broadcasted_iota(jnp.int32, (2 * bq, SB_KEYS), 0)
    ci = lax.broadcasted_iota(jnp.int32, (2 * bq, SB_KEYS), 1)
    ri = jnp.where(ri >= bq, ri - bq, ri)
    strictly_earlier = ci - (SB_KEYS - bq) < ri
    upper = upper_ref[...]
    n_before = n_hist + (qi + 1) * bq
    end = pad + n_before
    trips = (n_before + SB_KEYS - 1) // SB_KEYS
    acc_ref[...] = jnp.zeros_like(acc_ref)
    carry_ref[...] = jnp.zeros_like(carry_ref)

    def visit(it, masked):
        rows = pl.ds(pl.multiple_of(end - (it + 1) * SB_KEYS, 16), SB_KEYS)
        _sb_visit([_dot_nt(qms[p], k_ref[rows, p * LANES:(p + 1) * LANES]) for p in range(SB_PAIRS)],
                  lambda p, w: _dot(w, v_ref[rows, p * LANES:(p + 1) * LANES]),
                  strictly_earlier if masked else None, upper, acc_ref, carry_ref, left, bq)

    def any_weight_left():
        c = carry_ref[0]
        for p in range(1, SB_PAIRS):
            c = jnp.maximum(c, carry_ref[p])
        return jnp.max(c) > SB_DEAD

    visit(0, True)

    def trip(state):
        it, _ = state
        visit(it, False)
        return it + 1, any_weight_left()

    lax.while_loop(lambda s: jnp.logical_and(s[0] < trips, s[1]), trip, (jnp.int32(1), any_weight_left()))
    o_ref[0] = acc_ref[...].astype(BF16)


BF16_ROWS = 16


def _attn(qb, kb, vb, kh, vh):
    b, t, _ = qb.shape
    bq = min(t, 128)
    n_hist = 0 if kh is None else kh.shape[1]
    assert n_hist % BF16_ROWS == 0 and bq % BF16_ROWS == 0
    short = max(-(n_hist + (qi + 1) * bq) % SB_KEYS for qi in range(t // bq))
    pad = -(-short // BF16_ROWS) * BF16_ROWS
    block = pl.BlockSpec((1, bq, SB_WIDTH), lambda i, j: (i, j, 0))
    new = pl.BlockSpec((1, t, SB_WIDTH), lambda i, j: (i, 0, 0))
    upper = jnp.tril(jnp.ones((SB_KEYS, SB_KEYS), BF16), -1)
    args, specs = [qb, upper, kb, vb], [block, _full(upper.shape), new, new]
    if n_hist:
        hmap = (lambda i, j: (0, 0, 0)) if kh.shape[0] == 1 else (lambda i, j: (i, 0, 0))
        args += [kh, vh]
        specs += [pl.BlockSpec((1, n_hist, SB_WIDTH), hmap)] * 2
    rows = pad + n_hist + t
    return pl.pallas_call(
        functools.partial(_attn_body, bq=bq, n_hist=n_hist, pad=pad),
        grid=(b, t // bq),
        in_specs=specs,
        out_specs=block,
        out_shape=jax.ShapeDtypeStruct((b, t, SB_WIDTH), BF16),
        scratch_shapes=[pltpu.VMEM((rows, SB_WIDTH), BF16), pltpu.VMEM((rows, SB_WIDTH), BF16),
                        pltpu.VMEM((bq, SB_WIDTH), F32), pltpu.VMEM((SB_PAIRS, 2 * bq, 1), F32)],
        compiler_params=_params("parallel", "arbitrary"),
        name="attn",
    )(*args)


def _sb_visit(zs, values, mask, upper, acc_ref, carry_ref, left, bq):
    log_ws, totals = [], []
    for z in zs:
        sp = jnp.log(1.0 + jnp.exp(-jnp.abs(z)))
        log_beta = jnp.minimum(z, 0.0) - sp
        log_keep = log_beta - z
        if mask is not None:
            log_keep = jnp.where(mask, log_keep, 0.0)
        hi = log_keep.astype(BF16)
        lo = (log_keep - hi.astype(F32)).astype(BF16)
        both = _dot(jnp.concatenate([hi, lo], axis=0), upper)
        later = both[:2 * bq] + both[2 * bq:]
        log_ws.append(log_beta + later)
        totals.append(later[:, 0:1] + log_keep[:, 0:1])
    for p in range(SB_PAIRS):
        carry = carry_ref[p]
        w = jnp.exp(log_ws[p] + carry)
        if mask is not None:
            w = jnp.where(mask, w, 0.0)
        pv = values(p, w.astype(BF16))
        carry_ref[p] = carry + totals[p]
        acc_ref[:, p * LANES:(p + 1) * LANES] += jnp.where(left, pv[:bq], pv[bq:])


def _attn_cached_body(q_ref, upper_ref, upper_new_ref, kn_ref, vn_ref, kh_ref, vh_ref, o_ref, acc_ref, carry_ref, *, bq, n_hist):
    q = q_ref[0]
    lane = lax.broadcasted_iota(jnp.int32, (bq, LANES), 1)
    left = lane < SB_HEAD_DIM
    zero_b = jnp.zeros((), BF16)
    qms = [jnp.concatenate([jnp.where(left, q[:, p * LANES:(p + 1) * LANES], zero_b),
                            jnp.where(left, zero_b, q[:, p * LANES:(p + 1) * LANES])], axis=0)
           for p in range(SB_PAIRS)]
    ri = lax.broadcasted_iota(jnp.int32, (2 * bq, bq), 0)
    ci = lax.broadcasted_iota(jnp.int32, (2 * bq, bq), 1)
    strictly_earlier = ci < jnp.where(ri >= bq, ri - bq, ri)
    upper = upper_ref[...]
    acc_ref[...] = jnp.zeros_like(acc_ref)
    carry_ref[...] = jnp.zeros_like(carry_ref)
    pair = lambda p: slice(p * LANES, (p + 1) * LANES)

    _sb_visit([_dot_nt(qms[p], kn_ref[0, :, pair(p)]) for p in range(SB_PAIRS)],
              lambda p, w: _dot(w, vn_ref[0, :, pair(p)]),
              strictly_earlier, upper_new_ref[...], acc_ref, carry_ref, left, bq)

    def any_weight_left():
        c = carry_ref[0]
        for p in range(1, SB_PAIRS):
            c = jnp.maximum(c, carry_ref[p])
        return jnp.max(c) > SB_DEAD

    def trip(state):
        it, _ = state
        window = pl.ds(pl.multiple_of(n_hist - (it + 1) * SB_CACHE_KEYS, SB_CACHE_KEYS), SB_CACHE_KEYS)
        kt = kh_ref[0, 0, :, :, window].reshape(SB_WIDTH, SB_CACHE_KEYS).astype(BF16)
        vt = vh_ref[0, 0, :, :, window].reshape(SB_WIDTH, SB_CACHE_KEYS).astype(BF16)
        _sb_visit([_dot(qms[p], kt[pair(p)]) for p in range(SB_PAIRS)],
                  lambda p, w: _dot_nt(w, vt[pair(p)]),
                  None, upper, acc_ref, carry_ref, left, bq)
        return it + 1, any_weight_left()

    lax.while_loop(lambda s: jnp.logical_and(s[0] < n_hist // SB_CACHE_KEYS, s[1]), trip,
                   (jnp.int32(0), any_weight_left()))
    o_ref[0] = acc_ref[...].astype(BF16)


def _attn_cached(qb, kb, vb, cache_k, cache_v, layer):
    b, t, _ = qb.shape
    n_hist = cache_k.shape[2]
    assert t <= 128 and t % BF16_ROWS == 0 and n_hist % SB_CACHE_KEYS == 0
    upper = jnp.tril(jnp.ones((SB_CACHE_KEYS, SB_CACHE_KEYS), BF16), -1)
    upper_new = jnp.tril(jnp.ones((t, t), BF16), -1)
    block = pl.BlockSpec((1, t, SB_WIDTH), lambda i: (i, 0, 0))
    hist =pl.BlockSpec((1, 1, SB_HEADS, SB_HEAD_DIM, n_hist), lambda i: (layer, i, 0, 0, 0))
    minor = lambda c: jnp.transpose(c, (0, 1, 3, 4, 2))
    return pl.pallas_call(
        functools.partial(_attn_cached_body, bq=t, n_hist=n_hist),
        grid=(b,),
        in_specs=[block, _full(upper.shape), _full(upper_new.shape), block, block, hist, hist],
        out_specs=block,
        out_shape=jax.ShapeDtypeStruct((b, t, SB_WIDTH), BF16),
        scratch_shapes=[pltpu.VMEM((t, SB_WIDTH), F32), pltpu.VMEM((SB_PAIRS, 2 * t, 1), F32)],
        compiler_params=_params("parallel"),
        name="attn_cached",
    )(qb, upper, upper_new, kb, vb, minor(cache_k), minor(cache_v))


def _kv_rows_body(*refs, b, t, depth):
    ko_ref, vo_ref = refs[4 * depth:]
    layer = pl.program_id(0) // b
    for l in range(depth):
        km_ref, vm_ref, kn_ref, vn_ref = refs[4 * l:4 * l + 4]
        n_meta = km_ref.shape[0]
        total = n_meta + t

        @pl.when(layer == l)
        def _(km_ref=km_ref, vm_ref=vm_ref, kn_ref=kn_ref, vn_ref=vn_ref, n_meta=n_meta, total=total):
            for meta_ref, new_ref, out_ref in ((km_ref, kn_ref, ko_ref), (vm_ref, vn_ref, vo_ref)):
                for c in range(-(-total // LANES)):
                    lo, hi = c * LANES, min((c + 1) * LANES, total)
                    parts = []
                    if lo < n_meta:
                        parts.append(meta_ref[lo:min(hi, n_meta), :])
                    if hi > n_meta:
                        parts.append(new_ref[max(lo, n_meta) - n_meta:hi - n_meta, :])
                    if hi - lo < LANES:
                        parts.append(jnp.zeros((LANES - (hi - lo), SB_WIDTH), F32))
                    blk = parts[0] if len(parts) == 1 else jnp.concatenate(parts, axis=0)
                    blk_t = blk.T.reshape(SB_HEADS, SB_HEAD_DIM, LANES)
                    out_ref[0, 0, :, :, lo:hi] = blk_t[:, :, :hi - lo]


def _kv_rows(meta_rows, new_rows, b):
    depth = len(new_rows)
    n = new_rows[0][0].shape[0]
    t = n // b
    n_meta = meta_rows[0][0].shape[0]
    args, specs = [], []
    for l in range(depth):
        rows = pl.BlockSpec((t, SB_WIDTH), lambda i, l=l: (jnp.clip(i - l * b, 0, b - 1), 0))
        args += [*meta_rows[l], *new_rows[l]]
        specs += [_full((n_meta, SB_WIDTH))] * 2 + [rows] * 2
    out = pl.BlockSpec((1, 1, SB_HEADS, SB_HEAD_DIM, n_meta + t), lambda i: (i // b, i % b, 0, 0, 0))
    shape = jax.ShapeDtypeStruct((depth, b, SB_HEADS, SB_HEAD_DIM, n_meta + t), F32)
    k_out, v_out = pl.pallas_call(
        functools.partial(_kv_rows_body, b=b, t=t, depth=depth),
        grid=(depth * b,),
        in_specs=specs,
        out_specs=[out, out],
        out_shape=[shape, shape],
        compiler_params=_params("arbitrary"),
        name="kv_rows",
    )(*args)
    return jnp.transpose(k_out, (0, 1, 4, 2, 3)), jnp.transpose(v_out, (0, 1, 4, 2, 3))


def _merge_body(x_ref, ya_ref, yb_ref, oc_ref, g_ref, wg_ref, wa_ref, wglu_ref, wc_ref, wo_ref, o_ref):
    x = x_ref[...]
    hb = _rmsnorm_rows(x, g_ref[...]).astype(BF16)
    gates = _sigmoid(_dot(hb, wg_ref[...]))
    glu = _dot(yb_ref[...], wglu_ref[...])
    mix = (gates[:, :D_MODEL] * _dot(ya_ref[...], wa_ref[...])
           + gates[:, D_MODEL:2 * D_MODEL] * (glu[:, :D_MODEL] * _sigmoid(glu[:, D_MODEL:]))
           + gates[:, 2 * D_MODEL:] * _dot(oc_ref[...], wc_ref[...]))
    o_ref[...] = x + _dot(mix.astype(BF16), wo_ref[...])


def _merge(x2, ya, yb, oc, lw):
    n = x2.shape[0]
    tm = _row_tile(n)
    rows = lambda w: pl.BlockSpec((tm, w), lambda i: (i, 0))
    consts = (lw['norm_mix'], lw['wgate'], lw['w_lift_a'], lw['w_glu'], lw['w_lift_c'], lw['w_out'])
    return pl.pallas_call(
        _merge_body,
        grid=(n // tm,),
        in_specs=[rows(D_MODEL), rows(SSD_WIDTH), rows(S5_WIDTH), rows(SB_WIDTH)]
                 + [_full(a.shape) for a in consts],
        out_specs=rows(D_MODEL),
        out_shape=jax.ShapeDtypeStruct((n, D_MODEL), F32),
        compiler_params=_params("parallel"),
        name="merge",
    )(x2, ya, yb, oc, *consts)


FF_CHUNK = 1024


def _ffn_body(x_ref, g_ref, wu_ref, wd_ref, o_ref):
    x = x_ref[...]
    hb = _rmsnorm_rows(x, g_ref[...]).astype(BF16)
    acc = x
    for j in range(D_FF // FF_CHUNK):
        cols = slice(j * FF_CHUNK, (j + 1) * FF_CHUNK)
        up = jnp.maximum(_dot(hb, wu_ref[:, cols]), 0.0)
        acc = acc + _dot((up * up).astype(BF16), wd_ref[cols, :])
    o_ref[...] = acc


def _ffn(x2, lw):
    n = x2.shape[0]
    tm = _row_tile(n)
    rows = pl.BlockSpec((tm, D_MODEL), lambda i: (i, 0))
    consts = (lw['norm_ffn'], lw['w_up'], lw['w_down'])
    return pl.pallas_call(
        _ffn_body,
        grid=(n // tm,),
        in_specs=[rows] + [_full(a.shape) for a in consts],
        out_specs=rows,
        out_shape=jax.ShapeDtypeStruct((n, D_MODEL), F32),
        compiler_params=_params("parallel"),
        name="ffn",
    )(x2, *consts)


def _block_diag(blocks):
    n, r, c = blocks.shape
    eye = jnp.eye(n, dtype=blocks.dtype)
    return (eye[:, None, :, None] * blocks[:, :, None, :]).reshape(n * r, n * c)


def _layer_weights(p, l):
    w_in = p['w_in'][l]
    col = lambda off, n: w_in[:, off:off + n].astype(BF16)
    row = lambda v: v.reshape(1, -1).astype(F32)
    heads = jnp.arange(SSD_WIDTH) // SSD_HEAD_DIM
    pad_lanes = lambda v: jnp.pad(v, (0, LANES - v.shape[0])).reshape(1, LANES)
    ab_re, ab_im, bb_re, bb_im = _s5_params(p['lam_re'][l], p['lam_im'][l], p['log_step'][l],
                                            p['b_re'][l], p['b_im'][l])
    to_bb = lambda t: jnp.stack([_block_diag(blk) for blk in
                                 t.reshape(S5_NBLK, S5_BLK_GROUPS, S5_GROUP_CH, S5_STATE)]).astype(BF16)
    to_cc = lambda t: jnp.stack([_block_diag(blk) for blk in
                                 jnp.transpose(t, (0, 2, 1)).reshape(S5_NBLK, S5_BLK_GROUPS, S5_STATE, S5_GROUP_CH)]
                                ).astype(BF16)
    state_rows = jnp.arange(LANES) // SSD_STATE
    state_cols = jnp.arange(SSD_WIDTH) // (SSD_WIDTH // SSD_GROUPS)
    return {
        'norm_mix': row(p['norm_mix'][l]),
        'wz': col(OFF_Z, SSD_WIDTH), 'wxbc': col(OFF_XBC, CONV_DIM),
        'wdt': jnp.pad(col(OFF_DT, SSD_HEADS), ((0, 0), (0, LANES - SSD_HEADS))),
        'wu': col(OFF_U, S5_WIDTH), 'wq': col(OFF_Q, SB_WIDTH), 'wk': col(OFF_K, SB_WIDTH),
        'wv': col(OFF_V, SB_WIDTH), 'wgate': col(OFF_GATE, N_BRANCH * D_MODEL),
        'q_norm': row(jnp.tile(p['q_norm'][l], SB_HEADS)), 'k_norm': row(jnp.tile(p['k_norm'][l], SB_HEADS)),
        'head_mean': (_block_diag(jnp.ones((SB_HEADS, SB_HEAD_DIM, SB_HEAD_DIM), F32)) / SB_HEAD_DIM).astype(BF16),
        'conv_w': jnp.pad(p['conv_w'][l], ((0, SUBLANES - CONV_W), (0, 0))),
        'conv_b': row(p['conv_b'][l]),
        'dt_bias': pad_lanes(p['dt_bias'][l]), 'a_log': pad_lanes(p['a_log'][l]),
        'd_ssd': row(p['d_ssd'][l][heads]), 'norm_ssd': row(p['norm_ssd'][l]),
        'eye': jnp.eye(LANES, dtype=BF16),
        'state_mask': (state_rows[:, None] == state_cols[None, :]).astype(F32),
        's5_ab_re': ab_re[::S5_GROUP_CH].reshape(1, S5_LANES), 's5_ab_im': ab_im[::S5_GROUP_CH].reshape(1, S5_LANES),
        's5_bb_re': to_bb(bb_re), 's5_bb_im': to_bb(bb_im),
        's5_cc_re': to_cc(p['c_re'][l]), 's5_cc_im': to_cc(p['c_im'][l]),
        'd_s5': row(p['d_s5'][l]),
        'w_glu': p['w_glu'][l].astype(BF16), 'w_lift_a': p['w_lift_a'][l].astype(BF16),
        'w_lift_c': p['w_lift_c'][l].astype(BF16), 'w_out': p['w_out'][l].astype(BF16),
        'norm_ffn': row(p['norm_ffn'][l]), 'w_up': p['w_up'][l].astype(BF16), 'w_down': p['w_down'][l].astype(BF16),
    }


def _trunk_layer(x, k_hist, v_hist, conv_hist, ssd_s0, s5_re0, s5_im0, lw, cache_layer=None):
    b, t, _ = x.shape
    x2 = x.reshape(b * t, D_MODEL)
    z, xbc, dt, u, k, v, qb, kb, vb = _proj(x2, lw)
    seq = lambda a: a.reshape(b, t, a.shape[-1])
    hist8 = jnp.pad(conv_hist.astype(F32), ((0, 0), (SUBLANES - (CONV_W - 1), 0), (0, 0)))
    y_a, ssd_new = _ssd(seq(z), seq(xbc), seq(dt), hist8,
                        ssd_s0.astype(F32).reshape(-1, SSD_HEADS, SSD_HEAD_DIM, SSD_STATE), lw)
    y_b, s5_re, s5_im = _s5(seq(u), s5_re0.reshape(-1, S5_LANES).astype(F32),
                            s5_im0.reshape(-1, S5_LANES).astype(F32), lw)
    if cache_layer is not None:
        o_c = _attn_cached(seq(qb), seq(kb), seq(vb), k_hist.astype(F32), v_hist.astype(F32), cache_layer)
    elif k_hist is None:
        o_c = _attn(seq(qb), seq(kb), seq(vb), None, None)
    else:
        o_c = _attn(seq(qb), seq(kb), seq(vb), k_hist.reshape(k_hist.shape[0], -1, SB_WIDTH).astype(F32),
                    v_hist.reshape(v_hist.shape[0], -1, SB_WIDTH).astype(F32))
    x2 = _merge(x2, y_a.reshape(b * t, SSD_WIDTH), y_b.reshape(b * t, S5_WIDTH), o_c.reshape(b * t, SB_WIDTH), lw)
    x2 = _ffn(x2, lw)
    conv_rows = jnp.concatenate([jnp.broadcast_to(conv_hist.astype(F32), (b, CONV_W - 1, CONV_DIM)),
                                 seq(xbc)[:, -(CONV_W - 1):]], axis=1)[:, -(CONV_W - 1):]
    return (x2.reshape(b, t, D_MODEL), k, v, conv_rows,
            ssd_new.reshape(b, SSD_GROUPS, SSD_HPG, SSD_HEAD_DIM, SSD_STATE),
            s5_re.reshape(b, S5_GROUPS, S5_STATE), s5_im.reshape(b, S5_GROUPS, S5_STATE))


def kernel(x_prompt, x_sample, cache_k, cache_v, state_conv, state_ssd, state_s5_re, state_s5_im, meta_tokens, norm_mix, w_in, conv_w, conv_b, dt_bias, a_log, d_ssd, norm_ssd, lam_re, lam_im, log_step, b_re, b_im, c_re, c_im, d_s5, w_glu, q_norm, k_norm, w_lift_a, w_lift_c, w_out, norm_ffn, w_up, w_down):
    p = dict(norm_mix=norm_mix, w_in=w_in, conv_w=conv_w, conv_b=conv_b, dt_bias=dt_bias, a_log=a_log,
             d_ssd=d_ssd, norm_ssd=norm_ssd, lam_re=lam_re, lam_im=lam_im, log_step=log_step, b_re=b_re,
             b_im=b_im, c_re=c_re, c_im=c_im, d_s5=d_s5, w_glu=w_glu, q_norm=q_norm, k_norm=k_norm,
             w_lift_a=w_lift_a, w_lift_c=w_lift_c, w_out=w_out, norm_ffn=norm_ffn, w_up=w_up, w_down=w_down)
    depth = w_in.shape[0]
    bp = x_prompt.shape[0]
    bs, ts, _ = x_sample.shape
    xm = meta_tokens.astype(x_prompt.dtype)[None]
    xp, xs = x_prompt, x_sample
    zeros = lambda *s: jnp.zeros(s, F32)
    kv_m, kv_p, kv_s, outs_p, outs_s = [], [], [], [], []
    for l in range(depth):
        lw = _layer_weights(p, l)
        xm, k_m, v_m, conv_m, ssd_m, s5re_m, s5im_m = _trunk_layer(
            xm, None, None, zeros(1, CONV_W - 1, CONV_DIM), zeros(1, SSD_GROUPS, SSD_HPG, SSD_HEAD_DIM, SSD_STATE),
            zeros(1, S5_GROUPS, S5_STATE), zeros(1, S5_GROUPS, S5_STATE), lw)
        xp, k_p, v_p, *st_p = _trunk_layer(xp, k_m[None], v_m[None], conv_m, ssd_m, s5re_m, s5im_m, lw)
        xs, k_s, v_s, *st_s = _trunk_layer(xs, cache_k, cache_v, state_conv[l], state_ssd[l],
                                           state_s5_re[l], state_s5_im[l], lw, cache_layer=l)
        kv_m.append((k_m, v_m))
        kv_p.append((k_p, v_p))
        kv_s.append((k_s, v_s))
        outs_p.append(st_p)
        outs_s.append(st_s)
    stk = lambda outs, i: jnp.stack([o[i] for o in outs], axis=0)
    k_prompt, v_prompt = _kv_rows(kv_m, kv_p, bp)
    sample_rows = lambda i: stk(kv_s, i).reshape(depth, bs, ts, SB_HEADS, SB_HEAD_DIM)
    return (xp, xs,
            k_prompt, v_prompt, stk(outs_p, 0), stk(outs_p, 1), stk(outs_p, 2), stk(outs_p, 3),
            sample_rows(0), sample_rows(1), stk(outs_s, 0), stk(outs_s, 1), stk(outs_s, 2), stk(outs_s, 3))
```

```python
import functools
import math

import jax
import jax.numpy as jnp
from jax import lax
from jax.experimental import pallas as pl
from jax.experimental.pallas import tpu as pltpu

F32 = jnp.float32
BF16 = jnp.bfloat16

D_MODEL = 1024
N_META = 16
RMS_EPS = 1e-6
SSD_HEADS = 16
SSD_HEAD_DIM = 64
SSD_GROUPS = 2
SSD_HPG = SSD_HEADS // SSD_GROUPS
SSD_STATE = 64
SSD_WIDTH = SSD_HEADS * SSD_HEAD_DIM
CONV_W = 4
N_BC = SSD_GROUPS * SSD_STATE
CONV_DIM = SSD_WIDTH + 2 * N_BC
S5_WIDTH = D_MODEL // 2
S5_GROUP_CH = 16
S5_GROUPS = S5_WIDTH // S5_GROUP_CH
S5_STATE = 64
S5_LANES = S5_GROUPS * S5_STATE
SB_HEADS = 8
SB_HEAD_DIM = 64
SB_WIDTH = SB_HEADS * SB_HEAD_DIM
SB_SCALE = 1.0 / math.sqrt(SB_HEAD_DIM)
N_BRANCH = 3
D_FF = 4 * D_MODEL
OFF_Z = 0
OFF_XBC = OFF_Z + SSD_WIDTH
OFF_DT = OFF_XBC + CONV_DIM
OFF_U = OFF_DT + SSD_HEADS
OFF_Q = OFF_U + S5_WIDTH
OFF_K = OFF_Q + SB_WIDTH
OFF_V = OFF_K + SB_WIDTH
OFF_GATE = OFF_V + SB_WIDTH

LANES = 128
SUBLANES = 8
VMEM_LIMIT = 56 * 1024 * 1024
S5_BLK_GROUPS = LANES // S5_GROUP_CH
S5_BLK_LANES = S5_BLK_GROUPS * S5_STATE
S5_NBLK = S5_GROUPS // S5_BLK_GROUPS


def _dot(a, b):
    return jnp.dot(a, b, preferred_element_type=F32)


def _dot_nt(a, b):
    return lax.dot_general(a, b, (((1,), (1,)), ((), ())), preferred_element_type=F32)


def _split(a, terms):
    out = []
    r = a
    for _ in range(terms):
        p = r.astype(BF16)
        out.append(p)
        r = r - p.astype(F32)
    return out


def _dot_split_lhs(a, b_bf16, terms=3):
    return sum(_dot(p, b_bf16) for p in _split(a, terms))


def _dot_split_rhs(a_bf16, b, terms=3):
    return sum(_dot(a_bf16, p) for p in _split(b, terms))


def _sigmoid(x):
    return 1.0 / (1.0 + jnp.exp(-x))


def _softplus(x):
    return jnp.maximum(x, 0.0) + jnp.log(1.0 + jnp.exp(-jnp.abs(x)))


def _rmsnorm_rows(x, g):
    return x * lax.rsqrt(jnp.mean(x * x, axis=-1, keepdims=True) + RMS_EPS) * g


def _params(*sem):
    return pltpu.CompilerParams(dimension_semantics=sem, vmem_limit_bytes=VMEM_LIMIT)


def _full(shape):
    n = len(shape)
    return pl.BlockSpec(shape, lambda *_: (0,) * n, pipeline_mode=pl.Buffered(1))


def _row_tile(n, cap=256):
    for t in (512, 256, 128, 64, 32, 16, 8):
        if t <= cap and n % t == 0:
            return t
    raise ValueError(f"token count {n} is not a multiple of {SUBLANES}")


def _proj_body(x_ref, g_ref, wz_ref, wxbc_ref, wdt_ref, wu_ref, wq_ref, wk_ref, wv_ref,
               qn_ref, kn_ref, hm_ref,
               z_ref, xbc_ref, dt_ref, u_ref, k_ref, v_ref, qb_ref, kb_ref, vb_ref, hb_ref):
    hb_ref[...] = _rmsnorm_rows(x_ref[...], g_ref[...]).astype(BF16)
    z_ref[...] = _dot(hb_ref[...], wz_ref[...])
    xbc_ref[...] = _dot(hb_ref[...], wxbc_ref[...])
    dt_ref[...] = _dot(hb_ref[...], wdt_ref[...])
    u_ref[...] = _dot(hb_ref[...], wu_ref[...])
    hm = hm_ref[...]

    def head_norm(t, w):
        return t * lax.rsqrt(_dot_split_lhs(t * t, hm, 2) + RMS_EPS) * w

    q = head_norm(_dot(hb_ref[...], wq_ref[...]), qn_ref[...])
    k = head_norm(_dot(hb_ref[...], wk_ref[...]), kn_ref[...])
    v = _dot(hb_ref[...], wv_ref[...])
    k_ref[...] = k
    v_ref[...] = v
    qb_ref[...] = (q * SB_SCALE).astype(BF16)
    kb_ref[...] = k.astype(BF16)
    vb_ref[...] = v.astype(BF16)


def _proj(x2, lw):
    n = x2.shape[0]
    tm = _row_tile(n, 512)
    rows = lambda w: pl.BlockSpec((tm, w), lambda i: (i, 0))
    widths = (SSD_WIDTH, CONV_DIM, LANES, S5_WIDTH, SB_WIDTH, SB_WIDTH, SB_WIDTH, SB_WIDTH, SB_WIDTH)
    dtypes = (F32, F32, F32, F32, F32, F32, BF16, BF16, BF16)
    ins = (lw['norm_mix'], lw['wz'], lw['wxbc'], lw['wdt'], lw['wu'], lw['wq'], lw['wk'], lw['wv'],
           lw['q_norm'], lw['k_norm'], lw['head_mean'])
    return pl.pallas_call(
        _proj_body,
        grid=(n // tm,),
        in_specs=[rows(D_MODEL)] + [_full(a.shape) for a in ins],
        out_specs=[rows(w) for w in widths],
        out_shape=[jax.ShapeDtypeStruct((n, w), d) for w, d in zip(widths, dtypes)],
        scratch_shapes=[pltpu.VMEM((tm, D_MODEL), BF16)],
        compiler_params=_params("parallel"),
        name="proj",
    )(x2, *ins)


def _ssd_body(z_ref, xbc_ref, dt_ref, hist_ref, s0_ref, cw_ref, cb_ref, dtb_ref, alog_ref,
              dskip_ref, nw_ref, eye_ref, smask_ref, tri_ref,
              y_ref, sout_ref, xp_ref, st_ref, yacc_ref, *, lc):
    c = pl.program_id(1)

    @pl.when(c == 0)
    def _():
        xp_ref[0:SUBLANES, :] = hist_ref[0]
        none = jnp.zeros((SSD_HEAD_DIM, SSD_STATE), F32)
        for hp in range(SSD_HEADS // 2):
            rows = [jnp.concatenate([s0_ref[0, h], none] if h < SSD_HPG else [none, s0_ref[0, h]], axis=1)
                    for h in (2 * hp, 2 * hp + 1)]
            st_ref[:, hp * LANES:(hp + 1) * LANES] = jnp.concatenate(rows, axis=0).T

    xp_ref[SUBLANES:SUBLANES + lc, :] = xbc_ref[0]
    conv = cb_ref[...]
    for k in range(CONV_W):
        lo = SUBLANES - (CONV_W - 1) + k
        conv = conv + xp_ref[lo:lo + lc, :] * cw_ref[k:k + 1, :]
    xp_ref[0:SUBLANES, :] = xp_ref[lc:lc + SUBLANES, :]
    act = conv * _sigmoid(conv)
    xs = act[:, :SSD_WIDTH]
    bm = act[:, SSD_WIDTH:SSD_WIDTH + N_BC]
    cm = act[:, SSD_WIDTH + N_BC:]

    eye = eye_ref[...]
    dt = _softplus(dt_ref[0] + dtb_ref[...])
    a = -jnp.exp(alog_ref[...])
    ri = lax.broadcasted_iota(jnp.int32, (lc, lc), 0)
    ci = lax.broadcasted_iota(jnp.int32, (lc, lc), 1)
    causal = ri >= ci
    acum = _dot_split_rhs(tri_ref[...], dt * a)
    if lc == LANES:
        transpose = lambda t: t.T
    else:
        transpose = lambda t: sum(_dot_nt(eye, p) for p in _split(t, 3))
    acum_t = transpose(acum)
    log_dt_t = jnp.log(transpose(dt))
    a_last = acum[lc - 1:lc, :]
    shifted_t = acum_t - log_dt_t
    to_end_t = jnp.exp(acum_t[:, lc - 1:lc] - shifted_t)
    dec = jnp.exp(a_last)

    xs_b = xs.astype(BF16)
    bm_b = bm.astype(BF16)
    cm_b = cm.astype(BF16)
    bm_t = bm.T if lc == LANES else _dot_nt(eye, bm_b)
    lane = lax.broadcasted_iota(jnp.int32, (lc, LANES), 1)
    left = lane < SSD_STATE
    left_sq = lax.broadcasted_iota(jnp.int32, (LANES, LANES), 1) < SSD_HEAD_DIM
    zero_b = jnp.zeros((), BF16)
    cbs = [_dot_nt(jnp.where(left if g == 0 else ~left, cm_b, zero_b), bm_b) for g in range(SSD_GROUPS)]
    st = st_ref[...]
    st_b = st.astype(BF16)
    smask = smask_ref[...]

    for hp in range(SSD_HEADS // 2):
        pair = slice(hp * LANES, (hp + 1) * LANES)
        rhs = jnp.concatenate([xs_b[:, pair], st_b[:, pair]], axis=0)
        outs, news = [], []
        for h in (2 * hp, 2 * hp + 1):
            col = jnp.broadcast_to(acum[:, h:h + 1], (lc, LANES))
            seg = col[:, :lc] - shifted_t[h:h + 1, :]
            m = jnp.where(causal, cbs[h // SSD_HPG] * jnp.exp(seg), 0.0)
            read = cm * jnp.exp(col)
            outs.append(_dot(jnp.concatenate([m.astype(BF16), read.astype(BF16)], axis=1), rhs))
            news.append(_dot((bm_t * to_end_t[h:h + 1, :]).astype(BF16), xs_b[:, pair]))
        yacc_ref[:, pair] = jnp.where(left, outs[0], outs[1])
        decay = jnp.where(left_sq, dec[:, 2 * hp:2 * hp + 1], dec[:, 2 * hp + 1:2 * hp + 2])
        st_ref[:, pair] = st[:, pair] * decay + smask[:, pair] * jnp.where(left_sq, news[0], news[1])

    y = (yacc_ref[...] + dskip_ref[...] * xs)
    zz = z_ref[0]
    y = y * (zz * _sigmoid(zz))
    gw = SSD_WIDTH // SSD_GROUPS
    parts = []
    for g in range(SSD_GROUPS):
        yg = y[:, g * gw:(g + 1) * gw]
        parts.append(yg * lax.rsqrt(jnp.mean(yg * yg, axis=-1, keepdims=True) + RMS_EPS))
    y_ref[0] = (jnp.concatenate(parts, axis=1) * nw_ref[...]).astype(BF16)

    @pl.when(c == pl.num_programs(1) - 1)
    def _():
        for hp in range(SSD_HEADS // 2):
            both = st_ref[:, hp * LANES:(hp + 1) * LANES].T
            for i, h in enumerate((2 * hp, 2 * hp + 1)):
                g = h // SSD_HPG
                sout_ref[0, h] = both[i * SSD_HEAD_DIM:(i + 1) * SSD_HEAD_DIM, g * SSD_STATE:(g + 1) * SSD_STATE]


def _ssd(z3, xbc3, dt3, hist, s0, lw):
    b, t, _ = z3.shape
    lc = min(t, 128)
    per_b = (lambda i, c: (0, 0, 0)) if hist.shape[0] == 1 else (lambda i, c: (i, 0, 0))
    per_b4 = (lambda i, c: (0, 0, 0, 0)) if hist.shape[0] == 1 else (lambda i, c: (i, 0, 0, 0))
    state = (1, SSD_HEADS, SSD_HEAD_DIM, SSD_STATE)
    seq = lambda w: pl.BlockSpec((1, lc, w), lambda i, c: (i, c, 0))
    consts = (lw['conv_w'], lw['conv_b'], lw['dt_bias'], lw['a_log'], lw['d_ssd'], lw['norm_ssd'],
              lw['eye'], lw['state_mask'], jnp.tril(jnp.ones((lc, lc), BF16)))
    return pl.pallas_call(
        functools.partial(_ssd_body, lc=lc),
        grid=(b, t // lc),
        in_specs=[seq(SSD_WIDTH), seq(CONV_DIM), seq(LANES),
                  pl.BlockSpec((1, SUBLANES, CONV_DIM), per_b),
                  pl.BlockSpec(state, per_b4)] + [_full(a.shape) for a in consts],
        out_specs=[seq(SSD_WIDTH), pl.BlockSpec(state, lambda i, c: (i, 0, 0, 0))],
        out_shape=[jax.ShapeDtypeStruct((b, t, SSD_WIDTH), BF16),
                   jax.ShapeDtypeStruct((b,) + state[1:], F32)],
        scratch_shapes=[pltpu.VMEM((lc + SUBLANES, CONV_DIM), F32),
                        pltpu.VMEM((LANES, SSD_WIDTH), F32),
                        pltpu.VMEM((lc, SSD_WIDTH), F32)],
        compiler_params=_params("parallel", "arbitrary"),
        name="ssd",
    )(z3, xbc3, dt3, hist, s0, *consts)


def _s5_param_body(lr_ref, li_ref, step_ref, br_ref, bi_ref, abr_ref, abi_ref, bbr_ref, bbi_ref):
    lr, li, step = lr_ref[...], li_ref[...], jnp.exp(step_ref[...])
    mag = jnp.exp(lr * step)
    ab_re = mag * jnp.cos(li * step)
    ab_im = mag * jnp.sin(li * step)
    den = lr * lr + li * li
    nr = ab_re - 1.0
    f_re = (nr * lr + ab_im * li) / den
    f_im = (ab_im * lr - nr * li) / den
    br, bi = br_ref[...], bi_ref[...]
    abr_ref[...] = ab_re
    abi_ref[...] = ab_im
    bbr_ref[...] = f_re * br - f_im * bi
    bbi_ref[...] = f_re * bi + f_im * br


def _s5_params(lam_re, lam_im, log_step, b_re, b_im):
    rep = lambda t: jnp.repeat(t, S5_GROUP_CH, axis=0)
    shape = (S5_WIDTH, S5_STATE)
    step = jnp.broadcast_to(rep(log_step[:, None]), shape)
    to_rows = lambda t: jnp.transpose(t, (0, 2, 1)).reshape(shape)
    outs = pl.pallas_call(
        _s5_param_body,
        out_shape=[jax.ShapeDtypeStruct(shape, F32)] * 4,
        name="s5_params",
    )(rep(lam_re), rep(lam_im), step, to_rows(b_re), to_rows(b_im))
    return outs


S5_STEPS = 16
S5_SEQS = 16
S5_SCAN_LANES = 256


def _s5_body(u_ref, sre_ref, sim_ref, abr_ref, abi_ref, bbr_ref, bbi_ref, ccr_ref, cci_ref, dsk_ref,
             perm_ref, perm_t_ref, y_ref, ore_ref, oim_ref, xre_ref, xim_ref, cr_ref, ci_ref, *, nb):
    c = pl.program_id(1)
    rows = nb * S5_STEPS

    @pl.when(c == 0)
    def _():
        cr_ref[...] = jnp.broadcast_to(sre_ref[...], cr_ref.shape)
        ci_ref[...] = jnp.broadcast_to(sim_ref[...], ci_ref.shape)

    u = u_ref[...].reshape(rows, S5_WIDTH)
    u_tm = _dot(perm_ref[...], u.astype(BF16)).astype(BF16)
    perm_t = perm_t_ref[...]
    for j in range(S5_NBLK):
        ub = u_tm[:, j * LANES:(j + 1) * LANES]
        xre_ref[j] = _dot(ub, bbr_ref[j])
        xim_ref[j] = _dot(ub, bbi_ref[j])
    for j in range(S5_NBLK):
        xre, xim = xre_ref.at[j], xim_ref.at[j]
        for h in range(S5_BLK_LANES // S5_SCAN_LANES):
            cols = slice(h * S5_SCAN_LANES, (h + 1) * S5_SCAN_LANES)
            lanes = slice(j * S5_BLK_LANES + h * S5_SCAN_LANES, j * S5_BLK_LANES + (h + 1) * S5_SCAN_LANES)
            ar, ai = abr_ref[:, lanes], abi_ref[:, lanes]
            cr, ci = cr_ref[:, lanes], ci_ref[:, lanes]
            for s in range(S5_STEPS):
                step = slice(s * nb, (s + 1) * nb)
                cr, ci = (ar * cr - ai * ci + xre[step, cols], ar * ci + ai * cr + xim[step, cols])
                xre[step, cols] = cr
                xim[step, cols] = ci
            cr_ref[:, lanes] = cr
            ci_ref[:, lanes] = ci
    for j in range(S5_NBLK):
        yx = _dot(xre_ref[j].astype(BF16), ccr_ref[j]) - _dot(xim_ref[j].astype(BF16), cci_ref[j])
        yb = _dot_split_rhs(perm_t, yx, 2) + dsk_ref[:, j * LANES:(j + 1) * LANES] * u[:, j * LANES:(j + 1) * LANES]
        gelu = 0.5 * yb * (1.0 + jnp.tanh(math.sqrt(2.0 / math.pi) * (yb + 0.044715 * (yb * yb * yb))))
        y_ref[:, :, j * LANES:(j + 1) * LANES] = gelu.astype(BF16).reshape(nb, S5_STEPS, LANES)

    @pl.when(c == pl.num_programs(1) - 1)
    def _():
        ore_ref[...] = cr_ref[...]
        oim_ref[...] = ci_ref[...]


def _s5(u3, s_re, s_im, lw):
    b, t, _ = u3.shape
    shared = s_re.shape[0] == 1
    bp = -(-b // SUBLANES) * SUBLANES
    if bp != b:
        u3 = jnp.pad(u3, ((0, bp - b), (0, 0), (0, 0)))
        if not shared:
            s_re, s_im = (jnp.pad(s, ((0, bp - b), (0, 0))) for s in (s_re, s_im))
    nb = min(bp, S5_SEQS)
    rows = nb * S5_STEPS
    r = jnp.arange(rows)
    perm = ((r % S5_STEPS) * nb + r // S5_STEPS)[None, :] == r[:, None]
    perm, perm_t = perm.astype(BF16), perm.T.astype(BF16)
    state = pl.BlockSpec((1, S5_LANES), lambda i, c: (0, 0)) if shared else pl.BlockSpec((nb, S5_LANES), lambda i, c: (i, 0))
    state_out = pl.BlockSpec((nb, S5_LANES), lambda i, c: (i, 0))
    seq = pl.BlockSpec((nb, S5_STEPS, S5_WIDTH), lambda i, c: (i, c, 0))
    consts = (lw['s5_ab_re'], lw['s5_ab_im'], lw['s5_bb_re'], lw['s5_bb_im'], lw['s5_cc_re'],
              lw['s5_cc_im'], lw['d_s5'], perm, perm_t)
    y, o_re, o_im = pl.pallas_call(
        functools.partial(_s5_body, nb=nb),
        grid=(bp // nb, t // S5_STEPS),
        in_specs=[seq, state, state] + [_full(a.shape) for a in consts],
        out_specs=[seq, state_out, state_out],
        out_shape=[jax.ShapeDtypeStruct((bp, t, S5_WIDTH), BF16),
                   jax.ShapeDtypeStruct((bp, S5_LANES), F32),
                   jax.ShapeDtypeStruct((bp, S5_LANES), F32)],
        scratch_shapes=[pltpu.VMEM((S5_NBLK, rows, S5_BLK_LANES), F32), pltpu.VMEM((S5_NBLK, rows, S5_BLK_LANES), F32),
                        pltpu.VMEM((nb, S5_LANES), F32), pltpu.VMEM((nb, S5_LANES), F32)],
        compiler_params=_params("parallel", "arbitrary"),
        name="s5",
    )(u3, s_re, s_im, *consts)
    return y[:b], o_re[:b], o_im[:b]


SB_KEYS = 384
SB_CACHE_KEYS = 256
SB_PAIRS = SB_WIDTH // LANES
SB_DEAD = -105.0


def _attn_body(*refs, bq, n_hist, pad):
    if n_hist:
        q_ref, upper_ref, kn_ref, vn_ref, kh_ref, vh_ref, o_ref, kall_ref, vall_ref, acc_ref, carry_ref = refs
    else:
        q_ref, upper_ref, kn_ref, vn_ref, o_ref, kall_ref, vall_ref, acc_ref, carry_ref = refs
    qi = pl.program_id(1)

    @pl.when(qi == 0)
    def _():
        for all_ref, new_ref, hist_ref in ((kall_ref, kn_ref, kh_ref if n_hist else None),
                                           (vall_ref, vn_ref, vh_ref if n_hist else None)):
            if pad:
                all_ref[0:pad, :] = jnp.zeros((pad, SB_WIDTH), BF16)
            if n_hist:
                all_ref[pad:pad + n_hist, :] = hist_ref[0].astype(BF16)
            all_ref[pad + n_hist:, :] = new_ref[0]

    k_ref, v_ref = kall_ref, vall_ref
    q = q_ref[0]
    lane = lax.broadcasted_iota(jnp.int32, (bq, LANES), 1)
    left = lane < SB_HEAD_DIM
    zero_b = jnp.zeros((), BF16)
    qms = [jnp.concatenate([jnp.where(left, q[:, p * LANES:(p + 1) * LANES], zero_b),
                            jnp.where(left, zero_b, q[:, p * LANES:(p + 1) * LANES])], axis=0)
           for p in range(SB_PAIRS)]
    ri = lax.broadcasted_iota(jnp.int32, (2 * bq, SB_KEYS), 0)
    ci = lax.broadcasted_iota(jnp.int32, (2 * bq, SB_KEYS), 1)
    ri = jnp.where(ri >= bq, ri - bq, ri)
    strictly_earlier = ci - (SB_KEYS - bq) < ri
    upper = upper_ref[...]
    n_before = n_hist + (qi + 1) * bq
    end = pad + n_before
    trips = (n_before + SB_KEYS - 1) // SB_KEYS
    acc_ref[...] = jnp.zeros_like(acc_ref)
    carry_ref[...] = jnp.zeros_like(carry_ref)

    def visit(it, masked):
        rows = pl.ds(pl.multiple_of(end - (it + 1) * SB_KEYS, 16), SB_KEYS)
        _sb_visit([_dot_nt(qms[p], k_ref[rows, p * LANES:(p + 1) * LANES]) for p in range(SB_PAIRS)],
                  lambda p, w: _dot(w, v_ref[rows, p * LANES:(p + 1) * LANES]),
                  strictly_earlier if masked else None, upper, acc_ref, carry_ref, left, bq)

    def any_weight_left():
        c = carry_ref[0]
        for p in range(1, SB_PAIRS):
            c = jnp.maximum(c, carry_ref[p])
        return jnp.max(c) > SB_DEAD

    visit(0, True)

    def trip(state):
        it, _ = state
        visit(it, False)
        return it + 1, any_weight_left()

    lax.while_loop(lambda s: jnp.logical_and(s[0] < trips, s[1]), trip, (jnp.int32(1), any_weight_left()))
    o_ref[0] = acc_ref[...].astype(BF16)


BF16_ROWS = 16


def _attn(qb, kb, vb, kh, vh):
    b, t, _ = qb.shape
    bq = min(t, 128)
    n_hist = 0 if kh is None else kh.shape[1]
    assert n_hist % BF16_ROWS == 0 and bq % BF16_ROWS == 0
    short = max(-(n_hist + (qi + 1) * bq) % SB_KEYS for qi in range(t // bq))
    pad = -(-short // BF16_ROWS) * BF16_ROWS
    block = pl.BlockSpec((1, bq, SB_WIDTH), lambda i, j: (i, j, 0))
    new = pl.BlockSpec((1, t, SB_WIDTH), lambda i, j: (i, 0, 0))
    upper = jnp.tril(jnp.ones((SB_KEYS, SB_KEYS), BF16), -1)
    args, specs = [qb, upper, kb, vb], [block, _full(upper.shape), new, new]
    if n_hist:
        hmap = (lambda i, j: (0, 0, 0)) if kh.shape[0] == 1 else (lambda i, j: (i, 0, 0))
        args += [kh, vh]
        specs += [pl.BlockSpec((1, n_hist, SB_WIDTH), hmap)] * 2
    rows = pad + n_hist + t
    return pl.pallas_call(
        functools.partial(_attn_body, bq=bq, n_hist=n_hist, pad=pad),
        grid=(b, t // bq),
        in_specs=specs,
        out_specs=block,
        out_shape=jax.ShapeDtypeStruct((b, t, SB_WIDTH), BF16),
        scratch_shapes=[pltpu.VMEM((rows, SB_WIDTH), BF16), pltpu.VMEM((rows, SB_WIDTH), BF16),
                        pltpu.VMEM((bq, SB_WIDTH), F32), pltpu.VMEM((SB_PAIRS, 2 * bq, 1), F32)],
        compiler_params=_params("parallel", "arbitrary"),
        name="attn",
    )(*args)


def _sb_visit(zs, values, mask, upper, acc_ref, carry_ref, left, bq):
    log_ws, totals = [], []
    for z in zs:
        sp = jnp.log(1.0 + jnp.exp(-jnp.abs(z)))
        log_beta = jnp.minimum(z, 0.0) - sp
        log_keep = log_beta - z
        if mask is not None:
            log_keep = jnp.where(mask, log_keep, 0.0)
        hi = log_keep.astype(BF16)
        lo = (log_keep - hi.astype(F32)).astype(BF16)
        both = _dot(jnp.concatenate([hi, lo], axis=0), upper)
        later = both[:2 * bq] + both[2 * bq:]
        log_ws.append(log_beta + later)
        totals.append(later[:, 0:1] + log_keep[:, 0:1])
    for p in range(SB_PAIRS):
        carry = carry_ref[p]
        w = jnp.exp(log_ws[p] + carry)
        if mask is not None:
            w = jnp.where(mask, w, 0.0)
        pv = values(p, w.astype(BF16))
        carry_ref[p] = carry + totals[p]
        acc_ref[:, p * LANES:(p + 1) * LANES] += jnp.where(left, pv[:bq], pv[bq:])


def _attn_cached_body(q_ref, upper_ref, upper_new_ref, kn_ref, vn_ref, kh_ref, vh_ref, o_ref, acc_ref, carry_ref, *, bq, n_hist):
    q = q_ref[0]
    lane = lax.broadcasted_iota(jnp.int32, (bq, LANES), 1)
    left = lane < SB_HEAD_DIM
    zero_b = jnp.zeros((), BF16)
    qms = [jnp.concatenate([jnp.where(left, q[:, p * LANES:(p + 1) * LANES], zero_b),
                            jnp.where(left, zero_b, q[:, p * LANES:(p + 1) * LANES])], axis=0)
           for p in range(SB_PAIRS)]
    ri = lax.broadcasted_iota(jnp.int32, (2 * bq, bq), 0)
    ci = lax.broadcasted_iota(jnp.int32, (2 * bq, bq), 1)
    strictly_earlier = ci < jnp.where(ri >= bq, ri - bq, ri)
    upper = upper_ref[...]
    acc_ref[...] = jnp.zeros_like(acc_ref)
    carry_ref[...] = jnp.zeros_like(carry_ref)
    pair = lambda p: slice(p * LANES, (p + 1) * LANES)

    _sb_visit([_dot_nt(qms[p], kn_ref[0, :, pair(p)]) for p in range(SB_PAIRS)],
              lambda p, w: _dot(w, vn_ref[0, :, pair(p)]),
              strictly_earlier, upper_new_ref[...], acc_ref, carry_ref, left, bq)

    def any_weight_left():
        c = carry_ref[0]
        for p in range(1, SB_PAIRS):
            c = jnp.maximum(c, carry_ref[p])
        return jnp.max(c) > SB_DEAD

    def trip(state):
        it, _ = state
        window = pl.ds(pl.multiple_of(n_hist - (it + 1) * SB_CACHE_KEYS, SB_CACHE_KEYS), SB_CACHE_KEYS)
        kt = kh_ref[0, 0, :, :, window].reshape(SB_WIDTH, SB_CACHE_KEYS).astype(BF16)
        vt = vh_ref[0, 0, :, :, window].reshape(SB_WIDTH, SB_CACHE_KEYS).astype(BF16)
        _sb_visit([_dot(qms[p], kt[pair(p)]) for p in range(SB_PAIRS)],
                  lambda p, w: _dot_nt(w, vt[pair(p)]),
                  None, upper, acc_ref, carry_ref, left, bq)
        return it + 1, any_weight_left()

    lax.while_loop(lambda s: jnp.logical_and(s[0] < n_hist // SB_CACHE_KEYS, s[1]), trip,
                   (jnp.int32(0), any_weight_left()))
    o_ref[0] = acc_ref[...].astype(BF16)


def _attn_cached(qb, kb, vb, cache_k, cache_v, layer):
    b, t, _ = qb.shape
    n_hist = cache_k.shape[2]
    assert t <= 128 and t % BF16_ROWS == 0 and n_hist % SB_CACHE_KEYS == 0
    upper = jnp.tril(jnp.ones((SB_CACHE_KEYS, SB_CACHE_KEYS), BF16), -1)
    upper_new = jnp.tril(jnp.ones((t, t), BF16), -1)
    block = pl.BlockSpec((1, t, SB_WIDTH), lambda i: (i, 0, 0))
    hist =pl.BlockSpec((1, 1, SB_HEADS, SB_HEAD_DIM, n_hist), lambda i: (layer, i, 0, 0, 0))
    minor = lambda c: jnp.transpose(c, (0, 1, 3, 4, 2))
    return pl.pallas_call(
        functools.partial(_attn_cached_body, bq=t, n_hist=n_hist),
        grid=(b,),
        in_specs=[block, _full(upper.shape), _full(upper_new.shape), block, block, hist, hist],
        out_specs=block,
        out_shape=jax.ShapeDtypeStruct((b, t, SB_WIDTH), BF16),
        scratch_shapes=[pltpu.VMEM((t, SB_WIDTH), F32), pltpu.VMEM((SB_PAIRS, 2 * t, 1), F32)],
        compiler_params=_params("parallel"),
        name="attn_cached",
    )(qb, upper, upper_new, kb, vb, minor(cache_k), minor(cache_v))


def _kv_rows_body(*refs, b, t, depth):
    ko_ref, vo_ref = refs[4 * depth:]
    layer = pl.program_id(0) // b
    for l in range(depth):
        km_ref, vm_ref, kn_ref, vn_ref = refs[4 * l:4 * l + 4]
        n_meta = km_ref.shape[0]
        total = n_meta + t

        @pl.when(layer == l)
        def _(km_ref=km_ref, vm_ref=vm_ref, kn_ref=kn_ref, vn_ref=vn_ref, n_meta=n_meta, total=total):
            for meta_ref, new_ref, out_ref in ((km_ref, kn_ref, ko_ref), (vm_ref, vn_ref, vo_ref)):
                for c in range(-(-total // LANES)):
                    lo, hi = c * LANES, min((c + 1) * LANES, total)
                    parts = []
                    if lo < n_meta:
                        parts.append(meta_ref[lo:min(hi, n_meta), :])
                    if hi > n_meta:
                        parts.append(new_ref[max(lo, n_meta) - n_meta:hi - n_meta, :])
                    if hi - lo < LANES:
                        parts.append(jnp.zeros((LANES - (hi - lo), SB_WIDTH), F32))
                    blk = parts[0] if len(parts) == 1 else jnp.concatenate(parts, axis=0)
                    blk_t = blk.T.reshape(SB_HEADS, SB_HEAD_DIM, LANES)
                    out_ref[0, 0, :, :, lo:hi] = blk_t[:, :, :hi - lo]


def _kv_rows(meta_rows, new_rows, b):
    depth = len(new_rows)
    n = new_rows[0][0].shape[0]
    t = n // b
    n_meta = meta_rows[0][0].shape[0]
    args, specs = [], []
    for l in range(depth):
        rows = pl.BlockSpec((t, SB_WIDTH), lambda i, l=l: (jnp.clip(i - l * b, 0, b - 1), 0))
        args += [*meta_rows[l], *new_rows[l]]
        specs += [_full((n_meta, SB_WIDTH))] * 2 + [rows] * 2
    out = pl.BlockSpec((1, 1, SB_HEADS, SB_HEAD_DIM, n_meta + t), lambda i: (i // b, i % b, 0, 0, 0))
    shape = jax.ShapeDtypeStruct((depth, b, SB_HEADS, SB_HEAD_DIM, n_meta + t), F32)
    k_out, v_out = pl.pallas_call(
        functools.partial(_kv_rows_body, b=b, t=t, depth=depth),
        grid=(depth * b,),
        in_specs=specs,
        out_specs=[out, out],
        out_shape=[shape, shape],
        compiler_params=_params("arbitrary"),
        name="kv_rows",
    )(*args)
    return jnp.transpose(k_out, (0, 1, 4, 2, 3)), jnp.transpose(v_out, (0, 1, 4, 2, 3))


def _merge_body(x_ref, ya_ref, yb_ref, oc_ref, g_ref, wg_ref, wa_ref, wglu_ref, wc_ref, wo_ref, o_ref):
    x = x_ref[...]
    hb = _rmsnorm_rows(x, g_ref[...]).astype(BF16)
    gates = _sigmoid(_dot(hb, wg_ref[...]))
    glu = _dot(yb_ref[...], wglu_ref[...])
    mix = (gates[:, :D_MODEL] * _dot(ya_ref[...], wa_ref[...])
           + gates[:, D_MODEL:2 * D_MODEL] * (glu[:, :D_MODEL] * _sigmoid(glu[:, D_MODEL:]))
           + gates[:, 2 * D_MODEL:] * _dot(oc_ref[...], wc_ref[...]))
    o_ref[...] = x + _dot(mix.astype(BF16), wo_ref[...])


def _merge(x2, ya, yb, oc, lw):
    n = x2.shape[0]
    tm = _row_tile(n, 512)
    rows = lambda w: pl.BlockSpec((tm, w), lambda i: (i, 0))
    consts = (lw['norm_mix'], lw['wgate'], lw['w_lift_a'], lw['w_glu'], lw['w_lift_c'], lw['w_out'])
    return pl.pallas_call(
        _merge_body,
        grid=(n // tm,),
        in_specs=[rows(D_MODEL), rows(SSD_WIDTH), rows(S5_WIDTH), rows(SB_WIDTH)]
                 + [_full(a.shape) for a in consts],
        out_specs=rows(D_MODEL),
        out_shape=jax.ShapeDtypeStruct((n, D_MODEL), F32),
        compiler_params=_params("parallel"),
        name="merge",
    )(x2, ya, yb, oc, *consts)


FF_CHUNK = 1024


def _ffn_body(x_ref, g_ref, wu_ref, wd_ref, o_ref):
    x = x_ref[...]
    hb = _rmsnorm_rows(x, g_ref[...]).astype(BF16)
    acc = x
    for j in range(D_FF // FF_CHUNK):
        cols = slice(j * FF_CHUNK, (j + 1) * FF_CHUNK)
        up = jnp.maximum(_dot(hb, wu_ref[:, cols]), 0.0)
        acc = acc + _dot((up * up).astype(BF16), wd_ref[cols, :])
    o_ref[...] = acc


def _ffn(x2, lw):
    n = x2.shape[0]
    tm = _row_tile(n, 512)
    rows = pl.BlockSpec((tm, D_MODEL), lambda i: (i, 0))
    consts = (lw['norm_ffn'], lw['w_up'], lw['w_down'])
    return pl.pallas_call(
        _ffn_body,
        grid=(n // tm,),
        in_specs=[rows] + [_full(a.shape) for a in consts],
        out_specs=rows,
        out_shape=jax.ShapeDtypeStruct((n, D_MODEL), F32),
        compiler_params=_params("parallel"),
        name="ffn",
    )(x2, *consts)


def _block_diag(blocks):
    n, r, c = blocks.shape
    eye = jnp.eye(n, dtype=blocks.dtype)
    return (eye[:, None, :, None] * blocks[:, :, None, :]).reshape(n * r, n * c)


def _layer_weights(p, l):
    w_in = p['w_in'][l]
    col = lambda off, n: w_in[:, off:off + n].astype(BF16)
    row = lambda v: v.reshape(1, -1).astype(F32)
    heads = jnp.arange(SSD_WIDTH) // SSD_HEAD_DIM
    pad_lanes = lambda v: jnp.pad(v, (0, LANES - v.shape[0])).reshape(1, LANES)
    ab_re, ab_im, bb_re, bb_im = _s5_params(p['lam_re'][l], p['lam_im'][l], p['log_step'][l],
                                            p['b_re'][l], p['b_im'][l])
    to_bb = lambda t: jnp.stack([_block_diag(blk) for blk in
                                 t.reshape(S5_NBLK, S5_BLK_GROUPS, S5_GROUP_CH, S5_STATE)]).astype(BF16)
    to_cc = lambda t: jnp.stack([_block_diag(blk) for blk in
                                 jnp.transpose(t, (0, 2, 1)).reshape(S5_NBLK, S5_BLK_GROUPS, S5_STATE, S5_GROUP_CH)]
                                ).astype(BF16)
    state_rows = jnp.arange(LANES) // SSD_STATE
    state_cols = jnp.arange(SSD_WIDTH) // (SSD_WIDTH // SSD_GROUPS)
    return {
        'norm_mix': row(p['norm_mix'][l]),
        'wz': col(OFF_Z, SSD_WIDTH), 'wxbc': col(OFF_XBC, CONV_DIM),
        'wdt': jnp.pad(col(OFF_DT, SSD_HEADS), ((0, 0), (0, LANES - SSD_HEADS))),
        'wu': col(OFF_U, S5_WIDTH), 'wq': col(OFF_Q, SB_WIDTH), 'wk': col(OFF_K, SB_WIDTH),
        'wv': col(OFF_V, SB_WIDTH), 'wgate': col(OFF_GATE, N_BRANCH * D_MODEL),
        'q_norm': row(jnp.tile(p['q_norm'][l], SB_HEADS)), 'k_norm': row(jnp.tile(p['k_norm'][l], SB_HEADS)),
        'head_mean': (_block_diag(jnp.ones((SB_HEADS, SB_HEAD_DIM, SB_HEAD_DIM), F32)) / SB_HEAD_DIM).astype(BF16),
        'conv_w': jnp.pad(p['conv_w'][l], ((0, SUBLANES - CONV_W), (0, 0))),
        'conv_b': row(p['conv_b'][l]),
        'dt_bias': pad_lanes(p['dt_bias'][l]), 'a_log': pad_lanes(p['a_log'][l]),
        'd_ssd': row(p['d_ssd'][l][heads]), 'norm_ssd': row(p['norm_ssd'][l]),
        'eye': jnp.eye(LANES, dtype=BF16),
        'state_mask': (state_rows[:, None] == state_cols[None, :]).astype(F32),
        's5_ab_re': ab_re[::S5_GROUP_CH].reshape(1, S5_LANES), 's5_ab_im': ab_im[::S5_GROUP_CH].reshape(1, S5_LANES),
        's5_bb_re': to_bb(bb_re), 's5_bb_im': to_bb(bb_im),
        's5_cc_re': to_cc(p['c_re'][l]), 's5_cc_im': to_cc(p['c_im'][l]),
        'd_s5': row(p['d_s5'][l]),
        'w_glu': p['w_glu'][l].astype(BF16), 'w_lift_a': p['w_lift_a'][l].astype(BF16),
        'w_lift_c': p['w_lift_c'][l].astype(BF16), 'w_out': p['w_out'][l].astype(BF16),
        'norm_ffn': row(p['norm_ffn'][l]), 'w_up': p['w_up'][l].astype(BF16), 'w_down': p['w_down'][l].astype(BF16),
    }


def _trunk_layer(x, k_hist, v_hist, conv_hist, ssd_s0, s5_re0, s5_im0, lw, cache_layer=None):
    b, t, _ = x.shape
    x2 = x.reshape(b * t, D_MODEL)
    z, xbc, dt, u, k, v, qb, kb, vb = _proj(x2, lw)
    seq = lambda a: a.reshape(b, t, a.shape[-1])
    hist8 = jnp.pad(conv_hist.astype(F32), ((0, 0), (SUBLANES - (CONV_W - 1), 0), (0, 0)))
    y_a, ssd_new = _ssd(seq(z), seq(xbc), seq(dt), hist8,
                        ssd_s0.astype(F32).reshape(-1, SSD_HEADS, SSD_HEAD_DIM, SSD_STATE), lw)
    y_b, s5_re, s5_im = _s5(seq(u), s5_re0.reshape(-1, S5_LANES).astype(F32),
                            s5_im0.reshape(-1, S5_LANES).astype(F32), lw)
    if cache_layer is not None:
        o_c = _attn_cached(seq(qb), seq(kb), seq(vb), k_hist.astype(F32), v_hist.astype(F32), cache_layer)
    elif k_hist is None:
        o_c = _attn(seq(qb), seq(kb), seq(vb), None, None)
    else:
        o_c = _attn(seq(qb), seq(kb), seq(vb), k_hist.reshape(k_hist.shape[0], -1, SB_WIDTH).astype(F32),
                    v_hist.reshape(v_hist.shape[0], -1, SB_WIDTH).astype(F32))
    x2 = _merge(x2, y_a.reshape(b * t, SSD_WIDTH), y_b.reshape(b * t, S5_WIDTH), o_c.reshape(b * t, SB_WIDTH), lw)
    x2 = _ffn(x2, lw)
    conv_rows = jnp.concatenate([jnp.broadcast_to(conv_hist.astype(F32), (b, CONV_W - 1, CONV_DIM)),
                                 seq(xbc)[:, -(CONV_W - 1):]], axis=1)[:, -(CONV_W - 1):]
    return (x2.reshape(b, t, D_MODEL), k, v, conv_rows,
            ssd_new.reshape(b, SSD_GROUPS, SSD_HPG, SSD_HEAD_DIM, SSD_STATE),
            s5_re.reshape(b, S5_GROUPS, S5_STATE), s5_im.reshape(b, S5_GROUPS, S5_STATE))


def kernel(x_prompt, x_sample, cache_k, cache_v, state_conv, state_ssd, state_s5_re, state_s5_im, meta_tokens, norm_mix, w_in, conv_w, conv_b, dt_bias, a_log, d_ssd, norm_ssd, lam_re, lam_im, log_step, b_re, b_im, c_re, c_im, d_s5, w_glu, q_norm, k_norm, w_lift_a, w_lift_c, w_out, norm_ffn, w_up, w_down):
    p = dict(norm_mix=norm_mix, w_in=w_in, conv_w=conv_w, conv_b=conv_b, dt_bias=dt_bias, a_log=a_log,
             d_ssd=d_ssd, norm_ssd=norm_ssd, lam_re=lam_re, lam_im=lam_im, log_step=log_step, b_re=b_re,
             b_im=b_im, c_re=c_re, c_im=c_im, d_s5=d_s5, w_glu=w_glu, q_norm=q_norm, k_norm=k_norm,
             w_lift_a=w_lift_a, w_lift_c=w_lift_c, w_out=w_out, norm_ffn=norm_ffn, w_up=w_up, w_down=w_down)
    depth = w_in.shape[0]
    bp = x_prompt.shape[0]
    bs, ts, _ = x_sample.shape
    xm = meta_tokens.astype(x_prompt.dtype)[None]
    xp, xs = x_prompt, x_sample
    zeros = lambda *s: jnp.zeros(s, F32)
    kv_m, kv_p, kv_s, outs_p, outs_s = [], [], [], [], []
    for l in range(depth):
        lw = _layer_weights(p, l)
        xm, k_m, v_m, conv_m, ssd_m, s5re_m, s5im_m = _trunk_layer(
            xm, None, None, zeros(1, CONV_W - 1, CONV_DIM), zeros(1, SSD_GROUPS, SSD_HPG, SSD_HEAD_DIM, SSD_STATE),
            zeros(1, S5_GROUPS, S5_STATE), zeros(1, S5_GROUPS, S5_STATE), lw)
        xp, k_p, v_p, *st_p = _trunk_layer(xp, k_m[None], v_m[None], conv_m, ssd_m, s5re_m, s5im_m, lw)
        xs, k_s, v_s, *st_s = _trunk_layer(xs, cache_k, cache_v, state_conv[l], state_ssd[l],
                                           state_s5_re[l], state_s5_im[l], lw, cache_layer=l)
        kv_m.append((k_m, v_m))
        kv_p.append((k_p, v_p))
        kv_s.append((k_s, v_s))
        outs_p.append(st_p)
        outs_s.append(st_s)
    stk = lambda outs, i: jnp.stack([o[i] for o in outs], axis=0)
    k_prompt, v_prompt = _kv_rows(kv_m, kv_p, bp)
    sample_rows = lambda i: stk(kv_s, i).reshape(depth, bs, ts, SB_HEADS, SB_HEAD_DIM)
    return (xp, xs,
            k_prompt, v_prompt, stk(outs_p, 0), stk(outs_p, 1), stk(outs_p, 2), stk(outs_p, 3),
            sample_rows(0), sample_rows(1), stk(outs_s, 0), stk(outs_s, 1), stk(outs_s, 2), stk(outs_s, 3))
```

```python
import functools
import math

import jax
import jax.numpy as jnp
from jax import lax
from jax.experimental import pallas as pl
from jax.experimental.pallas import tpu as pltpu

F32 = jnp.float32
BF16 = jnp.bfloat16

D_MODEL = 1024
N_META = 16
RMS_EPS = 1e-6
SSD_HEADS = 16
SSD_HEAD_DIM = 64
SSD_GROUPS = 2
SSD_HPG = SSD_HEADS // SSD_GROUPS
SSD_STATE = 64
SSD_WIDTH = SSD_HEADS * SSD_HEAD_DIM
CONV_W = 4
N_BC = SSD_GROUPS * SSD_STATE
CONV_DIM = SSD_WIDTH + 2 * N_BC
S5_WIDTH = D_MODEL // 2
S5_GROUP_CH = 16
S5_GROUPS = S5_WIDTH // S5_GROUP_CH
S5_STATE = 64
S5_LANES = S5_GROUPS * S5_STATE
SB_HEADS = 8
SB_HEAD_DIM = 64
SB_WIDTH = SB_HEADS * SB_HEAD_DIM
SB_SCALE = 1.0 / math.sqrt(SB_HEAD_DIM)
N_BRANCH = 3
D_FF = 4 * D_MODEL
OFF_Z = 0
OFF_XBC = OFF_Z + SSD_WIDTH
OFF_DT = OFF_XBC + CONV_DIM
OFF_U = OFF_DT + SSD_HEADS
OFF_Q = OFF_U + S5_WIDTH
OFF_K = OFF_Q + SB_WIDTH
OFF_V = OFF_K + SB_WIDTH
OFF_GATE = OFF_V + SB_WIDTH

LANES = 128
SUBLANES = 8
VMEM_LIMIT = 56 * 1024 * 1024
S5_BLK_GROUPS = LANES // S5_GROUP_CH
S5_BLK_LANES = S5_BLK_GROUPS * S5_STATE
S5_NBLK = S5_GROUPS // S5_BLK_GROUPS


def _dot(a, b):
    return jnp.dot(a, b, preferred_element_type=F32)


def _dot_nt(a, b):
    return lax.dot_general(a, b, (((1,), (1,)), ((), ())), preferred_element_type=F32)


def _split(a, terms):
    out = []
    r = a
    for _ in range(terms):
        p = r.astype(BF16)
        out.append(p)
        r = r - p.astype(F32)
    return out


def _dot_split_lhs(a, b_bf16, terms=3):
    return sum(_dot(p, b_bf16) for p in _split(a, terms))


def _dot_split_rhs(a_bf16, b, terms=3):
    return sum(_dot(a_bf16, p) for p in _split(b, terms))


def _sigmoid(x):
    return 1.0 / (1.0 + jnp.exp(-x))


def _softplus(x):
    return jnp.maximum(x, 0.0) + jnp.log(1.0 + jnp.exp(-jnp.abs(x)))


def _rmsnorm_rows(x, g):
    return x * lax.rsqrt(jnp.mean(x * x, axis=-1, keepdims=True) + RMS_EPS) * g


def _params(*sem):
    return pltpu.CompilerParams(dimension_semantics=sem, vmem_limit_bytes=VMEM_LIMIT)


def _full(shape):
    n = len(shape)
    return pl.BlockSpec(shape, lambda *_: (0,) * n, pipeline_mode=pl.Buffered(1))


def _row_tile(n, cap=256):
    for t in (1024, 512, 256, 128, 64, 32, 16, 8):
        if t <= cap and n % t == 0:
            return t
    raise ValueError(f"token count {n} is not a multiple of {SUBLANES}")


def _proj_body(x_ref, g_ref, wz_ref, wxbc_ref, wdt_ref, wu_ref, wq_ref, wk_ref, wv_ref,
               qn_ref, kn_ref, hm_ref,
               z_ref, xbc_ref, dt_ref, u_ref, k_ref, v_ref, qb_ref, kb_ref, vb_ref, hb_ref):
    hb_ref[...] = _rmsnorm_rows(x_ref[...], g_ref[...]).astype(BF16)
    z_ref[...] = _dot(hb_ref[...], wz_ref[...])
    xbc_ref[...] = _dot(hb_ref[...], wxbc_ref[...])
    dt_ref[...] = _dot(hb_ref[...], wdt_ref[...])
    u_ref[...] = _dot(hb_ref[...], wu_ref[...])
    hm = hm_ref[...]

    def head_norm(t, w):
        return t * lax.rsqrt(_dot((t * t).astype(BF16), hm) + RMS_EPS) * w

    q = head_norm(_dot(hb_ref[...], wq_ref[...]), qn_ref[...])
    k = head_norm(_dot(hb_ref[...], wk_ref[...]), kn_ref[...])
    v = _dot(hb_ref[...], wv_ref[...])
    k_ref[...] = k
    v_ref[...] = v
    qb_ref[...] = (q * SB_SCALE).astype(BF16)
    kb_ref[...] = k.astype(BF16)
    vb_ref[...] = v.astype(BF16)


def _proj(x2, lw):
    n = x2.shape[0]
    tm = _row_tile(n, 512)
    rows = lambda w: pl.BlockSpec((tm, w), lambda i: (i, 0))
    widths = (SSD_WIDTH, CONV_DIM, LANES, S5_WIDTH, SB_WIDTH, SB_WIDTH, SB_WIDTH, SB_WIDTH, SB_WIDTH)
    dtypes = (F32, F32, F32, F32, F32, F32, BF16, BF16, BF16)
    ins = (lw['norm_mix'], lw['wz'], lw['wxbc'], lw['wdt'], lw['wu'], lw['wq'], lw['wk'], lw['wv'],
           lw['q_norm'], lw['k_norm'], lw['head_mean'])
    return pl.pallas_call(
        _proj_body,
        grid=(n // tm,),
        in_specs=[rows(D_MODEL)] + [_full(a.shape) for a in ins],
        out_specs=[rows(w) for w in widths],
        out_shape=[jax.ShapeDtypeStruct((n, w), d) for w, d in zip(widths, dtypes)],
        scratch_shapes=[pltpu.VMEM((tm, D_MODEL), BF16)],
        compiler_params=_params("parallel"),
        name="proj",
    )(x2, *ins)


def _ssd_body(z_ref, xbc_ref, dt_ref, hist_ref, s0_ref, cw_ref, cb_ref, dtb_ref, alog_ref,
              dskip_ref, nw_ref, eye_ref, smask_ref, tri_ref,
              y_ref, sout_ref, xp_ref, st_ref, yacc_ref, *, lc):
    c = pl.program_id(1)

    @pl.when(c == 0)
    def _():
        xp_ref[0:SUBLANES, :] = hist_ref[0]
        none = jnp.zeros((SSD_HEAD_DIM, SSD_STATE), F32)
        for hp in range(SSD_HEADS // 2):
            rows = [jnp.concatenate([s0_ref[0, h], none] if h < SSD_HPG else [none, s0_ref[0, h]], axis=1)
                    for h in (2 * hp, 2 * hp + 1)]
            st_ref[:, hp * LANES:(hp + 1) * LANES] = jnp.concatenate(rows, axis=0).T

    xp_ref[SUBLANES:SUBLANES + lc, :] = xbc_ref[0]
    conv = cb_ref[...]
    for k in range(CONV_W):
        lo = SUBLANES - (CONV_W - 1) + k
        conv = conv + xp_ref[lo:lo + lc, :] * cw_ref[k:k + 1, :]
    xp_ref[0:SUBLANES, :] = xp_ref[lc:lc + SUBLANES, :]
    act = conv * _sigmoid(conv)
    xs = act[:, :SSD_WIDTH]
    bm = act[:, SSD_WIDTH:SSD_WIDTH + N_BC]
    cm = act[:, SSD_WIDTH + N_BC:]

    eye = eye_ref[...]
    dt = _softplus(dt_ref[0] + dtb_ref[...])
    a = -jnp.exp(alog_ref[...])
    ri = lax.broadcasted_iota(jnp.int32, (lc, lc), 0)
    ci = lax.broadcasted_iota(jnp.int32, (lc, lc), 1)
    causal = ri >= ci
    acum = _dot_split_rhs(tri_ref[...], dt * a)
    if lc == LANES:
        transpose = lambda t: t.T
    else:
        transpose = lambda t: sum(_dot_nt(eye, p) for p in _split(t, 3))
    acum_t = transpose(acum)
    log_dt_t = jnp.log(transpose(dt))
    a_last = acum[lc - 1:lc, :]
    shifted_t = acum_t - log_dt_t
    to_end_t = jnp.exp(acum_t[:, lc - 1:lc] - shifted_t)
    dec = jnp.exp(a_last)

    xs_b = xs.astype(BF16)
    bm_b = bm.astype(BF16)
    cm_b = cm.astype(BF16)
    bm_t = bm.T if lc == LANES else _dot_nt(eye, bm_b)
    lane = lax.broadcasted_iota(jnp.int32, (lc, LANES), 1)
    left = lane < SSD_STATE
    left_sq = lax.broadcasted_iota(jnp.int32, (LANES, LANES), 1) < SSD_HEAD_DIM
    zero_b = jnp.zeros((), BF16)
    cbs = [_dot_nt(jnp.where(left if g == 0 else ~left, cm_b, zero_b), bm_b) for g in range(SSD_GROUPS)]
    st = st_ref[...]
    st_b = st.astype(BF16)
    smask = smask_ref[...]

    for hp in range(SSD_HEADS // 2):
        pair = slice(hp * LANES, (hp + 1) * LANES)
        rhs = jnp.concatenate([xs_b[:, pair], st_b[:, pair]], axis=0)
        outs, news = [], []
        for h in (2 * hp, 2 * hp + 1):
            col = jnp.broadcast_to(acum[:, h:h + 1], (lc, LANES))
            seg = col[:, :lc] - shifted_t[h:h + 1, :]
            m = jnp.where(causal, cbs[h // SSD_HPG] * jnp.exp(seg), 0.0)
            read = cm * jnp.exp(col)
            outs.append(_dot(jnp.concatenate([m.astype(BF16), read.astype(BF16)], axis=1), rhs))
            news.append(_dot((bm_t * to_end_t[h:h + 1, :]).astype(BF16), xs_b[:, pair]))
        yacc_ref[:, pair] = jnp.where(left, outs[0], outs[1])
        decay = jnp.where(left_sq, dec[:, 2 * hp:2 * hp + 1], dec[:, 2 * hp + 1:2 * hp + 2])
        st_ref[:, pair] = st[:, pair] * decay + smask[:, pair] * jnp.where(left_sq, news[0], news[1])

    y = (yacc_ref[...] + dskip_ref[...] * xs)
    zz = z_ref[0]
    y = y * (zz * _sigmoid(zz))
    gw = SSD_WIDTH // SSD_GROUPS
    parts = []
    for g in range(SSD_GROUPS):
        yg = y[:, g * gw:(g + 1) * gw]
        parts.append(yg * lax.rsqrt(jnp.mean(yg * yg, axis=-1, keepdims=True) + RMS_EPS))
    y_ref[0] = (jnp.concatenate(parts, axis=1) * nw_ref[...]).astype(BF16)

    @pl.when(c == pl.num_programs(1) - 1)
    def _():
        for hp in range(SSD_HEADS // 2):
            both = st_ref[:, hp * LANES:(hp + 1) * LANES].T
            for i, h in enumerate((2 * hp, 2 * hp + 1)):
                g = h // SSD_HPG
                sout_ref[0, h] = both[i * SSD_HEAD_DIM:(i + 1) * SSD_HEAD_DIM, g * SSD_STATE:(g + 1) * SSD_STATE]


def _ssd(z3, xbc3, dt3, hist, s0, lw):
    b, t, _ = z3.shape
    lc = min(t, 128)
    per_b = (lambda i, c: (0, 0, 0)) if hist.shape[0] == 1 else (lambda i, c: (i, 0, 0))
    per_b4 = (lambda i, c: (0, 0, 0, 0)) if hist.shape[0] == 1 else (lambda i, c: (i, 0, 0, 0))
    state = (1, SSD_HEADS, SSD_HEAD_DIM, SSD_STATE)
    seq = lambda w: pl.BlockSpec((1, lc, w), lambda i, c: (i, c, 0))
    consts = (lw['conv_w'], lw['conv_b'], lw['dt_bias'], lw['a_log'], lw['d_ssd'], lw['norm_ssd'],
              lw['eye'], lw['state_mask'], jnp.tril(jnp.ones((lc, lc), BF16)))
    return pl.pallas_call(
        functools.partial(_ssd_body, lc=lc),
        grid=(b, t // lc),
        in_specs=[seq(SSD_WIDTH), seq(CONV_DIM), seq(LANES),
                  pl.BlockSpec((1, SUBLANES, CONV_DIM), per_b),
                  pl.BlockSpec(state, per_b4)] + [_full(a.shape) for a in consts],
        out_specs=[seq(SSD_WIDTH), pl.BlockSpec(state, lambda i, c: (i, 0, 0, 0))],
        out_shape=[jax.ShapeDtypeStruct((b, t, SSD_WIDTH), BF16),
                   jax.ShapeDtypeStruct((b,) + state[1:], F32)],
        scratch_shapes=[pltpu.VMEM((lc + SUBLANES, CONV_DIM), F32),
                        pltpu.VMEM((LANES, SSD_WIDTH), F32),
                        pltpu.VMEM((lc, SSD_WIDTH), F32)],
        compiler_params=_params("parallel", "arbitrary"),
        name="ssd",
    )(z3, xbc3, dt3, hist, s0, *consts)


def _s5_param_body(lr_ref, li_ref, step_ref, br_ref, bi_ref, abr_ref, abi_ref, bbr_ref, bbi_ref):
    lr, li, step = lr_ref[...], li_ref[...], jnp.exp(step_ref[...])
    mag = jnp.exp(lr * step)
    ab_re = mag * jnp.cos(li * step)
    ab_im = mag * jnp.sin(li * step)
    den = lr * lr + li * li
    nr = ab_re - 1.0
    f_re = (nr * lr + ab_im * li) / den
    f_im = (ab_im * lr - nr * li) / den
    br, bi = br_ref[...], bi_ref[...]
    abr_ref[...] = ab_re
    abi_ref[...] = ab_im
    bbr_ref[...] = f_re * br - f_im * bi
    bbi_ref[...] = f_re * bi + f_im * br


def _s5_params(lam_re, lam_im, log_step, b_re, b_im):
    rep = lambda t: jnp.repeat(t, S5_GROUP_CH, axis=0)
    shape = (S5_WIDTH, S5_STATE)
    step = jnp.broadcast_to(rep(log_step[:, None]), shape)
    to_rows = lambda t: jnp.transpose(t, (0, 2, 1)).reshape(shape)
    outs = pl.pallas_call(
        _s5_param_body,
        out_shape=[jax.ShapeDtypeStruct(shape, F32)] * 4,
        name="s5_params",
    )(rep(lam_re), rep(lam_im), step, to_rows(b_re), to_rows(b_im))
    return outs


S5_STEPS = 16
S5_SEQS = 16
S5_SCAN_LANES = 256


def _s5_body(u_ref, sre_ref, sim_ref, abr_ref, abi_ref, bbr_ref, bbi_ref, ccr_ref, cci_ref, dsk_ref,
             perm_ref, perm_t_ref, y_ref, ore_ref, oim_ref, xre_ref, xim_ref, cr_ref, ci_ref, *, nb):
    c = pl.program_id(1)
    rows = nb * S5_STEPS

    @pl.when(c == 0)
    def _():
        cr_ref[...] = jnp.broadcast_to(sre_ref[...], cr_ref.shape)
        ci_ref[...] = jnp.broadcast_to(sim_ref[...], ci_ref.shape)

    u = u_ref[...].reshape(rows, S5_WIDTH)
    u_tm = _dot(perm_ref[...], u.astype(BF16)).astype(BF16)
    perm_t = perm_t_ref[...]
    for j in range(S5_NBLK):
        ub = u_tm[:, j * LANES:(j + 1) * LANES]
        xre_ref[j] = _dot(ub, bbr_ref[j])
        xim_ref[j] = _dot(ub, bbi_ref[j])
    for j in range(S5_NBLK):
        xre, xim = xre_ref.at[j], xim_ref.at[j]
        for h in range(S5_BLK_LANES // S5_SCAN_LANES):
            cols = slice(h * S5_SCAN_LANES, (h + 1) * S5_SCAN_LANES)
            lanes = slice(j * S5_BLK_LANES + h * S5_SCAN_LANES, j * S5_BLK_LANES + (h + 1) * S5_SCAN_LANES)
            ar, ai = abr_ref[:, lanes], abi_ref[:, lanes]
            cr, ci = cr_ref[:, lanes], ci_ref[:, lanes]
            for s in range(S5_STEPS):
                step = slice(s * nb, (s + 1) * nb)
                cr, ci = (ar * cr - ai * ci + xre[step, cols], ar * ci + ai * cr + xim[step, cols])
                xre[step, cols] = cr
                xim[step, cols] = ci
            cr_ref[:, lanes] = cr
            ci_ref[:, lanes] = ci
    for j in range(S5_NBLK):
        yx = _dot(xre_ref[j].astype(BF16), ccr_ref[j]) - _dot(xim_ref[j].astype(BF16), cci_ref[j])
        yb = _dot_split_rhs(perm_t, yx, 2) + dsk_ref[:, j * LANES:(j + 1) * LANES] * u[:, j * LANES:(j + 1) * LANES]
        gelu = 0.5 * yb * (1.0 + jnp.tanh(math.sqrt(2.0 / math.pi) * (yb + 0.044715 * (yb * yb * yb))))
        y_ref[:, :, j * LANES:(j + 1) * LANES] = gelu.astype(BF16).reshape(nb, S5_STEPS, LANES)

    @pl.when(c == pl.num_programs(1) - 1)
    def _():
        ore_ref[...] = cr_ref[...]
        oim_ref[...] = ci_ref[...]


def _s5(u3, s_re, s_im, lw):
    b, t, _ = u3.shape
    shared = s_re.shape[0] == 1
    bp = -(-b // SUBLANES) * SUBLANES
    if bp != b:
        u3 = jnp.pad(u3, ((0, bp - b), (0, 0), (0, 0)))
        if not shared:
            s_re, s_im = (jnp.pad(s, ((0, bp - b), (0, 0))) for s in (s_re, s_im))
    nb = min(bp, S5_SEQS)
    rows = nb * S5_STEPS
    r = jnp.arange(rows)
    perm = ((r % S5_STEPS) * nb + r // S5_STEPS)[None, :] == r[:, None]
    perm, perm_t = perm.astype(BF16), perm.T.astype(BF16)
    state = pl.BlockSpec((1, S5_LANES), lambda i, c: (0, 0)) if shared else pl.BlockSpec((nb, S5_LANES), lambda i, c: (i, 0))
    state_out = pl.BlockSpec((nb, S5_LANES), lambda i, c: (i, 0))
    seq = pl.BlockSpec((nb, S5_STEPS, S5_WIDTH), lambda i, c: (i, c, 0))
    consts = (lw['s5_ab_re'], lw['s5_ab_im'], lw['s5_bb_re'], lw['s5_bb_im'], lw['s5_cc_re'],
              lw['s5_cc_im'], lw['d_s5'], perm, perm_t)
    y, o_re, o_im = pl.pallas_call(
        functools.partial(_s5_body, nb=nb),
        grid=(bp // nb, t // S5_STEPS),
        in_specs=[seq, state, state] + [_full(a.shape) for a in consts],
        out_specs=[seq, state_out, state_out],
        out_shape=[jax.ShapeDtypeStruct((bp, t, S5_WIDTH), BF16),
                   jax.ShapeDtypeStruct((bp, S5_LANES), F32),
                   jax.ShapeDtypeStruct((bp, S5_LANES), F32)],
        scratch_shapes=[pltpu.VMEM((S5_NBLK, rows, S5_BLK_LANES), F32), pltpu.VMEM((S5_NBLK, rows, S5_BLK_LANES), F32),
                        pltpu.VMEM((nb, S5_LANES), F32), pltpu.VMEM((nb, S5_LANES), F32)],
        compiler_params=_params("parallel", "arbitrary"),
        name="s5",
    )(u3, s_re, s_im, *consts)
    return y[:b], o_re[:b], o_im[:b]


SB_KEYS = 384
SB_CACHE_KEYS = 256
SB_PAIRS = SB_WIDTH // LANES
SB_DEAD = -105.0


def _attn_body(*refs, bq, n_hist, pad):
    if n_hist:
        q_ref, upper_ref, kn_ref, vn_ref, kh_ref, vh_ref, o_ref, kall_ref, vall_ref, acc_ref, carry_ref = refs
    else:
        q_ref, upper_ref, kn_ref, vn_ref, o_ref, kall_ref, vall_ref, acc_ref, carry_ref = refs
    qi = pl.program_id(1)

    @pl.when(qi == 0)
    def _():
        for all_ref, new_ref, hist_ref in ((kall_ref, kn_ref, kh_ref if n_hist else None),
                                           (vall_ref, vn_ref, vh_ref if n_hist else None)):
            if pad:
                all_ref[0:pad, :] = jnp.zeros((pad, SB_WIDTH), BF16)
            if n_hist:
                all_ref[pad:pad + n_hist, :] = hist_ref[0].astype(BF16)
            all_ref[pad + n_hist:, :] = new_ref[0]

    k_ref, v_ref = kall_ref, vall_ref
    q = q_ref[0]
    lane = lax.broadcasted_iota(jnp.int32, (bq, LANES), 1)
    left = lane < SB_HEAD_DIM
    zero_b = jnp.zeros((), BF16)
    qms = [jnp.concatenate([jnp.where(left, q[:, p * LANES:(p + 1) * LANES], zero_b),
                            jnp.where(left, zero_b, q[:, p * LANES:(p + 1) * LANES])], axis=0)
           for p in range(SB_PAIRS)]
    ri = lax.broadcasted_iota(jnp.int32, (2 * bq, SB_KEYS), 0)
    ci = lax.broadcasted_iota(jnp.int32, (2 * bq, SB_KEYS), 1)
    ri = jnp.where(ri >= bq, ri - bq, ri)
    strictly_earlier = ci - (SB_KEYS - bq) < ri
    upper = upper_ref[...]
    n_before = n_hist + (qi + 1) * bq
    end = pad + n_before
    trips = (n_before + SB_KEYS - 1) // SB_KEYS
    acc_ref[...] = jnp.zeros_like(acc_ref)
    carry_ref[...] = jnp.zeros_like(carry_ref)

    def visit(it, masked):
        rows = pl.ds(pl.multiple_of(end - (it + 1) * SB_KEYS, 16), SB_KEYS)
        _sb_visit([_dot_nt(qms[p], k_ref[rows, p * LANES:(p + 1) * LANES]) for p in range(SB_PAIRS)],
                  lambda p, w: _dot(w, v_ref[rows, p * LANES:(p + 1) * LANES]),
                  strictly_earlier if masked else None, upper, acc_ref, carry_ref, left, bq)

    def any_weight_left():
        c = carry_ref[0]
        for p in range(1, SB_PAIRS):
            c = jnp.maximum(c, carry_ref[p])
        return jnp.max(c) > SB_DEAD

    visit(0, True)

    def trip(state):
        it, _ = state
        visit(it, False)
        return it + 1, any_weight_left()

    lax.while_loop(lambda s: jnp.logical_and(s[0] < trips, s[1]), trip, (jnp.int32(1), any_weight_left()))
    o_ref[0] = acc_ref[...].astype(BF16)


BF16_ROWS = 16


def _attn(qb, kb, vb, kh, vh):
    b, t, _ = qb.shape
    bq = min(t, 128)
    n_hist = 0 if kh is None else kh.shape[1]
    assert n_hist % BF16_ROWS == 0 and bq % BF16_ROWS == 0
    short = max(-(n_hist + (qi + 1) * bq) % SB_KEYS for qi in range(t // bq))
    pad = -(-short // BF16_ROWS) * BF16_ROWS
    block = pl.BlockSpec((1, bq, SB_WIDTH), lambda i, j: (i, j, 0))
    new = pl.BlockSpec((1, t, SB_WIDTH), lambda i, j: (i, 0, 0))
    upper = jnp.tril(jnp.ones((SB_KEYS, SB_KEYS), BF16), -1)
    args, specs = [qb, upper, kb, vb], [block, _full(upper.shape), new, new]
    if n_hist:
        hmap = (lambda i, j: (0, 0, 0)) if kh.shape[0] == 1 else (lambda i, j: (i, 0, 0))
        args += [kh, vh]
        specs += [pl.BlockSpec((1, n_hist, SB_WIDTH), hmap)] * 2
    rows = pad + n_hist + t
    return pl.pallas_call(
        functools.partial(_attn_body, bq=bq, n_hist=n_hist, pad=pad),
        grid=(b, t // bq),
        in_specs=specs,
        out_specs=block,
        out_shape=jax.ShapeDtypeStruct((b, t, SB_WIDTH), BF16),
        scratch_shapes=[pltpu.VMEM((rows, SB_WIDTH), BF16), pltpu.VMEM((rows, SB_WIDTH), BF16),
                        pltpu.VMEM((bq, SB_WIDTH), F32), pltpu.VMEM((SB_PAIRS, 2 * bq, 1), F32)],
        compiler_params=_params("parallel", "arbitrary"),
        name="attn",
    )(*args)


def _sb_visit(zs, values, mask, upper, acc_ref, carry_ref, left, bq):
    log_ws, totals = [], []
    for z in zs:
        sp = jnp.log(1.0 + jnp.exp(-jnp.abs(z)))
        log_beta = jnp.minimum(z, 0.0) - sp
        log_keep = log_beta - z
        if mask is not None:
            log_keep = jnp.where(mask, log_keep, 0.0)
        hi = log_keep.astype(BF16)
        lo = (log_keep - hi.astype(F32)).astype(BF16)
        both = _dot(jnp.concatenate([hi, lo], axis=0), upper)
        later = both[:2 * bq] + both[2 * bq:]
        log_ws.append(log_beta + later)
        totals.append(later[:, 0:1] + log_keep[:, 0:1])
    for p in range(SB_PAIRS):
        carry = carry_ref[p]
        w = jnp.exp(log_ws[p] + carry)
        if mask is not None:
            w = jnp.where(mask, w, 0.0)
        pv = values(p, w.astype(BF16))
        carry_ref[p] = carry + totals[p]
        acc_ref[:, p * LANES:(p + 1) * LANES] += jnp.where(left, pv[:bq], pv[bq:])


def _attn_cached_body(q_ref, upper_ref, upper_new_ref, kn_ref, vn_ref, kh_ref, vh_ref, o_ref, acc_ref, carry_ref, *, bq, n_hist):
    q = q_ref[0]
    lane = lax.broadcasted_iota(jnp.int32, (bq, LANES), 1)
    left = lane < SB_HEAD_DIM
    zero_b = jnp.zeros((), BF16)
    qms = [jnp.concatenate([jnp.where(left, q[:, p * LANES:(p + 1) * LANES], zero_b),
                            jnp.where(left, zero_b, q[:, p * LANES:(p + 1) * LANES])], axis=0)
           for p in range(SB_PAIRS)]
    ri = lax.broadcasted_iota(jnp.int32, (2 * bq, bq), 0)
    ci = lax.broadcasted_iota(jnp.int32, (2 * bq, bq), 1)
    strictly_earlier = ci < jnp.where(ri >= bq, ri - bq, ri)
    upper = upper_ref[...]
    acc_ref[...] = jnp.zeros_like(acc_ref)
    carry_ref[...] = jnp.zeros_like(carry_ref)
    pair = lambda p: slice(p * LANES, (p + 1) * LANES)

    _sb_visit([_dot_nt(qms[p], kn_ref[0, :, pair(p)]) for p in range(SB_PAIRS)],
              lambda p, w: _dot(w, vn_ref[0, :, pair(p)]),
              strictly_earlier, upper_new_ref[...], acc_ref, carry_ref, left, bq)

    def any_weight_left():
        c = carry_ref[0]
        for p in range(1, SB_PAIRS):
            c = jnp.maximum(c, carry_ref[p])
        return jnp.max(c) > SB_DEAD

    def trip(state):
        it, _ = state
        window = pl.ds(pl.multiple_of(n_hist - (it + 1) * SB_CACHE_KEYS, SB_CACHE_KEYS), SB_CACHE_KEYS)
        kt = kh_ref[0, 0, :, :, window].reshape(SB_WIDTH, SB_CACHE_KEYS).astype(BF16)
        vt = vh_ref[0, 0, :, :, window].reshape(SB_WIDTH, SB_CACHE_KEYS).astype(BF16)
        _sb_visit([_dot(qms[p], kt[pair(p)]) for p in range(SB_PAIRS)],
                  lambda p, w: _dot_nt(w, vt[pair(p)]),
                  None, upper, acc_ref, carry_ref, left, bq)
        return it + 1, any_weight_left()

    lax.while_loop(lambda s: jnp.logical_and(s[0] < n_hist // SB_CACHE_KEYS, s[1]), trip,
                   (jnp.int32(0), any_weight_left()))
    o_ref[0] = acc_ref[...].astype(BF16)


def _attn_cached(qb, kb, vb, cache_k, cache_v, layer):
    b, t, _ = qb.shape
    n_hist = cache_k.shape[2]
    assert t <= 128 and t % BF16_ROWS == 0 and n_hist % SB_CACHE_KEYS == 0
    upper = jnp.tril(jnp.ones((SB_CACHE_KEYS, SB_CACHE_KEYS), BF16), -1)
    upper_new = jnp.tril(jnp.ones((t, t), BF16), -1)
    block = pl.BlockSpec((1, t, SB_WIDTH), lambda i: (i, 0, 0))
    hist =pl.BlockSpec((1, 1, SB_HEADS, SB_HEAD_DIM, n_hist), lambda i: (layer, i, 0, 0, 0))
    minor = lambda c: jnp.transpose(c, (0, 1, 3, 4, 2))
    return pl.pallas_call(
        functools.partial(_attn_cached_body, bq=t, n_hist=n_hist),
        grid=(b,),
        in_specs=[block, _full(upper.shape), _full(upper_new.shape), block, block, hist, hist],
        out_specs=block,
        out_shape=jax.ShapeDtypeStruct((b, t, SB_WIDTH), BF16),
        scratch_shapes=[pltpu.VMEM((t, SB_WIDTH), F32), pltpu.VMEM((SB_PAIRS, 2 * t, 1), F32)],
        compiler_params=_params("parallel"),
        name="attn_cached",
    )(qb, upper, upper_new, kb, vb, minor(cache_k), minor(cache_v))


def _kv_rows_body(*refs, b, t, depth):
    ko_ref, vo_ref = refs[4 * depth:]
    layer = pl.program_id(0) // b
    for l in range(depth):
        km_ref, vm_ref, kn_ref, vn_ref = refs[4 * l:4 * l + 4]
        n_meta = km_ref.shape[0]
        total = n_meta + t

        @pl.when(layer == l)
        def _(km_ref=km_ref, vm_ref=vm_ref, kn_ref=kn_ref, vn_ref=vn_ref, n_meta=n_meta, total=total):
            for meta_ref, new_ref, out_ref in ((km_ref, kn_ref, ko_ref), (vm_ref, vn_ref, vo_ref)):
                for c in range(-(-total // LANES)):
                    lo, hi = c * LANES, min((c + 1) * LANES, total)
                    parts = []
                    if lo < n_meta:
                        parts.append(meta_ref[lo:min(hi, n_meta), :])
                    if hi > n_meta:
                        parts.append(new_ref[max(lo, n_meta) - n_meta:hi - n_meta, :])
                    if hi - lo < LANES:
                        parts.append(jnp.zeros((LANES - (hi - lo), SB_WIDTH), F32))
                    blk = parts[0] if len(parts) == 1 else jnp.concatenate(parts, axis=0)
                    blk_t = blk.T.reshape(SB_HEADS, SB_HEAD_DIM, LANES)
                    out_ref[0, 0, :, :, lo:hi] = blk_t[:, :, :hi - lo]


def _kv_rows(meta_rows, new_rows, b):
    depth = len(new_rows)
    n = new_rows[0][0].shape[0]
    t = n // b
    n_meta = meta_rows[0][0].shape[0]
    args, specs = [], []
    for l in range(depth):
        rows = pl.BlockSpec((t, SB_WIDTH), lambda i, l=l: (jnp.clip(i - l * b, 0, b - 1), 0))
        args += [*meta_rows[l], *new_rows[l]]
        specs += [_full((n_meta, SB_WIDTH))] * 2 + [rows] * 2
    out = pl.BlockSpec((1, 1, SB_HEADS, SB_HEAD_DIM, n_meta + t), lambda i: (i // b, i % b, 0, 0, 0))
    shape = jax.ShapeDtypeStruct((depth, b, SB_HEADS, SB_HEAD_DIM, n_meta + t), F32)
    k_out, v_out = pl.pallas_call(
        functools.partial(_kv_rows_body, b=b, t=t, depth=depth),
        grid=(depth * b,),
        in_specs=specs,
        out_specs=[out, out],
        out_shape=[shape, shape],
        compiler_params=_params("arbitrary"),
        name="kv_rows",
    )(*args)
    return jnp.transpose(k_out, (0, 1, 4, 2, 3)), jnp.transpose(v_out, (0, 1, 4, 2, 3))


def _merge_body(x_ref, ya_ref, yb_ref, oc_ref, g_ref, wg_ref, wa_ref, wglu_ref, wc_ref, wo_ref, o_ref):
    x = x_ref[...]
    hb = _rmsnorm_rows(x, g_ref[...]).astype(BF16)
    gates = _sigmoid(_dot(hb, wg_ref[...]))
    glu = _dot(yb_ref[...], wglu_ref[...])
    mix = (gates[:, :D_MODEL] * _dot(ya_ref[...], wa_ref[...])
           + gates[:, D_MODEL:2 * D_MODEL] * (glu[:, :D_MODEL] * _sigmoid(glu[:, D_MODEL:]))
           + gates[:, 2 * D_MODEL:] * _dot(oc_ref[...], wc_ref[...]))
    o_ref[...] = x + _dot(mix.astype(BF16), wo_ref[...])


def _merge(x2, ya, yb, oc, lw):
    n = x2.shape[0]
    tm = _row_tile(n, 512)
    rows = lambda w: pl.BlockSpec((tm, w), lambda i: (i, 0))
    consts = (lw['norm_mix'], lw['wgate'], lw['w_lift_a'], lw['w_glu'], lw['w_lift_c'], lw['w_out'])
    return pl.pallas_call(
        _merge_body,
        grid=(n // tm,),
        in_specs=[rows(D_MODEL), rows(SSD_WIDTH), rows(S5_WIDTH), rows(SB_WIDTH)]
                 + [_full(a.shape) for a in consts],
        out_specs=rows(D_MODEL),
        out_shape=jax.ShapeDtypeStruct((n, D_MODEL), F32),
        compiler_params=_params("parallel"),
        name="merge",
    )(x2, ya, yb, oc, *consts)


FF_CHUNK = 1024


def _ffn_body(x_ref, g_ref, wu_ref, wd_ref, o_ref):
    x = x_ref[...]
    hb = _rmsnorm_rows(x, g_ref[...]).astype(BF16)
    acc = x
    for j in range(D_FF // FF_CHUNK):
        cols = slice(j * FF_CHUNK, (j + 1) * FF_CHUNK)
        up = jnp.maximum(_dot(hb, wu_ref[:, cols]), 0.0)
        acc = acc + _dot((up * up).astype(BF16), wd_ref[cols, :])
    o_ref[...] = acc


def _ffn(x2, lw):
    n = x2.shape[0]
    tm = _row_tile(n, 1024)
    rows = pl.BlockSpec((tm, D_MODEL), lambda i: (i, 0))
    consts = (lw['norm_ffn'], lw['w_up'], lw['w_down'])
    return pl.pallas_call(
        _ffn_body,
        grid=(n // tm,),
        in_specs=[rows] + [_full(a.shape) for a in consts],
        out_specs=rows,
        out_shape=jax.ShapeDtypeStruct((n, D_MODEL), F32),
        compiler_params=_params("parallel"),
        name="ffn",
    )(x2, *consts)


def _block_diag(blocks):
    n, r, c = blocks.shape
    eye = jnp.eye(n, dtype=blocks.dtype)
    return (eye[:, None, :, None] * blocks[:, :, None, :]).reshape(n * r, n * c)


def _layer_weights(p, l):
    w_in = p['w_in'][l]
    col = lambda off, n: w_in[:, off:off + n].astype(BF16)
    row = lambda v: v.reshape(1, -1).astype(F32)
    heads = jnp.arange(SSD_WIDTH) // SSD_HEAD_DIM
    pad_lanes = lambda v: jnp.pad(v, (0, LANES - v.shape[0])).reshape(1, LANES)
    ab_re, ab_im, bb_re, bb_im = _s5_params(p['lam_re'][l], p['lam_im'][l], p['log_step'][l],
                                            p['b_re'][l], p['b_im'][l])
    to_bb = lambda t: jnp.stack([_block_diag(blk) for blk in
                                 t.reshape(S5_NBLK, S5_BLK_GROUPS, S5_GROUP_CH, S5_STATE)]).astype(BF16)
    to_cc = lambda t: jnp.stack([_block_diag(blk) for blk in
                                 jnp.transpose(t, (0, 2, 1)).reshape(S5_NBLK, S5_BLK_GROUPS, S5_STATE, S5_GROUP_CH)]
                                ).astype(BF16)
    state_rows = jnp.arange(LANES) // SSD_STATE
    state_cols = jnp.arange(SSD_WIDTH) // (SSD_WIDTH // SSD_GROUPS)
    return {
        'norm_mix': row(p['norm_mix'][l]),
        'wz': col(OFF_Z, SSD_WIDTH), 'wxbc': col(OFF_XBC, CONV_DIM),
        'wdt': jnp.pad(col(OFF_DT, SSD_HEADS), ((0, 0), (0, LANES - SSD_HEADS))),
        'wu': col(OFF_U, S5_WIDTH), 'wq': col(OFF_Q, SB_WIDTH), 'wk': col(OFF_K, SB_WIDTH),
        'wv': col(OFF_V, SB_WIDTH), 'wgate': col(OFF_GATE, N_BRANCH * D_MODEL),
        'q_norm': row(jnp.tile(p['q_norm'][l], SB_HEADS)), 'k_norm': row(jnp.tile(p['k_norm'][l], SB_HEADS)),
        'head_mean': (_block_diag(jnp.ones((SB_HEADS, SB_HEAD_DIM, SB_HEAD_DIM), F32)) / SB_HEAD_DIM).astype(BF16),
        'conv_w': jnp.pad(p['conv_w'][l], ((0, SUBLANES - CONV_W), (0, 0))),
        'conv_b': row(p['conv_b'][l]),
        'dt_bias': pad_lanes(p['dt_bias'][l]), 'a_log': pad_lanes(p['a_log'][l]),
        'd_ssd': row(p['d_ssd'][l][heads]), 'norm_ssd': row(p['norm_ssd'][l]),
        'eye': jnp.eye(LANES, dtype=BF16),
        'state_mask': (state_rows[:, None] == state_cols[None, :]).astype(F32),
        's5_ab_re': ab_re[::S5_GROUP_CH].reshape(1, S5_LANES), 's5_ab_im': ab_im[::S5_GROUP_CH].reshape(1, S5_LANES),
        's5_bb_re': to_bb(bb_re), 's5_bb_im': to_bb(bb_im),
        's5_cc_re': to_cc(p['c_re'][l]), 's5_cc_im': to_cc(p['c_im'][l]),
        'd_s5': row(p['d_s5'][l]),
        'w_glu': p['w_glu'][l].astype(BF16), 'w_lift_a': p['w_lift_a'][l].astype(BF16),
        'w_lift_c': p['w_lift_c'][l].astype(BF16), 'w_out': p['w_out'][l].astype(BF16),
        'norm_ffn': row(p['norm_ffn'][l]), 'w_up': p['w_up'][l].astype(BF16), 'w_down': p['w_down'][l].astype(BF16),
    }


def _trunk_layer(x, k_hist, v_hist, conv_hist, ssd_s0, s5_re0, s5_im0, lw, cache_layer=None):
    b, t, _ = x.shape
    x2 = x.reshape(b * t, D_MODEL)
    z, xbc, dt, u, k, v, qb, kb, vb = _proj(x2, lw)
    seq = lambda a: a.reshape(b, t, a.shape[-1])
    hist8 = jnp.pad(conv_hist.astype(F32), ((0, 0), (SUBLANES - (CONV_W - 1), 0), (0, 0)))
    y_a, ssd_new = _ssd(seq(z), seq(xbc), seq(dt), hist8,
                        ssd_s0.astype(F32).reshape(-1, SSD_HEADS, SSD_HEAD_DIM, SSD_STATE), lw)
    y_b, s5_re, s5_im = _s5(seq(u), s5_re0.reshape(-1, S5_LANES).astype(F32),
                            s5_im0.reshape(-1, S5_LANES).astype(F32), lw)
    if cache_layer is not None:
        o_c = _attn_cached(seq(qb), seq(kb), seq(vb), k_hist.astype(F32), v_hist.astype(F32), cache_layer)
    elif k_hist is None:
        o_c = _attn(seq(qb), seq(kb), seq(vb), None, None)
    else:
        o_c = _attn(seq(qb), seq(kb), seq(vb), k_hist.reshape(k_hist.shape[0], -1, SB_WIDTH).astype(F32),
                    v_hist.reshape(v_hist.shape[0], -1, SB_WIDTH).astype(F32))
    x2 = _merge(x2, y_a.reshape(b * t, SSD_WIDTH), y_b.reshape(b * t, S5_WIDTH), o_c.reshape(b * t, SB_WIDTH), lw)
    x2 = _ffn(x2, lw)
    conv_rows = jnp.concatenate([jnp.broadcast_to(conv_hist.astype(F32), (b, CONV_W - 1, CONV_DIM)),
                                 seq(xbc)[:, -(CONV_W - 1):]], axis=1)[:, -(CONV_W - 1):]
    return (x2.reshape(b, t, D_MODEL), k, v, conv_rows,
            ssd_new.reshape(b, SSD_GROUPS, SSD_HPG, SSD_HEAD_DIM, SSD_STATE),
            s5_re.reshape(b, S5_GROUPS, S5_STATE), s5_im.reshape(b, S5_GROUPS, S5_STATE))


def kernel(x_prompt, x_sample, cache_k, cache_v, state_conv, state_ssd, state_s5_re, state_s5_im, meta_tokens, norm_mix, w_in, conv_w, conv_b, dt_bias, a_log, d_ssd, norm_ssd, lam_re, lam_im, log_step, b_re, b_im, c_re, c_im, d_s5, w_glu, q_norm, k_norm, w_lift_a, w_lift_c, w_out, norm_ffn, w_up, w_down):
    p = dict(norm_mix=norm_mix, w_in=w_in, conv_w=conv_w, conv_b=conv_b, dt_bias=dt_bias, a_log=a_log,
             d_ssd=d_ssd, norm_ssd=norm_ssd, lam_re=lam_re, lam_im=lam_im, log_step=log_step, b_re=b_re,
             b_im=b_im, c_re=c_re, c_im=c_im, d_s5=d_s5, w_glu=w_glu, q_norm=q_norm, k_norm=k_norm,
             w_lift_a=w_lift_a, w_lift_c=w_lift_c, w_out=w_out, norm_ffn=norm_ffn, w_up=w_up, w_down=w_down)
    depth = w_in.shape[0]
    bp = x_prompt.shape[0]
    bs, ts, _ = x_sample.shape
    xm = meta_tokens.astype(x_prompt.dtype)[None]
    xp, xs = x_prompt, x_sample
    zeros = lambda *s: jnp.zeros(s, F32)
    kv_m, kv_p, kv_s, outs_p, outs_s = [], [], [], [], []
    for l in range(depth):
        lw = _layer_weights(p, l)
        xm, k_m, v_m, conv_m, ssd_m, s5re_m, s5im_m = _trunk_layer(
            xm, None, None, zeros(1, CONV_W - 1, CONV_DIM), zeros(1, SSD_GROUPS, SSD_HPG, SSD_HEAD_DIM, SSD_STATE),
            zeros(1, S5_GROUPS, S5_STATE), zeros(1, S5_GROUPS, S5_STATE), lw)
        xp, k_p, v_p, *st_p = _trunk_layer(xp, k_m[None], v_m[None], conv_m, ssd_m, s5re_m, s5im_m, lw)
        xs, k_s, v_s, *st_s = _trunk_layer(xs, cache_k, cache_v, state_conv[l], state_ssd[l],
                                           state_s5_re[l], state_s5_im[l], lw, cache_layer=l)
        kv_m.append((k_m, v_m))
        kv_p.append((k_p, v_p))
        kv_s.append((k_s, v_s))
        outs_p.append(st_p)
        outs_s.append(st_s)
    stk = lambda outs, i: jnp.stack([o[i] for o in outs], axis=0)
    k_prompt, v_prompt = _kv_rows(kv_m, kv_p, bp)
    sample_rows = lambda i: stk(kv_s, i).reshape(depth, bs, ts, SB_HEADS, SB_HEAD_DIM)
    return (xp, xs,
            k_prompt, v_prompt, stk(outs_p, 0), stk(outs_p, 1), stk(outs_p, 2), stk(outs_p, 3),
            sample_rows(0), sample_rows(1), stk(outs_s, 0), stk(outs_s, 1), stk(outs_s, 2), stk(outs_s, 3))
```

```python
import functools
import math

import jax
import jax.numpy as jnp
from jax import lax
from jax.experimental import pallas as pl
from jax.experimental.pallas import tpu as pltpu

F32 = jnp.float32
BF16 = jnp.bfloat16

D_MODEL = 1024
N_META = 16
RMS_EPS = 1e-6
SSD_HEADS = 16
SSD_HEAD_DIM = 64
SSD_GROUPS = 2
SSD_HPG = SSD_HEADS // SSD_GROUPS
SSD_STATE = 64
SSD_WIDTH = SSD_HEADS * SSD_HEAD_DIM
CONV_W = 4
N_BC = SSD_GROUPS * SSD_STATE
CONV_DIM = SSD_WIDTH + 2 * N_BC
S5_WIDTH = D_MODEL // 2
S5_GROUP_CH = 16
S5_GROUPS = S5_WIDTH // S5_GROUP_CH
S5_STATE = 64
S5_LANES = S5_GROUPS * S5_STATE
SB_HEADS = 8
SB_HEAD_DIM = 64
SB_WIDTH = SB_HEADS * SB_HEAD_DIM
SB_SCALE = 1.0 / math.sqrt(SB_HEAD_DIM)
N_BRANCH = 3
D_FF = 4 * D_MODEL
OFF_Z = 0
OFF_XBC = OFF_Z + SSD_WIDTH
OFF_DT = OFF_XBC + CONV_DIM
OFF_U = OFF_DT + SSD_HEADS
OFF_Q = OFF_U + S5_WIDTH
OFF_K = OFF_Q + SB_WIDTH
OFF_V = OFF_K + SB_WIDTH
OFF_GATE = OFF_V + SB_WIDTH

LANES = 128
SUBLANES = 8
VMEM_LIMIT = 56 * 1024 * 1024
S5_BLK_GROUPS = LANES // S5_GROUP_CH
S5_BLK_LANES = S5_BLK_GROUPS * S5_STATE
S5_NBLK = S5_GROUPS // S5_BLK_GROUPS


def _dot(a, b):
    return jnp.dot(a, b, preferred_element_type=F32)


def _dot_nt(a, b):
    return lax.dot_general(a, b, (((1,), (1,)), ((), ())), preferred_element_type=F32)


def _split(a, terms):
    out = []
    r = a
    for _ in range(terms):
        p = r.astype(BF16)
        out.append(p)
        r = r - p.astype(F32)
    return out


def _dot_split_lhs(a, b_bf16, terms=3):
    return sum(_dot(p, b_bf16) for p in _split(a, terms))


def _dot_split_rhs(a_bf16, b, terms=3):
    return sum(_dot(a_bf16, p) for p in _split(b, terms))


def _sigmoid(x):
    return 1.0 / (1.0 + jnp.exp(-x))


def _softplus(x):
    return jnp.maximum(x, 0.0) + jnp.log(1.0 + jnp.exp(-jnp.abs(x)))


def _rmsnorm_rows(x, g):
    return x * lax.rsqrt(jnp.mean(x * x, axis=-1, keepdims=True) + RMS_EPS) * g


def _params(*sem):
    return pltpu.CompilerParams(dimension_semantics=sem, vmem_limit_bytes=VMEM_LIMIT)


def _full(shape):
    n = len(shape)
    return pl.BlockSpec(shape, lambda *_: (0,) * n, pipeline_mode=pl.Buffered(1))


def _row_tile(n, cap=256):
    for t in (1024, 512, 256, 128, 64, 32, 16, 8):
        if t <= cap and n % t == 0:
            return t
    raise ValueError(f"token count {n} is not a multiple of {SUBLANES}")


def _proj_body(x_ref, g_ref, wz_ref, wxbc_ref, wdt_ref, wu_ref, wq_ref, wk_ref, wv_ref,
               qn_ref, kn_ref, hm_ref,
               z_ref, xbc_ref, dt_ref, u_ref, k_ref, v_ref, qb_ref, kb_ref, vb_ref, hb_ref):
    hb_ref[...] = _rmsnorm_rows(x_ref[...], g_ref[...]).astype(BF16)
    z_ref[...] = _dot(hb_ref[...], wz_ref[...])
    xbc_ref[...] = _dot(hb_ref[...], wxbc_ref[...])
    dt_ref[...] = _dot(hb_ref[...], wdt_ref[...])
    u_ref[...] = _dot(hb_ref[...], wu_ref[...])
    hm = hm_ref[...]

    def head_norm(t, w):
        return t * lax.rsqrt(_dot((t * t).astype(BF16), hm) + RMS_EPS) * w

    q = head_norm(_dot(hb_ref[...], wq_ref[...]), qn_ref[...])
    k = head_norm(_dot(hb_ref[...], wk_ref[...]), kn_ref[...])
    v = _dot(hb_ref[...], wv_ref[...])
    k_ref[...] = k
    v_ref[...] = v
    qb_ref[...] = (q * SB_SCALE).astype(BF16)
    kb_ref[...] = k.astype(BF16)
    vb_ref[...] = v.astype(BF16)


def _proj(x2, lw):
    n = x2.shape[0]
    tm = _row_tile(n, 512)
    rows = lambda w: pl.BlockSpec((tm, w), lambda i: (i, 0))
    widths = (SSD_WIDTH, CONV_DIM, LANES, S5_WIDTH, SB_WIDTH, SB_WIDTH, SB_WIDTH, SB_WIDTH, SB_WIDTH)
    dtypes = (F32, F32, F32, F32, F32, F32, BF16, BF16, BF16)
    ins = (lw['norm_mix'], lw['wz'], lw['wxbc'], lw['wdt'], lw['wu'], lw['wq'], lw['wk'], lw['wv'],
           lw['q_norm'], lw['k_norm'], lw['head_mean'])
    return pl.pallas_call(
        _proj_body,
        grid=(n // tm,),
        in_specs=[rows(D_MODEL)] + [_full(a.shape) for a in ins],
        out_specs=[rows(w) for w in widths],
        out_shape=[jax.ShapeDtypeStruct((n, w), d) for w, d in zip(widths, dtypes)],
        scratch_shapes=[pltpu.VMEM((tm, D_MODEL), BF16)],
        compiler_params=_params("parallel"),
        name="proj",
    )(x2, *ins)


def _ssd_body(z_ref, xbc_ref, dt_ref, hist_ref, s0_ref, cw_ref, cb_ref, dtb_ref, alog_ref,
              dskip_ref, nw_ref, eye_ref, smask_ref, tri_ref,
              y_ref, sout_ref, xp_ref, st_ref, yacc_ref, *, lc):
    c = pl.program_id(1)

    @pl.when(c == 0)
    def _():
        xp_ref[0:SUBLANES, :] = hist_ref[0]
        none = jnp.zeros((SSD_HEAD_DIM, SSD_STATE), F32)
        for hp in range(SSD_HEADS // 2):
            rows = [jnp.concatenate([s0_ref[0, h], none] if h < SSD_HPG else [none, s0_ref[0, h]], axis=1)
                    for h in (2 * hp, 2 * hp + 1)]
            st_ref[:, hp * LANES:(hp + 1) * LANES] = jnp.concatenate(rows, axis=0).T

    xp_ref[SUBLANES:SUBLANES + lc, :] = xbc_ref[0]
    conv = cb_ref[...]
    for k in range(CONV_W):
        lo = SUBLANES - (CONV_W - 1) + k
        conv = conv + xp_ref[lo:lo + lc, :] * cw_ref[k:k + 1, :]
    xp_ref[0:SUBLANES, :] = xp_ref[lc:lc + SUBLANES, :]
    act = conv * _sigmoid(conv)
    xs = act[:, :SSD_WIDTH]
    bm = act[:, SSD_WIDTH:SSD_WIDTH + N_BC]
    cm = act[:, SSD_WIDTH + N_BC:]

    eye = eye_ref[...]
    dt = _softplus(dt_ref[0] + dtb_ref[...])
    a = -jnp.exp(alog_ref[...])
    ri = lax.broadcasted_iota(jnp.int32, (lc, lc), 0)
    ci = lax.broadcasted_iota(jnp.int32, (lc, lc), 1)
    causal = ri >= ci
    acum = _dot_split_rhs(tri_ref[...], dt * a)
    if lc == LANES:
        transpose = lambda t: t.T
    else:
        transpose = lambda t: sum(_dot_nt(eye, p) for p in _split(t, 3))
    acum_t = transpose(acum)
    log_dt_t = jnp.log(transpose(dt))
    a_last = acum[lc - 1:lc, :]
    shifted_t = acum_t - log_dt_t
    to_end_t = jnp.exp(acum_t[:, lc - 1:lc] - shifted_t)
    dec = jnp.exp(a_last)

    xs_b = xs.astype(BF16)
    bm_b = bm.astype(BF16)
    cm_b = cm.astype(BF16)
    bm_t = bm.T if lc == LANES else _dot_nt(eye, bm_b)
    lane = lax.broadcasted_iota(jnp.int32, (lc, LANES), 1)
    left = lane < SSD_STATE
    left_sq = lax.broadcasted_iota(jnp.int32, (LANES, LANES), 1) < SSD_HEAD_DIM
    zero_b = jnp.zeros((), BF16)
    cbs = [_dot_nt(jnp.where(left if g == 0 else ~left, cm_b, zero_b), bm_b) for g in range(SSD_GROUPS)]
    st = st_ref[...]
    st_b = st.astype(BF16)
    smask = smask_ref[...]

    for hp in range(SSD_HEADS // 2):
        pair = slice(hp * LANES, (hp + 1) * LANES)
        rhs = jnp.concatenate([xs_b[:, pair], st_b[:, pair]], axis=0)
        outs, news = [], []
        for h in (2 * hp, 2 * hp + 1):
            col = jnp.broadcast_to(acum[:, h:h + 1], (lc, LANES))
            seg = col[:, :lc] - shifted_t[h:h + 1, :]
            m = jnp.where(causal, cbs[h // SSD_HPG] * jnp.exp(seg), 0.0)
            read = cm * jnp.exp(col)
            outs.append(_dot(jnp.concatenate([m.astype(BF16), read.astype(BF16)], axis=1), rhs))
            news.append(_dot((bm_t * to_end_t[h:h + 1, :]).astype(BF16), xs_b[:, pair]))
        yacc_ref[:, pair] = jnp.where(left, outs[0], outs[1])
        decay = jnp.where(left_sq, dec[:, 2 * hp:2 * hp + 1], dec[:, 2 * hp + 1:2 * hp + 2])
        st_ref[:, pair] = st[:, pair] * decay + smask[:, pair] * jnp.where(left_sq, news[0], news[1])

    y = (yacc_ref[...] + dskip_ref[...] * xs)
    zz = z_ref[0]
    y = y * (zz * _sigmoid(zz))
    gw = SSD_WIDTH // SSD_GROUPS
    parts = []
    for g in range(SSD_GROUPS):
        yg = y[:, g * gw:(g + 1) * gw]
        parts.append(yg * lax.rsqrt(jnp.mean(yg * yg, axis=-1, keepdims=True) + RMS_EPS))
    y_ref[0] = (jnp.concatenate(parts, axis=1) * nw_ref[...]).astype(BF16)

    @pl.when(c == pl.num_programs(1) - 1)
    def _():
        for hp in range(SSD_HEADS // 2):
            both = st_ref[:, hp * LANES:(hp + 1) * LANES].T
            for i, h in enumerate((2 * hp, 2 * hp + 1)):
                g = h // SSD_HPG
                sout_ref[0, h] = both[i * SSD_HEAD_DIM:(i + 1) * SSD_HEAD_DIM, g * SSD_STATE:(g + 1) * SSD_STATE]


def _ssd(z3, xbc3, dt3, hist, s0, lw):
    b, t, _ = z3.shape
    lc = min(t, 128)
    per_b = (lambda i, c: (0, 0, 0)) if hist.shape[0] == 1 else (lambda i, c: (i, 0, 0))
    per_b4 = (lambda i, c: (0, 0, 0, 0)) if hist.shape[0] == 1 else (lambda i, c: (i, 0, 0, 0))
    state = (1, SSD_HEADS, SSD_HEAD_DIM, SSD_STATE)
    seq = lambda w: pl.BlockSpec((1, lc, w), lambda i, c: (i, c, 0))
    consts = (lw['conv_w'], lw['conv_b'], lw['dt_bias'], lw['a_log'], lw['d_ssd'], lw['norm_ssd'],
              lw['eye'], lw['state_mask'], jnp.tril(jnp.ones((lc, lc), BF16)))
    return pl.pallas_call(
        functools.partial(_ssd_body, lc=lc),
        grid=(b, t // lc),
        in_specs=[seq(SSD_WIDTH), seq(CONV_DIM), seq(LANES),
                  pl.BlockSpec((1, SUBLANES, CONV_DIM), per_b),
                  pl.BlockSpec(state, per_b4)] + [_full(a.shape) for a in consts],
        out_specs=[seq(SSD_WIDTH), pl.BlockSpec(state, lambda i, c: (i, 0, 0, 0))],
        out_shape=[jax.ShapeDtypeStruct((b, t, SSD_WIDTH), BF16),
                   jax.ShapeDtypeStruct((b,) + state[1:], F32)],
        scratch_shapes=[pltpu.VMEM((lc + SUBLANES, CONV_DIM), F32),
                        pltpu.VMEM((LANES, SSD_WIDTH), F32),
                        pltpu.VMEM((lc, SSD_WIDTH), F32)],
        compiler_params=_params("parallel", "arbitrary"),
        name="ssd",
    )(z3, xbc3, dt3, hist, s0, *consts)


def _s5_param_body(lr_ref, li_ref, step_ref, br_ref, bi_ref, abr_ref, abi_ref, bbr_ref, bbi_ref):
    lr, li, step = lr_ref[...], li_ref[...], jnp.exp(step_ref[...])
    mag = jnp.exp(lr * step)
    ab_re = mag * jnp.cos(li * step)
    ab_im = mag * jnp.sin(li * step)
    den = lr * lr + li * li
    nr = ab_re - 1.0
    f_re = (nr * lr + ab_im * li) / den
    f_im = (ab_im * lr - nr * li) / den
    br, bi = br_ref[...], bi_ref[...]
    abr_ref[...] = ab_re
    abi_ref[...] = ab_im
    bbr_ref[...] = f_re * br - f_im * bi
    bbi_ref[...] = f_re * bi + f_im * br


def _s5_params(lam_re, lam_im, log_step, b_re, b_im):
    rep = lambda t: jnp.repeat(t, S5_GROUP_CH, axis=0)
    shape = (S5_WIDTH, S5_STATE)
    step = jnp.broadcast_to(rep(log_step[:, None]), shape)
    to_rows = lambda t: jnp.transpose(t, (0, 2, 1)).reshape(shape)
    outs = pl.pallas_call(
        _s5_param_body,
        out_shape=[jax.ShapeDtypeStruct(shape, F32)] * 4,
        name="s5_params",
    )(rep(lam_re), rep(lam_im), step, to_rows(b_re), to_rows(b_im))
    return outs


S5_STEPS = 16
S5_SEQS = 16
S5_SCAN_LANES = 256


def _s5_body(u_ref, sre_ref, sim_ref, abr_ref, abi_ref, bbr_ref, bbi_ref, ccr_ref, cci_ref, dsk_ref,
             perm_ref, perm_t_ref, y_ref, ore_ref, oim_ref, xre_ref, xim_ref, cr_ref, ci_ref, *, nb):
    c = pl.program_id(1)
    rows = nb * S5_STEPS

    @pl.when(c == 0)
    def _():
        cr_ref[...] = jnp.broadcast_to(sre_ref[...], cr_ref.shape)
        ci_ref[...] = jnp.broadcast_to(sim_ref[...], ci_ref.shape)

    u = u_ref[...].reshape(rows, S5_WIDTH)
    u_tm = _dot(perm_ref[...], u.astype(BF16)).astype(BF16)
    perm_t = perm_t_ref[...]
    for j in range(S5_NBLK):
        ub = u_tm[:, j * LANES:(j + 1) * LANES]
        xre_ref[j] = _dot(ub, bbr_ref[j])
        xim_ref[j] = _dot(ub, bbi_ref[j])
    for j in range(S5_NBLK):
        xre, xim = xre_ref.at[j], xim_ref.at[j]
        for h in range(S5_BLK_LANES // S5_SCAN_LANES):
            cols = slice(h * S5_SCAN_LANES, (h + 1) * S5_SCAN_LANES)
            lanes = slice(j * S5_BLK_LANES + h * S5_SCAN_LANES, j * S5_BLK_LANES + (h + 1) * S5_SCAN_LANES)
            ar, ai = abr_ref[:, lanes], abi_ref[:, lanes]
            cr, ci = cr_ref[:, lanes], ci_ref[:, lanes]
            for s in range(S5_STEPS):
                step = slice(s * nb, (s + 1) * nb)
                cr, ci = (ar * cr - ai * ci + xre[step, cols], ar * ci + ai * cr + xim[step, cols])
                xre[step, cols] = cr
                xim[step, cols] = ci
            cr_ref[:, lanes] = cr
            ci_ref[:, lanes] = ci
    for j in range(S5_NBLK):
        yx = _dot(xre_ref[j].astype(BF16), ccr_ref[j]) - _dot(xim_ref[j].astype(BF16), cci_ref[j])
        yb = _dot_split_rhs(perm_t, yx, 2) + dsk_ref[:, j * LANES:(j + 1) * LANES] * u[:, j * LANES:(j + 1) * LANES]
        gelu = 0.5 * yb * (1.0 + jnp.tanh(math.sqrt(2.0 / math.pi) * (yb + 0.044715 * (yb * yb * yb))))
        y_ref[:, :, j * LANES:(j + 1) * LANES] = gelu.astype(BF16).reshape(nb, S5_STEPS, LANES)

    @pl.when(c == pl.num_programs(1) - 1)
    def _():
        ore_ref[...] = cr_ref[...]
        oim_ref[...] = ci_ref[...]


def _s5(u3, s_re, s_im, lw):
    b, t, _ = u3.shape
    shared = s_re.shape[0] == 1
    bp = -(-b // SUBLANES) * SUBLANES
    if bp != b:
        u3 = jnp.pad(u3, ((0, bp - b), (0, 0), (0, 0)))
        if not shared:
            s_re, s_im = (jnp.pad(s, ((0, bp - b), (0, 0))) for s in (s_re, s_im))
    nb = min(bp, S5_SEQS)
    rows = nb * S5_STEPS
    r = jnp.arange(rows)
    perm = ((r % S5_STEPS) * nb + r // S5_STEPS)[None, :] == r[:, None]
    perm, perm_t = perm.astype(BF16), perm.T.astype(BF16)
    state = pl.BlockSpec((1, S5_LANES), lambda i, c: (0, 0)) if shared else pl.BlockSpec((nb, S5_LANES), lambda i, c: (i, 0))
    state_out = pl.BlockSpec((nb, S5_LANES), lambda i, c: (i, 0))
    seq = pl.BlockSpec((nb, S5_STEPS, S5_WIDTH), lambda i, c: (i, c, 0))
    consts = (lw['s5_ab_re'], lw['s5_ab_im'], lw['s5_bb_re'], lw['s5_bb_im'], lw['s5_cc_re'],
              lw['s5_cc_im'], lw['d_s5'], perm, perm_t)
    y, o_re, o_im = pl.pallas_call(
        functools.partial(_s5_body, nb=nb),
        grid=(bp // nb, t // S5_STEPS),
        in_specs=[seq, state, state] + [_full(a.shape) for a in consts],
        out_specs=[seq, state_out, state_out],
        out_shape=[jax.ShapeDtypeStruct((bp, t, S5_WIDTH), BF16),
                   jax.ShapeDtypeStruct((bp, S5_LANES), F32),
                   jax.ShapeDtypeStruct((bp, S5_LANES), F32)],
        scratch_shapes=[pltpu.VMEM((S5_NBLK, rows, S5_BLK_LANES), F32), pltpu.VMEM((S5_NBLK, rows, S5_BLK_LANES), F32),
                        pltpu.VMEM((nb, S5_LANES), F32), pltpu.VMEM((nb, S5_LANES), F32)],
        compiler_params=_params("parallel", "arbitrary"),
        name="s5",
    )(u3, s_re, s_im, *consts)
    return y[:b], o_re[:b], o_im[:b]


SB_KEYS = 384
SB_CACHE_KEYS = 256
SB_PAIRS = SB_WIDTH // LANES
SB_DEAD = -105.0


def _attn_body(*refs, bq, n_hist, pad):
    if n_hist:
        q_ref, upper_ref, kn_ref, vn_ref, kh_ref, vh_ref, o_ref, kall_ref, vall_ref, acc_ref, carry_ref = refs
    else:
        q_ref, upper_ref, kn_ref, vn_ref, o_ref, kall_ref, vall_ref, acc_ref, carry_ref = refs
    qi = pl.program_id(1)

    @pl.when(qi == 0)
    def _():
        for all_ref, new_ref, hist_ref in ((kall_ref, kn_ref, kh_ref if n_hist else None),
                                           (vall_ref, vn_ref, vh_ref if n_hist else None)):
            if pad:
                all_ref[0:pad, :] = jnp.zeros((pad, SB_WIDTH), BF16)
            if n_hist:
                all_ref[pad:pad + n_hist, :] = hist_ref[0].astype(BF16)
            all_ref[pad + n_hist:, :] = new_ref[0]

    k_ref, v_ref = kall_ref, vall_ref
    q = q_ref[0]
    lane = lax.broadcasted_iota(jnp.int32, (bq, LANES), 1)
    left = lane < SB_HEAD_DIM
    zero_b = jnp.zeros((), BF16)
    qms = [jnp.concatenate([jnp.where(left, q[:, p * LANES:(p + 1) * LANES], zero_b),
                            jnp.where(left, zero_b, q[:, p * LANES:(p + 1) * LANES])], axis=0)
           for p in range(SB_PAIRS)]
    ri = lax.broadcasted_iota(jnp.int32, (2 * bq, SB_KEYS), 0)
    ci = lax.broadcasted_iota(jnp.int32, (2 * bq, SB_KEYS), 1)
    ri = jnp.where(ri >= bq, ri - bq, ri)
    strictly_earlier = ci - (SB_KEYS - bq) < ri
    upper = upper_ref[...]
    n_before = n_hist + (qi + 1) * bq
    end = pad + n_before
    trips = (n_before + SB_KEYS - 1) // SB_KEYS
    acc_ref[...] = jnp.zeros_like(acc_ref)
    carry_ref[...] = jnp.zeros_like(carry_ref)

    def visit(it, masked):
        rows = pl.ds(pl.multiple_of(end - (it + 1) * SB_KEYS, 16), SB_KEYS)
        _sb_visit([_dot_nt(qms[p], k_ref[rows, p * LANES:(p + 1) * LANES]) for p in range(SB_PAIRS)],
                  lambda p, w: _dot(w, v_ref[rows, p * LANES:(p + 1) * LANES]),
                  strictly_earlier if masked else None, upper, acc_ref, carry_ref, left, bq)

    def any_weight_left():
        c = carry_ref[0]
        for p in range(1, SB_PAIRS):
            c = jnp.maximum(c, carry_ref[p])
        return jnp.max(c) > SB_DEAD

    visit(0, True)

    def trip(state):
        it, _ = state
        visit(it, False)
        return it + 1, any_weight_left()

    lax.while_loop(lambda s: jnp.logical_and(s[0] < trips, s[1]), trip, (jnp.int32(1), any_weight_left()))
    o_ref[0] = acc_ref[...].astype(BF16)


BF16_ROWS = 16


def _attn(qb, kb, vb, kh, vh):
    b, t, _ = qb.shape
    bq = min(t, 128)
    n_hist = 0 if kh is None else kh.shape[1]
    assert n_hist % BF16_ROWS == 0 and bq % BF16_ROWS == 0
    short = max(-(n_hist + (qi + 1) * bq) % SB_KEYS for qi in range(t // bq))
    pad = -(-short // BF16_ROWS) * BF16_ROWS
    block = pl.BlockSpec((1, bq, SB_WIDTH), lambda i, j: (i, j, 0))
    new = pl.BlockSpec((1, t, SB_WIDTH), lambda i, j: (i, 0, 0))
    upper = jnp.tril(jnp.ones((SB_KEYS, SB_KEYS), BF16), -1)
    args, specs = [qb, upper, kb, vb], [block, _full(upper.shape), new, new]
    if n_hist:
        hmap = (lambda i, j: (0, 0, 0)) if kh.shape[0] == 1 else (lambda i, j: (i, 0, 0))
        args += [kh, vh]
        specs += [pl.BlockSpec((1, n_hist, SB_WIDTH), hmap)] * 2
    rows = pad + n_hist + t
    return pl.pallas_call(
        functools.partial(_attn_body, bq=bq, n_hist=n_hist, pad=pad),
        grid=(b, t // bq),
        in_specs=specs,
        out_specs=block,
        out_shape=jax.ShapeDtypeStruct((b, t, SB_WIDTH), BF16),
        scratch_shapes=[pltpu.VMEM((rows, SB_WIDTH), BF16), pltpu.VMEM((rows, SB_WIDTH), BF16),
                        pltpu.VMEM((bq, SB_WIDTH), F32), pltpu.VMEM((SB_PAIRS, 2 * bq, 1), F32)],
        compiler_params=_params("parallel", "arbitrary"),
        name="attn",
    )(*args)


def _sb_visit(zs, values, mask, upper, acc_ref, carry_ref, left, bq):
    log_ws, totals = [], []
    for z in zs:
        sp = jnp.log(1.0 + jnp.exp(-jnp.abs(z)))
        log_beta = jnp.minimum(z, 0.0) - sp
        log_keep = log_beta - z
        if mask is not None:
            log_keep = jnp.where(mask, log_keep, 0.0)
        hi = log_keep.astype(BF16)
        lo = (log_keep - hi.astype(F32)).astype(BF16)
        both = _dot(jnp.concatenate([hi, lo], axis=0), upper)
        later = both[:2 * bq] + both[2 * bq:]
        log_ws.append(log_beta + later)
        totals.append(later[:, 0:1] + log_keep[:, 0:1])
    for p in range(SB_PAIRS):
        carry = carry_ref[p]
        w = jnp.exp(log_ws[p] + carry)
        if mask is not None:
            w = jnp.where(mask, w, 0.0)
        pv = values(p, w.astype(BF16))
        carry_ref[p] = carry + totals[p]
        acc_ref[:, p * LANES:(p + 1) * LANES] += jnp.where(left, pv[:bq], pv[bq:])


def _attn_cached_body(q_ref, upper_ref, upper_new_ref, kn_ref, vn_ref, kh_ref, vh_ref, o_ref, acc_ref, carry_ref, *, bq, n_hist):
    q = q_ref[0]
    lane = lax.broadcasted_iota(jnp.int32, (bq, LANES), 1)
    left = lane < SB_HEAD_DIM
    zero_b = jnp.zeros((), BF16)
    qms = [jnp.concatenate([jnp.where(left, q[:, p * LANES:(p + 1) * LANES], zero_b),
                            jnp.where(left, zero_b, q[:, p * LANES:(p + 1) * LANES])], axis=0)
           for p in range(SB_PAIRS)]
    ri = lax.broadcasted_iota(jnp.int32, (2 * bq, bq), 0)
    ci = lax.broadcasted_iota(jnp.int32, (2 * bq, bq), 1)
    strictly_earlier = ci < jnp.where(ri >= bq, ri - bq, ri)
    upper = upper_ref[...]
    acc_ref[...] = jnp.zeros_like(acc_ref)
    carry_ref[...] = jnp.zeros_like(carry_ref)
    pair = lambda p: slice(p * LANES, (p + 1) * LANES)

    _sb_visit([_dot_nt(qms[p], kn_ref[0, :, pair(p)]) for p in range(SB_PAIRS)],
              lambda p, w: _dot(w, vn_ref[0, :, pair(p)]),
              strictly_earlier, upper_new_ref[...], acc_ref, carry_ref, left, bq)

    def any_weight_left():
        c = carry_ref[0]
        for p in range(1, SB_PAIRS):
            c = jnp.maximum(c, carry_ref[p])
        return jnp.max(c) > SB_DEAD

    def trip(state):
        it, _ = state
        window = pl.ds(pl.multiple_of(n_hist - (it + 1) * SB_CACHE_KEYS, SB_CACHE_KEYS), SB_CACHE_KEYS)
        kt = kh_ref[0, 0, :, :, window].reshape(SB_WIDTH, SB_CACHE_KEYS).astype(BF16)
        vt = vh_ref[0, 0, :, :, window].reshape(SB_WIDTH, SB_CACHE_KEYS).astype(BF16)
        _sb_visit([_dot(qms[p], kt[pair(p)]) for p in range(SB_PAIRS)],
                  lambda p, w: _dot_nt(w, vt[pair(p)]),
                  None, upper, acc_ref, carry_ref, left, bq)
        return it + 1, any_weight_left()

    lax.while_loop(lambda s: jnp.logical_and(s[0] < n_hist // SB_CACHE_KEYS, s[1]), trip,
                   (jnp.int32(0), any_weight_left()))
    o_ref[0] = acc_ref[...].astype(BF16)


def _attn_cached(qb, kb, vb, cache_k, cache_v, layer):
    b, t, _ = qb.shape
    n_hist = cache_k.shape[2]
    assert t <= 128 and t % BF16_ROWS == 0 and n_hist % SB_CACHE_KEYS == 0
    upper = jnp.tril(jnp.ones((SB_CACHE_KEYS, SB_CACHE_KEYS), BF16), -1)
    upper_new = jnp.tril(jnp.ones((t, t), BF16), -1)
    block = pl.BlockSpec((1, t, SB_WIDTH), lambda i: (i, 0, 0))
    hist =pl.BlockSpec((1, 1, SB_HEADS, SB_HEAD_DIM, n_hist), lambda i: (layer, i, 0, 0, 0))
    minor = lambda c: jnp.transpose(c, (0, 1, 3, 4, 2))
    return pl.pallas_call(
        functools.partial(_attn_cached_body, bq=t, n_hist=n_hist),
        grid=(b,),
        in_specs=[block, _full(upper.shape), _full(upper_new.shape), block, block, hist, hist],
        out_specs=block,
        out_shape=jax.ShapeDtypeStruct((b, t, SB_WIDTH), BF16),
        scratch_shapes=[pltpu.VMEM((t, SB_WIDTH), F32), pltpu.VMEM((SB_PAIRS, 2 * t, 1), F32)],
        compiler_params=_params("parallel"),
        name="attn_cached",
    )(qb, upper, upper_new, kb, vb, minor(cache_k), minor(cache_v))


def _kv_rows_body(*refs, b, t, depth):
    ko_ref, vo_ref = refs[4 * depth:]
    layer = pl.program_id(0) // b
    for l in range(depth):
        km_ref, vm_ref, kn_ref, vn_ref = refs[4 * l:4 * l + 4]
        n_meta = km_ref.shape[0]
        total = n_meta + t

        @pl.when(layer == l)
        def _(km_ref=km_ref, vm_ref=vm_ref, kn_ref=kn_ref, vn_ref=vn_ref, n_meta=n_meta, total=total):
            for meta_ref, new_ref, out_ref in ((km_ref, kn_ref, ko_ref), (vm_ref, vn_ref, vo_ref)):
                for c in range(-(-total // LANES)):
                    lo, hi = c * LANES, min((c + 1) * LANES, total)
                    parts = []
                    if lo < n_meta:
                        parts.append(meta_ref[lo:min(hi, n_meta), :])
                    if hi > n_meta:
                        parts.append(new_ref[max(lo, n_meta) - n_meta:hi - n_meta, :])
                    if hi - lo < LANES:
                        parts.append(jnp.zeros((LANES - (hi - lo), SB_WIDTH), F32))
                    blk = parts[0] if len(parts) == 1 else jnp.concatenate(parts, axis=0)
                    blk_t = blk.T.reshape(SB_HEADS, SB_HEAD_DIM, LANES)
                    out_ref[0, 0, :, :, lo:hi] = blk_t[:, :, :hi - lo]


def _kv_rows(meta_rows, new_rows, b):
    depth = len(new_rows)
    n = new_rows[0][0].shape[0]
    t = n // b
    n_meta = meta_rows[0][0].shape[0]
    args, specs = [], []
    for l in range(depth):
        rows = pl.BlockSpec((t, SB_WIDTH), lambda i, l=l: (jnp.clip(i - l * b, 0, b - 1), 0))
        args += [*meta_rows[l], *new_rows[l]]
        specs += [_full((n_meta, SB_WIDTH))] * 2 + [rows] * 2
    out = pl.BlockSpec((1, 1, SB_HEADS, SB_HEAD_DIM, n_meta + t), lambda i: (i // b, i % b, 0, 0, 0))
    shape = jax.ShapeDtypeStruct((depth, b, SB_HEADS, SB_HEAD_DIM, n_meta + t), F32)
    k_out, v_out = pl.pallas_call(
        functools.partial(_kv_rows_body, b=b, t=t, depth=depth),
        grid=(depth * b,),
        in_specs=specs,
        out_specs=[out, out],
        out_shape=[shape, shape],
        compiler_params=_params("arbitrary"),
        name="kv_rows",
    )(*args)
    return jnp.transpose(k_out, (0, 1, 4, 2, 3)), jnp.transpose(v_out, (0, 1, 4, 2, 3))


def _merge_body(x_ref, ya_ref, yb_ref, oc_ref, g_ref, wg_ref, wa_ref, wglu_ref, wc_ref, wo_ref, o_ref):
    x = x_ref[...]
    hb = _rmsnorm_rows(x, g_ref[...]).astype(BF16)
    gates = _sigmoid(_dot(hb, wg_ref[...]))
    glu = _dot(yb_ref[...], wglu_ref[...])
    mix = (gates[:, :D_MODEL] * _dot(ya_ref[...], wa_ref[...])
           + gates[:, D_MODEL:2 * D_MODEL] * (glu[:, :D_MODEL] * _sigmoid(glu[:, D_MODEL:]))
           + gates[:, 2 * D_MODEL:] * _dot(oc_ref[...], wc_ref[...]))
    o_ref[...] = x + _dot(mix.astype(BF16), wo_ref[...])


def _merge(x2, ya, yb, oc, lw):
    n = x2.shape[0]
    tm = _row_tile(n, 512)
    rows = lambda w: pl.BlockSpec((tm, w), lambda i: (i, 0))
    consts = (lw['norm_mix'], lw['wgate'], lw['w_lift_a'], lw['w_glu'], lw['w_lift_c'], lw['w_out'])
    return pl.pallas_call(
        _merge_body,
        grid=(n // tm,),
        in_specs=[rows(D_MODEL), rows(SSD_WIDTH), rows(S5_WIDTH), rows(SB_WIDTH)]
                 + [_full(a.shape) for a in consts],
        out_specs=rows(D_MODEL),
        out_shape=jax.ShapeDtypeStruct((n, D_MODEL), F32),
        compiler_params=_params("parallel"),
        name="merge",
    )(x2, ya, yb, oc, *consts)


def _merge_ffn_body(x_ref, ya_ref, yb_ref, oc_ref, g_ref, wg_ref, wa_ref, wglu_ref, wc_ref, wo_ref,
                    g2_ref, wu_ref, wd_ref, o_ref, x1_ref):
    _merge_body(x_ref, ya_ref, yb_ref, oc_ref, g_ref, wg_ref, wa_ref, wglu_ref, wc_ref, wo_ref, x1_ref)
    _ffn_body(x1_ref, g2_ref, wu_ref, wd_ref, o_ref)


def _merge_ffn(x2, ya, yb, oc, lw):
    n = x2.shape[0]
    tm = _row_tile(n, MERGE_FFN_ROWS)
    rows = lambda w: pl.BlockSpec((tm, w), lambda i: (i, 0))
    consts = (lw['norm_mix'], lw['wgate'], lw['w_lift_a'], lw['w_glu'], lw['w_lift_c'], lw['w_out'],
              lw['norm_ffn'], lw['w_up'], lw['w_down'])
    return pl.pallas_call(
        _merge_ffn_body,
        grid=(n // tm,),
        in_specs=[rows(D_MODEL), rows(SSD_WIDTH), rows(S5_WIDTH), rows(SB_WIDTH)]
                 + [_full(a.shape) for a in consts],
        out_specs=rows(D_MODEL),
        out_shape=jax.ShapeDtypeStruct((n, D_MODEL), F32),
        scratch_shapes=[pltpu.VMEM((tm, D_MODEL), F32)],
        compiler_params=_params("parallel"),
        name="merge_ffn",
    )(x2, ya, yb, oc, *consts)


MERGE_FFN_ROWS = 512


FF_CHUNK = 1024


def _ffn_body(x_ref, g_ref, wu_ref, wd_ref, o_ref):
    x = x_ref[...]
    hb = _rmsnorm_rows(x, g_ref[...]).astype(BF16)
    acc = x
    for j in range(D_FF // FF_CHUNK):
        cols = slice(j * FF_CHUNK, (j + 1) * FF_CHUNK)
        up = jnp.maximum(_dot(hb, wu_ref[:, cols]), 0.0)
        acc = acc + _dot((up * up).astype(BF16), wd_ref[cols, :])
    o_ref[...] = acc


def _ffn(x2, lw):
    n = x2.shape[0]
    tm = _row_tile(n, 1024)
    rows = pl.BlockSpec((tm, D_MODEL), lambda i: (i, 0))
    consts = (lw['norm_ffn'], lw['w_up'], lw['w_down'])
    return pl.pallas_call(
        _ffn_body,
        grid=(n // tm,),
        in_specs=[rows] + [_full(a.shape) for a in consts],
        out_specs=rows,
        out_shape=jax.ShapeDtypeStruct((n, D_MODEL), F32),
        compiler_params=_params("parallel"),
        name="ffn",
    )(x2, *consts)


def _block_diag(blocks):
    n, r, c = blocks.shape
    eye = jnp.eye(n, dtype=blocks.dtype)
    return (eye[:, None, :, None] * blocks[:, :, None, :]).reshape(n * r, n * c)


def _layer_weights(p, l):
    w_in = p['w_in'][l]
    col = lambda off, n: w_in[:, off:off + n].astype(BF16)
    row = lambda v: v.reshape(1, -1).astype(F32)
    heads = jnp.arange(SSD_WIDTH) // SSD_HEAD_DIM
    pad_lanes = lambda v: jnp.pad(v, (0, LANES - v.shape[0])).reshape(1, LANES)
    ab_re, ab_im, bb_re, bb_im = _s5_params(p['lam_re'][l], p['lam_im'][l], p['log_step'][l],
                                            p['b_re'][l], p['b_im'][l])
    to_bb = lambda t: jnp.stack([_block_diag(blk) for blk in
                                 t.reshape(S5_NBLK, S5_BLK_GROUPS, S5_GROUP_CH, S5_STATE)]).astype(BF16)
    to_cc = lambda t: jnp.stack([_block_diag(blk) for blk in
                                 jnp.transpose(t, (0, 2, 1)).reshape(S5_NBLK, S5_BLK_GROUPS, S5_STATE, S5_GROUP_CH)]
                                ).astype(BF16)
    state_rows = jnp.arange(LANES) // SSD_STATE
    state_cols = jnp.arange(SSD_WIDTH) // (SSD_WIDTH // SSD_GROUPS)
    return {
        'norm_mix': row(p['norm_mix'][l]),
        'wz': col(OFF_Z, SSD_WIDTH), 'wxbc': col(OFF_XBC, CONV_DIM),
        'wdt': jnp.pad(col(OFF_DT, SSD_HEADS), ((0, 0), (0, LANES - SSD_HEADS))),
        'wu': col(OFF_U, S5_WIDTH), 'wq': col(OFF_Q, SB_WIDTH), 'wk': col(OFF_K, SB_WIDTH),
        'wv': col(OFF_V, SB_WIDTH), 'wgate': col(OFF_GATE, N_BRANCH * D_MODEL),
        'q_norm': row(jnp.tile(p['q_norm'][l], SB_HEADS)), 'k_norm': row(jnp.tile(p['k_norm'][l], SB_HEADS)),
        'head_mean': (_block_diag(jnp.ones((SB_HEADS, SB_HEAD_DIM, SB_HEAD_DIM), F32)) / SB_HEAD_DIM).astype(BF16),
        'conv_w': jnp.pad(p['conv_w'][l], ((0, SUBLANES - CONV_W), (0, 0))),
        'conv_b': row(p['conv_b'][l]),
        'dt_bias': pad_lanes(p['dt_bias'][l]), 'a_log': pad_lanes(p['a_log'][l]),
        'd_ssd': row(p['d_ssd'][l][heads]), 'norm_ssd': row(p['norm_ssd'][l]),
        'eye': jnp.eye(LANES, dtype=BF16),
        'state_mask': (state_rows[:, None] == state_cols[None, :]).astype(F32),
        's5_ab_re': ab_re[::S5_GROUP_CH].reshape(1, S5_LANES), 's5_ab_im': ab_im[::S5_GROUP_CH].reshape(1, S5_LANES),
        's5_bb_re': to_bb(bb_re), 's5_bb_im': to_bb(bb_im),
        's5_cc_re': to_cc(p['c_re'][l]), 's5_cc_im': to_cc(p['c_im'][l]),
        'd_s5': row(p['d_s5'][l]),
        'w_glu': p['w_glu'][l].astype(BF16), 'w_lift_a': p['w_lift_a'][l].astype(BF16),
        'w_lift_c': p['w_lift_c'][l].astype(BF16), 'w_out': p['w_out'][l].astype(BF16),
        'norm_ffn': row(p['norm_ffn'][l]), 'w_up': p['w_up'][l].astype(BF16), 'w_down': p['w_down'][l].astype(BF16),
    }


def _trunk_layer(x, k_hist, v_hist, conv_hist, ssd_s0, s5_re0, s5_im0, lw, cache_layer=None):
    b, t, _ = x.shape
    x2 = x.reshape(b * t, D_MODEL)
    z, xbc, dt, u, k, v, qb, kb, vb = _proj(x2, lw)
    seq = lambda a: a.reshape(b, t, a.shape[-1])
    hist8 = jnp.pad(conv_hist.astype(F32), ((0, 0), (SUBLANES - (CONV_W - 1), 0), (0, 0)))
    y_a, ssd_new = _ssd(seq(z), seq(xbc), seq(dt), hist8,
                        ssd_s0.astype(F32).reshape(-1, SSD_HEADS, SSD_HEAD_DIM, SSD_STATE), lw)
    y_b, s5_re, s5_im = _s5(seq(u), s5_re0.reshape(-1, S5_LANES).astype(F32),
                            s5_im0.reshape(-1, S5_LANES).astype(F32), lw)
    if cache_layer is not None:
        o_c = _attn_cached(seq(qb), seq(kb), seq(vb), k_hist.astype(F32), v_hist.astype(F32), cache_layer)
    elif k_hist is None:
        o_c = _attn(seq(qb), seq(kb), seq(vb), None, None)
    else:
        o_c = _attn(seq(qb), seq(kb), seq(vb), k_hist.reshape(k_hist.shape[0], -1, SB_WIDTH).astype(F32),
                    v_hist.reshape(v_hist.shape[0], -1, SB_WIDTH).astype(F32))
    x2 = _merge_ffn(x2, y_a.reshape(b * t, SSD_WIDTH), y_b.reshape(b * t, S5_WIDTH), o_c.reshape(b * t, SB_WIDTH), lw)
    conv_rows = jnp.concatenate([jnp.broadcast_to(conv_hist.astype(F32), (b, CONV_W - 1, CONV_DIM)),
                                 seq(xbc)[:, -(CONV_W - 1):]], axis=1)[:, -(CONV_W - 1):]
    return (x2.reshape(b, t, D_MODEL), k, v, conv_rows,
            ssd_new.reshape(b, SSD_GROUPS, SSD_HPG, SSD_HEAD_DIM, SSD_STATE),
            s5_re.reshape(b, S5_GROUPS, S5_STATE), s5_im.reshape(b, S5_GROUPS, S5_STATE))


def kernel(x_prompt, x_sample, cache_k, cache_v, state_conv, state_ssd, state_s5_re, state_s5_im, meta_tokens, norm_mix, w_in, conv_w, conv_b, dt_bias, a_log, d_ssd, norm_ssd, lam_re, lam_im, log_step, b_re, b_im, c_re, c_im, d_s5, w_glu, q_norm, k_norm, w_lift_a, w_lift_c, w_out, norm_ffn, w_up, w_down):
    p = dict(norm_mix=norm_mix, w_in=w_in, conv_w=conv_w, conv_b=conv_b, dt_bias=dt_bias, a_log=a_log,
             d_ssd=d_ssd, norm_ssd=norm_ssd, lam_re=lam_re, lam_im=lam_im, log_step=log_step, b_re=b_re,
             b_im=b_im, c_re=c_re, c_im=c_im, d_s5=d_s5, w_glu=w_glu, q_norm=q_norm, k_norm=k_norm,
             w_lift_a=w_lift_a, w_lift_c=w_lift_c, w_out=w_out, norm_ffn=norm_ffn, w_up=w_up, w_down=w_down)
    depth = w_in.shape[0]
    bp = x_prompt.shape[0]
    bs, ts, _ = x_sample.shape
    xm = meta_tokens.astype(x_prompt.dtype)[None]
    xp, xs = x_prompt, x_sample
    zeros = lambda *s: jnp.zeros(s, F32)
    kv_m, kv_p, kv_s, outs_p, outs_s = [], [], [], [], []
    for l in range(depth):
        lw = _layer_weights(p, l)
        xm, k_m, v_m, conv_m, ssd_m, s5re_m, s5im_m = _trunk_layer(
            xm, None, None, zeros(1, CONV_W - 1, CONV_DIM), zeros(1, SSD_GROUPS, SSD_HPG, SSD_HEAD_DIM, SSD_STATE),
            zeros(1, S5_GROUPS, S5_STATE), zeros(1, S5_GROUPS, S5_STATE), lw)
        xp, k_p, v_p, *st_p = _trunk_layer(xp, k_m[None], v_m[None], conv_m, ssd_m, s5re_m, s5im_m, lw)
        xs, k_s, v_s, *st_s = _trunk_layer(xs, cache_k, cache_v, state_conv[l], state_ssd[l],
                                           state_s5_re[l], state_s5_im[l], lw, cache_layer=l)
        kv_m.append((k_m, v_m))
        kv_p.append((k_p, v_p))
        kv_s.append((k_s, v_s))
        outs_p.append(st_p)
        outs_s.append(st_s)
    stk = lambda outs, i: jnp.stack([o[i] for o in outs], axis=0)
    k_prompt, v_prompt = _kv_rows(kv_m, kv_p, bp)
    sample_rows = lambda i: stk(kv_s, i).reshape(depth, bs, ts, SB_HEADS, SB_HEAD_DIM)
    return (xp, xs,
            k_prompt, v_prompt, stk(outs_p, 0), stk(outs_p, 1), stk(outs_p, 2), stk(outs_p, 3),
            sample_rows(0), sample_rows(1), stk(outs_s, 0), stk(outs_s, 1), stk(outs_s, 2), stk(outs_s, 3))
```
